```python
import jax, jax.numpy as jnp
from jax import lax
import numpy as np

D_MODEL = 2048
BATCH = 4
SEQ = 2048
DEPTH = 1

NSA_HEADS = 16
NSA_KV_GROUPS = 4
NSA_HEAD_DIM = 128
CMP_BLOCK = 32
CMP_STRIDE = 16
SLC_BLOCK = 64
SLC_TOP_N = 16
WINDOW = 512
WIN_Q_BLOCK = 128
SLC_Q_CHUNK = 32
RET_HEADS = 8
RET_KEY_DIM = 128
RET_VAL_DIM = 256
RET_CHUNK = 128
D_FF = 5632
N_ADA = 9
EPS = 1e-6
SEL_FORCE = 1e4

NSA_WIDTH = NSA_HEADS * NSA_HEAD_DIM
KV_WIDTH = NSA_KV_GROUPS * NSA_HEAD_DIM
RET_QK_WIDTH = RET_HEADS * RET_KEY_DIM
RET_V_WIDTH = RET_HEADS * RET_VAL_DIM
IN_SPLITS = (NSA_WIDTH, 6 * KV_WIDTH, 3 * NSA_HEADS, RET_QK_WIDTH, RET_QK_WIDTH, RET_V_WIDTH, RET_V_WIDTH, D_MODEL, D_MODEL)
N_IN = sum(IN_SPLITS)

kernel_name = 'hybrid_nsa_retention_macaron'


def rms_norm(x, g):
    xf = x.astype(jnp.float32)
    y = xf * lax.rsqrt(jnp.mean(xf * xf, -1, keepdims=True) + EPS)
    return (y * g.astype(jnp.float32)).astype(x.dtype)


def modulate(x, g, shift, scale):
    return rms_norm(x, g) * (1 + scale) + shift


def swiglu(x, w_gate, w_up, w_down):
    return (jax.nn.silu(x @ w_gate) * (x @ w_up)) @ w_down


def masked_softmax(s, mask):
    s = jnp.where(mask, s.astype(jnp.float32), -jnp.inf)
    m = jnp.max(s, -1, keepdims=True)
    m = jnp.where(jnp.isfinite(m), m, 0.0)
    p = jnp.where(mask, jnp.exp(s - m), 0.0)
    return p / jnp.maximum(jnp.sum(p, -1, keepdims=True), jnp.finfo(jnp.float32).tiny)


def alibi_slopes(n):
    return jnp.exp2(-8.0 * jnp.arange(1, n + 1, dtype=jnp.float32) / n)


def nsa_attention(q, kv, gate_logits, g_qk, cmp_pos, cmp_w1, cmp_b1, cmp_w2):
    B, S, _ = q.shape
    G, dh = NSA_KV_GROUPS, NSA_HEAD_DIM
    Hg = NSA_HEADS // G
    scale = dh ** -0.5
    t = jnp.arange(S)
    slopes = alibi_slopes(NSA_HEADS).reshape(G, Hg)
    q = rms_norm(q.reshape(B, S, G, Hg, dh), g_qk[0])
    kv = kv.reshape(B, S, 6, G, dh)
    k_cmp_raw, v_cmp_raw, k_slc, v_slc, k_win, v_win = [kv[:, :, i] for i in range(6)]
    k_slc = rms_norm(k_slc, g_qk[2])
    k_win = rms_norm(k_win, g_qk[3])

    n_cmp = (S - CMP_BLOCK) // CMP_STRIDE + 1
    starts = jnp.arange(n_cmp) * CMP_STRIDE
    cidx = starts[:, None] + jnp.arange(CMP_BLOCK)[None, :]

    def compress(z, i):
        blk = z[:, cidx] + cmp_pos[i][:, None, :]
        blk = blk.transpose(0, 3, 1, 2, 4).reshape(B, G, n_cmp, CMP_BLOCK * dh)
        return jax.nn.gelu(blk @ cmp_w1[i] + cmp_b1[i]) @ cmp_w2[i]

    k_cmp = rms_norm(compress(k_cmp_raw, 0), g_qk[1])
    v_cmp = compress(v_cmp_raw, 1)
    s_c = jnp.einsum('bsghd,bgcd->bghsc', q, k_cmp).astype(jnp.float32) * scale
    centre = (starts + (CMP_BLOCK - 1) / 2).astype(jnp.float32)
    s_c = s_c - slopes[:, :, None, None] * (t[:, None].astype(jnp.float32) - centre[None, :])
    p_c = masked_softmax(s_c, (starts + CMP_BLOCK - 1)[None, :] <= t[:, None])
    o_cmp = jnp.einsum('bghsc,bgcd->bsghd', p_c.astype(v_cmp.dtype), v_cmp)

    n_slc = S // SLC_BLOCK
    top_n = min(SLC_TOP_N, n_slc)
    cs = starts[:, None]
    js = (jnp.arange(n_slc) * SLC_BLOCK)[None, :]
    overlap = jnp.clip(jnp.minimum(cs + CMP_BLOCK, js + SLC_BLOCK) - jnp.maximum(cs, js), 0, None)
    overlap = overlap.astype(jnp.float32) / CMP_BLOCK
    imp = jnp.einsum('bghsc,cj->bgsj', p_c, overlap)
    blk_t = t // SLC_BLOCK
    jj = jnp.arange(n_slc)[None, :]
    forced = (jj == 0) | (jj == blk_t[:, None]) | (jj == blk_t[:, None] - 1)
    imp = jnp.where(forced, SEL_FORCE, jnp.where(jj <= blk_t[:, None], imp, -SEL_FORCE))
    _, sel = lax.top_k(imp, top_n)

    kb = k_slc.reshape(B, n_slc, SLC_BLOCK, G, dh).transpose(0, 3, 1, 2, 4)
    vb = v_slc.reshape(B, n_slc, SLC_BLOCK, G, dh).transpose(0, 3, 1, 2, 4)
    nq = S // SLC_Q_CHUNK
    q_ch = q.reshape(B, nq, SLC_Q_CHUNK, G, Hg, dh).transpose(1, 0, 2, 3, 4, 5)
    sel_ch = sel.reshape(B, G, nq, SLC_Q_CHUNK, top_n).transpose(2, 0, 1, 3, 4)
    t_ch = t.reshape(nq, SLC_Q_CHUNK)
    bi = jnp.arange(B)[:, None, None, None]
    gi = jnp.arange(G)[None, :, None, None]
    n_keys = top_n * SLC_BLOCK

    def slc_chunk(args):
        qc, selc, tc = args
        kg = kb[bi, gi, selc]
        vg = vb[bi, gi, selc]
        s = jnp.einsum('bqghd,bgqnrd->bghqnr', qc, kg).astype(jnp.float32) * scale
        pos = selc[..., None] * SLC_BLOCK + jnp.arange(SLC_BLOCK)
        dist = tc[None, None, :, None, None] - pos
        s = s - slopes[None, :, :, None, None, None] * dist[:, :, None].astype(jnp.float32)
        mask = (dist >= 0)[:, :, None].reshape(B, G, 1, SLC_Q_CHUNK, n_keys)
        p = masked_softmax(s.reshape(B, G, Hg, SLC_Q_CHUNK, n_keys), mask)
        p = p.reshape(B, G, Hg, SLC_Q_CHUNK, top_n, SLC_BLOCK)
        return jnp.einsum('bghqnr,bgqnrd->bqghd', p.astype(vg.dtype), vg)

    o_slc = lax.map(slc_chunk, (q_ch, sel_ch, t_ch))
    o_slc = o_slc.transpose(1, 0, 2, 3, 4, 5).reshape(B, S, G, Hg, dh)

    nb = S // WIN_Q_BLOCK
    span = WIN_Q_BLOCK + WINDOW
    kpad = jnp.pad(k_win, ((0, 0), (WINDOW, 0), (0, 0), (0, 0)))
    vpad = jnp.pad(v_win, ((0, 0), (WINDOW, 0), (0, 0), (0, 0)))
    widx = jnp.arange(nb)[:, None] * WIN_Q_BLOCK + jnp.arange(span)[None, :]
    kw = kpad[:, widx]
    vw = vpad[:, widx]
    qw = q.reshape(B, nb, WIN_Q_BLOCK, G, Hg, dh)
    s_w = jnp.einsum('bnqghd,bnkgd->bghnqk', qw, kw).astype(jnp.float32) * scale
    sk = widx - WINDOW
    dist_w = t.reshape(nb, WIN_Q_BLOCK)[:, :, None] - sk[:, None, :]
    mask_w = (dist_w >= 0) & (dist_w < WINDOW) & (sk[:, None, :] >= 0)
    s_w = s_w - slopes[:, :, None, None, None] * dist_w.astype(jnp.float32)
    p_w = masked_softmax(s_w, mask_w)
    o_win = jnp.einsum('bghnqk,bnkgd->bnqghd', p_w.astype(vw.dtype), vw).reshape(B, S, G, Hg, dh)

    gates = jax.nn.sigmoid(gate_logits.reshape(B, S, 3, G, Hg, 1))
    o = gates[:, :, 0] * o_cmp + gates[:, :, 1] * o_slc + gates[:, :, 2] * o_win
    return o.reshape(B, S, NSA_WIDTH)


def retention(q, k, v, g, gn_gain):
    B, S, _ = q.shape
    H, dk, dv, C = RET_HEADS, RET_KEY_DIM, RET_VAL_DIM, RET_CHUNK
    nc = S // C
    f32 = jnp.float32
    qc = q.astype(f32).reshape(B, nc, C, H, dk)
    kc = k.astype(f32).reshape(B, nc, C, H, dk) * (dk ** -0.5)
    vc = v.astype(f32).reshape(B, nc, C, H, dv)
    log_gamma = jnp.log1p(-jnp.exp2(-5.0 - jnp.arange(H, dtype=f32)))
    n = jnp.arange(C, dtype=f32)
    diff = n[:, None] - n[None, :]
    decay = jnp.where(diff >= 0, jnp.exp(log_gamma[:, None, None] * jnp.maximum(diff, 0.0)), 0.0)
    scores = jnp.einsum('bcnhd,bcmhd->bchnm', qc, kc) * decay
    inner = jnp.einsum('bchnm,bcmhe->bcnhe', scores, vc)
    zeta = jnp.exp(log_gamma[:, None] * (C - 1 - n)[None, :])
    kv = jnp.einsum('bcmhd,hm,bcmhe->cbhde', kc, zeta, vc)
    chunk_decay = jnp.exp(log_gamma * C)[None, :, None, None]

    def step(state, kv_i):
        return state * chunk_decay + kv_i, state

    _, prev = lax.scan(step, jnp.zeros((B, H, dk, dv), f32), kv)
    xi = jnp.exp(log_gamma[:, None] * (n + 1.0)[None, :])
    cross = jnp.einsum('bcnhd,cbhde,hn->bcnhe', qc, prev, xi)
    y = (inner + cross).reshape(B, S, H, dv)
    yc = y - jnp.mean(y, -1, keepdims=True)
    y = yc * lax.rsqrt(jnp.mean(yc * yc, -1, keepdims=True) + EPS) * gn_gain.astype(f32)
    out = jax.nn.silu(g.astype(f32).reshape(B, S, H, dv)) * y
    return out.reshape(B, S, RET_V_WIDTH).astype(q.dtype)


def setup_inputs(seed: int = 0) -> dict:
    key = jax.random.key(seed)
    ks = jax.random.split(key, 20)
    f32 = jnp.float32
    dh = NSA_HEAD_DIM

    def nrm(k, shape, fan_in):
        return jax.random.normal(k, shape, f32) * (fan_in ** -0.5)

    return {
        'x': jax.random.normal(ks[0], (BATCH, SEQ, D_MODEL), f32),
        'c': jax.random.normal(ks[1], (BATCH, D_MODEL), f32),
        'w_ada': 0.5 * nrm(ks[2], (DEPTH, D_MODEL, N_ADA * D_MODEL), D_MODEL),
        'b_ada': 0.01 * jax.random.normal(ks[3], (DEPTH, N_ADA * D_MODEL), f32),
        'g_norm': 1.0 + 0.02 * jax.random.normal(ks[4], (DEPTH, 3, D_MODEL), f32),
        'w_ffn_gate': nrm(ks[5], (DEPTH, 2, D_MODEL, D_FF), D_MODEL),
        'w_ffn_up': nrm(ks[6], (DEPTH, 2, D_MODEL, D_FF), D_MODEL),
        'w_ffn_down': nrm(ks[7], (DEPTH, 2, D_FF, D_MODEL), D_FF),
        'w_in': nrm(ks[8], (DEPTH, D_MODEL, N_IN), D_MODEL),
        'g_qk': 1.0 + 0.02 * jax.random.normal(ks[9], (DEPTH, 4, dh), f32),
        'cmp_pos': 0.02 * jax.random.normal(ks[10], (DEPTH, 2, CMP_BLOCK, dh), f32),
        'cmp_w1': nrm(ks[11], (DEPTH, 2, CMP_BLOCK * dh, dh), CMP_BLOCK * dh),
        'cmp_b1': 0.01 * jax.random.normal(ks[12], (DEPTH, 2, dh), f32),
        'cmp_w2': nrm(ks[13], (DEPTH, 2, dh, dh), dh),
        'ret_gn_gain': 1.0 + 0.02 * jax.random.normal(ks[14], (DEPTH, RET_HEADS, RET_VAL_DIM), f32),
        'w_proj_nsa': nrm(ks[15], (DEPTH, NSA_WIDTH, D_MODEL), NSA_WIDTH),
        'w_proj_ret': nrm(ks[16], (DEPTH, RET_V_WIDTH, D_MODEL), RET_V_WIDTH),
        'w_out': nrm(ks[17], (DEPTH, D_MODEL, D_MODEL), D_MODEL),
    }


def reference(x, c, w_ada, b_ada, g_norm, w_ffn_gate, w_ffn_up, w_ffn_down, w_in, g_qk, cmp_pos, cmp_w1, cmp_b1, cmp_w2, ret_gn_gain, w_proj_nsa, w_proj_ret, w_out):
    B, S, D = x.shape
    cond = jax.nn.silu(c)
    split_at = tuple(np.cumsum(IN_SPLITS)[:-1].tolist())
    for l in range(DEPTH):
        ada = (cond @ w_ada[l] + b_ada[l]).reshape(B, N_ADA, 1, D)
        sh1, sc1, gt1, sh2, sc2, gt2, sh3, sc3, gt3 = [ada[:, i] for i in range(N_ADA)]
        h = modulate(x, g_norm[l, 0], sh1, sc1)
        x = x + 0.5 * gt1 * swiglu(h, w_ffn_gate[l, 0], w_ffn_up[l, 0], w_ffn_down[l, 0])
        u = modulate(x, g_norm[l, 1], sh2, sc2)
        q_nsa, kv_nsa, gl_nsa, q_r, k_r, v_r, g_r, ga, gb = jnp.split(u @ w_in[l], split_at, axis=-1)
        o_nsa = nsa_attention(q_nsa, kv_nsa, gl_nsa, g_qk[l], cmp_pos[l], cmp_w1[l], cmp_b1[l], cmp_w2[l])
        o_ret = retention(q_r, k_r, v_r, g_r, ret_gn_gain[l])
        merged = jax.nn.sigmoid(ga) * (o_nsa @ w_proj_nsa[l]) + jax.nn.sigmoid(gb) * (o_ret @ w_proj_ret[l])
        x = x + gt2 * (merged @ w_out[l])
        h = modulate(x, g_norm[l, 2], sh3, sc3)
        x = x + 0.5 * gt3 * swiglu(h, w_ffn_gate[l, 1], w_ffn_up[l, 1], w_ffn_down[l, 1])
    return x
```

```python
import functools

import numpy as np
import jax
import jax.numpy as jnp
from jax import lax
from jax.experimental import pallas as pl
from jax.experimental.pallas import tpu as pltpu

F32 = jnp.float32
BF16 = jnp.bfloat16

D_MODEL = 2048
NSA_HEADS = 16
NSA_KV_GROUPS = 4
HEADS_PER_GROUP = NSA_HEADS // NSA_KV_GROUPS
NSA_HEAD_DIM = 128
CMP_BLOCK = 32
CMP_STRIDE = 16
SLC_BLOCK = 64
SLC_TOP_N = 16
WINDOW = 512
RET_HEADS = 8
RET_KEY_DIM = 128
RET_VAL_DIM = 256
RET_CHUNK = 128
D_FF = 5632
N_ADA = 9
EPS = 1e-6
SEL_FORCE = 1e4

NSA_WIDTH = NSA_HEADS * NSA_HEAD_DIM
KV_WIDTH = NSA_KV_GROUPS * NSA_HEAD_DIM
RET_QK_WIDTH = RET_HEADS * RET_KEY_DIM
RET_V_WIDTH = RET_HEADS * RET_VAL_DIM

LANES = 128
VMEM_LIMIT_BYTES = 56 * 1024 * 1024
NEG_BIG = -1e30
F32_TINY = float(np.finfo(np.float32).tiny)

Q_BLOCK = 128
KV_TILE = 256


def _params(*sem):
    return pltpu.CompilerParams(dimension_semantics=sem, vmem_limit_bytes=VMEM_LIMIT_BYTES)


def _silu(x):
    return x * jax.nn.sigmoid(x)


def _rms(x, g):
    return x * lax.rsqrt(jnp.mean(x * x, axis=-1, keepdims=True) + EPS) * g


def _modulate(x, g, shift, scale):
    return _rms(x, g) * (1.0 + scale) + shift


def _dot(a, b):
    return jnp.dot(a, b, preferred_element_type=F32)


def _dot_nt(a, b):
    return lax.dot_general(a, b, (((1,), (1,)), ((), ())), preferred_element_type=F32)


def _ada_kernel(c_ref, w_ref, b_ref, o_ref):
    cond = _silu(c_ref[...]).astype(BF16)
    o_ref[...] = _dot(cond, w_ref[...].astype(BF16)) + b_ref[...]


def _ada(c_pad, w, b):
    rows, d = c_pad.shape
    n = w.shape[1]
    tn = 1024
    return pl.pallas_call(
        _ada_kernel,
        grid=(n // tn,),
        in_specs=[pl.BlockSpec((rows, d), lambda j: (0, 0)),
                  pl.BlockSpec((d, tn), lambda j: (0, j)),
                  pl.BlockSpec((1, tn), lambda j: (0, j))],
        out_specs=pl.BlockSpec((rows, tn), lambda j: (0, j)),
        out_shape=jax.ShapeDtypeStruct((rows, n), F32),
        compiler_params=_params("parallel"),
        name="ada",
    )(c_pad, w, b)


def _ffn_kernel(x_ref, sh_ref, sc_ref, gt_ref, g_ref, wg_ref, wu_ref, wd_ref, o_ref, h_ref, acc_ref):
    j = pl.program_id(1)

    @pl.when(j == 0)
    def _():
        h_ref[...] = _modulate(x_ref[...], g_ref[...], sh_ref[0], sc_ref[0]).astype(BF16)
        acc_ref[...] = jnp.zeros_like(acc_ref)

    h = h_ref[...]
    a = _dot(h, wg_ref[...])
    b = _dot(h, wu_ref[...])
    act = (_silu(a) * b).astype(BF16)
    acc_ref[...] += _dot(act, wd_ref[...])

    @pl.when(j == pl.num_programs(1) - 1)
    def _():
        o_ref[...] = x_ref[...] + (0.5 * gt_ref[0]) * acc_ref[...]


def _ffn(x2, shift, scale, gate, g, wg, wu, wd, seq):
    t, d = x2.shape
    ff = wg.shape[1]
    tm, tf = 512, 512
    per_b = seq // tm
    mod_spec = pl.BlockSpec((1, 1, d), lambda i, j: (i // per_b, 0, 0))
    return pl.pallas_call(
        _ffn_kernel,
        grid=(t // tm, ff // tf),
        in_specs=[pl.BlockSpec((tm, d), lambda i, j: (i, 0)),
                  mod_spec, mod_spec, mod_spec,
                  pl.BlockSpec((1, d), lambda i, j: (0, 0)),
                  pl.BlockSpec((d, tf), lambda i, j: (0, j)),
                  pl.BlockSpec((d, tf), lambda i, j: (0, j)),
                  pl.BlockSpec((tf, d), lambda i, j: (j, 0))],
        out_specs=pl.BlockSpec((tm, d), lambda i, j: (i, 0)),
        out_shape=jax.ShapeDtypeStruct((t, d), F32),
        scratch_shapes=[pltpu.VMEM((tm, d), BF16), pltpu.VMEM((tm, d), F32)],
        compiler_params=_params("parallel", "arbitrary"),
        name="ffn",
    )(x2, shift, scale, gate, g, wg, wu, wd)


def _mod_kernel(x_ref, sh_ref, sc_ref, g_ref, o_ref):
    o_ref[...] = _modulate(x_ref[...], g_ref[...], sh_ref[0], sc_ref[0]).astype(BF16)


def _mod(x2, shift, scale, g, seq):
    t, d = x2.shape
    tm = 512
    per_b = seq // tm
    mod_spec = pl.BlockSpec((1, 1, d), lambda i: (i // per_b, 0, 0))
    return pl.pallas_call(
        _mod_kernel,
        grid=(t // tm,),
        in_specs=[pl.BlockSpec((tm, d), lambda i: (i, 0)), mod_spec, mod_spec,
                  pl.BlockSpec((1, d), lambda i: (0, 0))],
        out_specs=pl.BlockSpec((tm, d), lambda i: (i, 0)),
        out_shape=jax.ShapeDtypeStruct((t, d), BF16),
        compiler_params=_params("parallel"),
        name="modulate",
    )(x2, shift, scale, g)


def _proj_kernel(u_ref, w_ref, gain_ref, o_ref, *, epilogue):
    acc = _dot(u_ref[...], w_ref[...])
    if epilogue == "headnorm":
        g = gain_ref[...]
        parts = [_rms(acc[:, k:k + NSA_HEAD_DIM], g) for k in range(0, acc.shape[1], NSA_HEAD_DIM)]
        acc = jnp.concatenate(parts, axis=1)
    elif epilogue == "keyscale":
        acc = acc * (RET_KEY_DIM ** -0.5)
    elif epilogue == "sigmoid":
        acc = jax.nn.sigmoid(acc)
    o_ref[...] = acc.astype(o_ref.dtype)


def _proj(u, w, out_dtype, epilogue="plain", gain=None, name="proj"):
    t, d = u.shape
    n = w.shape[1]
    tm = 1024
    tn = 512 if n % 512 == 0 else n
    if gain is None:
        gain = jnp.ones((1, NSA_HEAD_DIM), F32)
    return pl.pallas_call(
        functools.partial(_proj_kernel, epilogue=epilogue),
        grid=(t // tm, n // tn),
        in_specs=[pl.BlockSpec((tm, d), lambda i, j: (i, 0)),
                  pl.BlockSpec((d, tn), lambda i, j: (0, j)),
                  pl.BlockSpec((1, NSA_HEAD_DIM), lambda i, j: (0, 0))],
        out_specs=pl.BlockSpec((tm, tn), lambda i, j: (i, j)),
        out_shape=jax.ShapeDtypeStruct((t, n), out_dtype),
        compiler_params=_params("parallel", "parallel"),
        name=name,
    )(u, w, gain)


def _cmp_kernel(z_ref, pos_ref, w1_ref, b1_ref, w2_ref, gk_ref, o_ref):
    half = CMP_BLOCK // 2
    n_rows = z_ref.shape[1] // CMP_STRIDE
    y_lo = jnp.zeros((n_rows, NSA_HEAD_DIM), F32)
    y_hi = jnp.zeros((n_rows, NSA_HEAD_DIM), F32)
    for l in range(half):
        zl = z_ref[0, pl.ds(l, n_rows, stride=CMP_STRIDE), :]
        lo = (zl + pos_ref[0, l:l + 1, :]).astype(BF16)
        hi = (zl + pos_ref[0, half + l:half + l + 1, :]).astype(BF16)
        y_lo += _dot(lo, w1_ref[0, l * NSA_HEAD_DIM:(l + 1) * NSA_HEAD_DIM, :])
        y_hi += _dot(hi, w1_ref[0, (half + l) * NSA_HEAD_DIM:(half + l + 1) * NSA_HEAD_DIM, :])
    y = y_lo + pltpu.roll(y_hi, n_rows - 1, 0)
    hdn = jax.nn.gelu(y + b1_ref[0]).astype(BF16)
    out = _dot(hdn, w2_ref[0])
    normed = _rms(out, gk_ref[...])
    is_key = pl.program_id(0) == 0
    o_ref[0, 0, 0] = jnp.where(is_key, normed, out).astype(o_ref.dtype)


def _compress(raw, pos, w1, b1, w2, gk):
    b, s, _ = raw.shape
    g = NSA_KV_GROUPS
    n_rows = s // CMP_STRIDE
    return pl.pallas_call(
        _cmp_kernel,
        grid=(2, b, g),
        in_specs=[pl.BlockSpec((1, s, NSA_HEAD_DIM), lambda i, bb, gg: (bb, 0, i * g + gg)),
                  pl.BlockSpec((1, CMP_BLOCK, NSA_HEAD_DIM), lambda i, bb, gg: (i, 0, 0)),
                  pl.BlockSpec((1, CMP_BLOCK * NSA_HEAD_DIM, NSA_HEAD_DIM), lambda i, bb, gg: (i, 0, 0)),
                  pl.BlockSpec((1, 1, NSA_HEAD_DIM), lambda i, bb, gg: (i, 0, 0)),
                  pl.BlockSpec((1, NSA_HEAD_DIM, NSA_HEAD_DIM), lambda i, bb, gg: (i, 0, 0)),
                  pl.BlockSpec((1, NSA_HEAD_DIM), lambda i, bb, gg: (0, 0))],
        out_specs=pl.BlockSpec((1, 1, 1, n_rows, NSA_HEAD_DIM), lambda i, bb, gg: (i, bb, gg, 0, 0)),
        out_shape=jax.ShapeDtypeStruct((2, b, g, n_rows, NSA_HEAD_DIM), BF16),
        compiler_params=_params("parallel", "parallel", "parallel"),
        name="compress",
    )(raw, pos, w1, b1, w2, gk)


def _stack_heads(x):
    return jnp.concatenate([x[:, h * NSA_HEAD_DIM:(h + 1) * NSA_HEAD_DIM] for h in range(HEADS_PER_GROUP)], axis=0)


def _nsa_kernel(q_ref, kc_ref, vc_ref, ks_ref, vs_ref, kw_ref, vw_ref, gl_ref, slope_ref, ovl_ref, o_ref):
    qi = pl.program_id(2)
    t0 = qi * Q_BLOCK
    rows = HEADS_PER_GROUP * Q_BLOCK
    scale = NSA_HEAD_DIM ** -0.5
    n_slc = ovl_ref.shape[0]

    qs = _stack_heads(q_ref[...])
    slope = slope_ref[0]
    row = lax.broadcasted_iota(jnp.int32, (rows, 1), 0)
    t_col = t0 + (row & (Q_BLOCK - 1))
    t_f = t_col.astype(F32)

    n_cmp_pad = kc_ref.shape[3]
    cidx = lax.broadcasted_iota(jnp.int32, (1, n_cmp_pad), 1)
    s = _dot_nt(qs, kc_ref[0, 0, 0]) * scale
    centre = (cidx * CMP_STRIDE).astype(F32) + (CMP_BLOCK - 1) / 2
    s = s - slope * (t_f - centre)
    valid = (cidx * CMP_STRIDE + (CMP_BLOCK - 1)) <= t_col
    s = jnp.where(valid, s, -jnp.inf)
    m = jnp.max(s, axis=-1, keepdims=True)
    m = jnp.where(jnp.isfinite(m), m, 0.0)
    p = jnp.where(valid, jnp.exp(s - m), 0.0)
    p = p / jnp.maximum(jnp.sum(p, axis=-1, keepdims=True), F32_TINY)
    pb = p.astype(BF16)
    o_cmp = _dot(pb, vc_ref[0, 0, 0])

    ovl = ovl_ref[...]
    imp = jnp.zeros((n_slc, Q_BLOCK), F32)
    for h in range(HEADS_PER_GROUP):
        imp += _dot_nt(ovl, pb[h * Q_BLOCK:(h + 1) * Q_BLOCK])
    jj = lax.broadcasted_iota(jnp.int32, (n_slc, Q_BLOCK), 0)
    tq = t0 + lax.broadcasted_iota(jnp.int32, (n_slc, Q_BLOCK), 1)
    blk_t = tq // SLC_BLOCK
    forced = (jj == 0) | (jj == blk_t) | (jj == blk_t - 1)
    imp = jnp.where(forced, SEL_FORCE, jnp.where(jj <= blk_t, imp, -SEL_FORCE))
    rank = jnp.zeros((n_slc, Q_BLOCK), jnp.int32)
    for i in range(n_slc):
        ri = imp[i:i + 1, :]
        before = (ri > imp) | ((ri == imp) & (jj > i))
        rank += before.astype(jnp.int32)
    member_t = jnp.where(rank < min(SLC_TOP_N, n_slc), 1.0, 0.0)
    member_t = jnp.concatenate([member_t, jnp.zeros((LANES - n_slc, Q_BLOCK), F32)], axis=0)
    member = member_t.T.astype(BF16)

    def scores(k, pos0):
        sc = _dot_nt(qs, k) * scale
        pos = pos0 + lax.broadcasted_iota(jnp.int32, (1, KV_TILE), 1)
        dist = t_col - pos
        return sc - slope * dist.astype(F32), dist

    def online(carry, sc, mask, v):
        m_old, l_old, acc = carry
        sc = jnp.where(mask, sc, NEG_BIG)
        m_new = jnp.maximum(m_old, jnp.max(sc, axis=-1, keepdims=True))
        alpha = jnp.exp(m_old - m_new)
        pr = jnp.where(mask, jnp.exp(sc - m_new), 0.0)
        l_new = alpha * l_old + jnp.sum(pr, axis=-1, keepdims=True)
        return m_new, l_new, alpha * acc + _dot(pr.astype(BF16), v)

    init = (jnp.full((rows, 1), NEG_BIG, F32), jnp.zeros((rows, 1), F32),
            jnp.zeros((rows, NSA_HEAD_DIM), F32))

    def slc_body(j, carry):
        pos0 = pl.multiple_of(j * KV_TILE, KV_TILE)
        sc, dist = scores(ks_ref[pl.ds(pos0, KV_TILE), :], pos0)
        key_blk = (pos0 + lax.broadcasted_iota(jnp.int32, (LANES, KV_TILE), 1)) // SLC_BLOCK
        expand = jnp.where(key_blk == lax.broadcasted_iota(jnp.int32, (LANES, KV_TILE), 0), 1.0, 0.0)
        sel = _dot(member, expand.astype(BF16))
        sel = jnp.concatenate([sel] * HEADS_PER_GROUP, axis=0) > 0.5
        return online(carry, sc, sel & (dist >= 0), vs_ref[pl.ds(pos0, KV_TILE), :])

    n_tiles = (t0 + Q_BLOCK + KV_TILE - 1) // KV_TILE
    _, l_s, acc_s = lax.fori_loop(0, n_tiles, slc_body, init)
    o_slc = acc_s / jnp.maximum(l_s, F32_TINY)

    def win_body(j, carry):
        pos0 = pl.multiple_of(j * KV_TILE, KV_TILE)
        sc, dist = scores(kw_ref[pl.ds(pos0, KV_TILE), :], pos0)
        return online(carry, sc, (dist >= 0) & (dist < WINDOW), vw_ref[pl.ds(pos0, KV_TILE), :])

    first = jnp.maximum(t0 - (WINDOW - 1), 0) // KV_TILE
    _, l_w, acc_w = lax.fori_loop(first, n_tiles, win_body, init)
    o_win = acc_w / jnp.maximum(l_w, F32_TINY)

    gl = gl_ref[...]

    def gate(br):
        return jnp.concatenate([gl[:, br * HEADS_PER_GROUP + h:br * HEADS_PER_GROUP + h + 1]
                                for h in range(HEADS_PER_GROUP)], axis=0)

    o = gate(0) * o_cmp + gate(1) * o_slc + gate(2) * o_win
    o_ref[...] = jnp.concatenate([o[h * Q_BLOCK:(h + 1) * Q_BLOCK] for h in range(HEADS_PER_GROUP)],
                                 axis=1).astype(o_ref.dtype)


def _nsa(q, kv_cmp, k_slc, v_slc, k_win, v_win, gates, slopes, ovl_t, batch, seq):
    g = NSA_KV_GROUPS
    gw = HEADS_PER_GROUP * NSA_HEAD_DIM
    nq = seq // Q_BLOCK
    n_cmp_pad = kv_cmp.shape[3]
    kv_spec = pl.BlockSpec((seq, NSA_HEAD_DIM), lambda b, gg, qi: (b, gg))
    return pl.pallas_call(
        _nsa_kernel,
        grid=(batch, g, nq),
        in_specs=[pl.BlockSpec((Q_BLOCK, gw), lambda b, gg, qi: (b * nq + qi, gg)),
                  pl.BlockSpec((1, 1, 1, n_cmp_pad, NSA_HEAD_DIM), lambda b, gg, qi: (0, b, gg, 0, 0)),
                  pl.BlockSpec((1, 1, 1, n_cmp_pad, NSA_HEAD_DIM), lambda b, gg, qi: (1, b, gg, 0, 0)),
                  kv_spec, kv_spec, kv_spec, kv_spec,
                  pl.BlockSpec((Q_BLOCK, LANES), lambda b, gg, qi: (b * nq + qi, gg)),
                  pl.BlockSpec((1, HEADS_PER_GROUP * Q_BLOCK, 1), lambda b, gg, qi: (gg, 0, 0)),
                  pl.BlockSpec(ovl_t.shape, lambda b, gg, qi: (0, 0))],
        out_specs=pl.BlockSpec((Q_BLOCK, gw), lambda b, gg, qi: (b * nq + qi, gg)),
        out_shape=jax.ShapeDtypeStruct((batch * seq, NSA_WIDTH), BF16),
        compiler_params=_params("parallel", "parallel", "arbitrary"),
        name="nsa",
    )(q, kv_cmp, kv_cmp, k_slc, v_slc, k_win, v_win, gates, slopes, ovl_t)


def _ret_kernel(cd_ref, q_ref, k_ref, v_ref, g_ref, gain_ref, decay_ref, zeta_ref, xi_ref, o_ref, state_ref):
    @pl.when(pl.program_id(1) == 0)
    def _():
        state_ref[...] = jnp.zeros_like(state_ref)

    dk, dv = RET_KEY_DIM, RET_VAL_DIM
    for h in range(RET_HEADS):
        q = q_ref[:, h * dk:(h + 1) * dk]
        k = k_ref[:, h * dk:(h + 1) * dk]
        v = v_ref[:, h * dv:(h + 1) * dv]
        sc = _dot_nt(q.astype(BF16), k.astype(BF16)) * decay_ref[h]
        inner = _dot(sc.astype(BF16), v)
        st = state_ref[h]
        cross = _dot((q * xi_ref[h]).astype(BF16), st.astype(BF16))
        kz_t = (k * zeta_ref[h]).T.astype(BF16)
        state_ref[h] = st * cd_ref[h] + _dot(kz_t, v)
        y = inner + cross
        yc = y - jnp.mean(y, axis=-1, keepdims=True)
        yn = yc * lax.rsqrt(jnp.mean(yc * yc, axis=-1, keepdims=True) + EPS) * gain_ref[:, h * dv:(h + 1) * dv]
        o_ref[:, h * dv:(h + 1) * dv] = (_silu(g_ref[:, h * dv:(h + 1) * dv]) * yn).astype(o_ref.dtype)


def _retention(q, k, v, g, gain, consts, batch, seq):
    chunk_decay, decay, zeta, xi = consts
    c = RET_CHUNK
    nc = seq // c
    row = lambda b, ci: (b * nc + ci, 0)
    whole3 = lambda b, ci: (0, 0, 0)
    return pl.pallas_call(
        _ret_kernel,
        grid=(batch, nc),
        in_specs=[pl.BlockSpec(memory_space=pltpu.SMEM),
                  pl.BlockSpec((c, RET_QK_WIDTH), row),
                  pl.BlockSpec((c, RET_QK_WIDTH), row),
                  pl.BlockSpec((c, RET_V_WIDTH), row),
                  pl.BlockSpec((c, RET_V_WIDTH), row),
                  pl.BlockSpec((1, RET_V_WIDTH), lambda b, ci: (0, 0)),
                  pl.BlockSpec((RET_HEADS, c, c), whole3),
                  pl.BlockSpec((RET_HEADS, c, 1), whole3),
                  pl.BlockSpec((RET_HEADS, c, 1), whole3)],
        out_specs=pl.BlockSpec((c, RET_V_WIDTH), row),
        out_shape=jax.ShapeDtypeStruct((batch * seq, RET_V_WIDTH), BF16),
        scratch_shapes=[pltpu.VMEM((RET_HEADS, RET_KEY_DIM, RET_VAL_DIM), F32)],
        compiler_params=_params("parallel", "arbitrary"),
        name="retention",
    )(chunk_decay, q, k, v, g, gain, decay, zeta, xi)


def _retention_consts():
    c = RET_CHUNK
    log_gamma = jnp.log1p(-jnp.exp2(-5.0 - jnp.arange(RET_HEADS, dtype=F32)))
    n = jnp.arange(c, dtype=F32)
    diff = n[:, None] - n[None, :]
    decay = jnp.where(diff >= 0, jnp.exp(log_gamma[:, None, None] * jnp.maximum(diff, 0.0)), 0.0)
    zeta = jnp.exp(log_gamma[:, None] * (c - 1 - n)[None, :])[:, :, None]
    xi = jnp.exp(log_gamma[:, None] * (n + 1.0)[None, :])[:, :, None]
    chunk_decay = jnp.exp(log_gamma * c)
    return chunk_decay, decay, zeta, xi


def _merge_kernel(x_ref, gt_ref, on_ref, or_ref, ga_ref, gb_ref, wpn_ref, wpr_ref, wo_ref, o_ref, acc_ref):
    j = pl.program_id(1)

    @pl.when(j == 0)
    def _():
        acc_ref[...] = jnp.zeros_like(acc_ref)

    a = _dot(on_ref[...], wpn_ref[...])
    b = _dot(or_ref[...], wpr_ref[...])
    merged = jax.nn.sigmoid(ga_ref[...]) * a + jax.nn.sigmoid(gb_ref[...]) * b
    acc_ref[...] += _dot(merged.astype(BF16), wo_ref[...])

    @pl.when(j == pl.num_programs(1) - 1)
    def _():
        o_ref[...] = x_ref[...] + gt_ref[0] * acc_ref[...]


def _merge(x2, gate, o_nsa, o_ret, ga, gb, wpn, wpr, wo, seq):
    t, d = x2.shape
    tm, tn = 512, 512
    per_b = seq // tm
    return pl.pallas_call(
        _merge_kernel,
        grid=(t // tm, d // tn),
        in_specs=[pl.BlockSpec((tm, d), lambda i, j: (i, 0)),
                  pl.BlockSpec((1, 1, d), lambda i, j: (i // per_b, 0, 0)),
                  pl.BlockSpec((tm, o_nsa.shape[1]), lambda i, j: (i, 0)),
                  pl.BlockSpec((tm, o_ret.shape[1]), lambda i, j: (i, 0)),
                  pl.BlockSpec((tm, tn), lambda i, j: (i, j)),
                  pl.BlockSpec((tm, tn), lambda i, j: (i, j)),
                  pl.BlockSpec((wpn.shape[0], tn), lambda i, j: (0, j)),
                  pl.BlockSpec((wpr.shape[0], tn), lambda i, j: (0, j)),
                  pl.BlockSpec((tn, d), lambda i, j: (j, 0))],
        out_specs=pl.BlockSpec((tm, d), lambda i, j: (i, 0)),
        out_shape=jax.ShapeDtypeStruct((t, d), F32),
        scratch_shapes=[pltpu.VMEM((tm, d), F32)],
        compiler_params=_params("parallel", "arbitrary"),
        name="merge",
    )(x2, gate, o_nsa, o_ret, ga, gb, wpn, wpr, wo)


def _nsa_tables(seq):
    n_cmp = (seq - CMP_BLOCK) // CMP_STRIDE + 1
    n_cmp_pad = seq // CMP_STRIDE
    n_slc = seq // SLC_BLOCK
    cs = (np.arange(n_cmp) * CMP_STRIDE)[:, None]
    js = (np.arange(n_slc) * SLC_BLOCK)[None, :]
    overlap = np.clip(np.minimum(cs + CMP_BLOCK, js + SLC_BLOCK) - np.maximum(cs, js), 0, None) / CMP_BLOCK
    ovl_t = np.zeros((n_slc, n_cmp_pad), np.float32)
    ovl_t[:, :n_cmp] = overlap.T
    slopes = jnp.exp2(-8.0 * jnp.arange(1, NSA_HEADS + 1, dtype=F32) / NSA_HEADS)
    slopes = jnp.repeat(slopes.reshape(NSA_KV_GROUPS, HEADS_PER_GROUP), Q_BLOCK, axis=1)[:, :, None]
    return jnp.asarray(ovl_t, BF16), slopes


def kernel(x, c, w_ada, b_ada, g_norm, w_ffn_gate, w_ffn_up, w_ffn_down, w_in, g_qk, cmp_pos, cmp_w1, cmp_b1,
           cmp_w2, ret_gn_gain, w_proj_nsa, w_proj_ret, w_out):
    batch, seq, d = x.shape
    depth = w_ada.shape[0]
    t = batch * seq
    x2 = x.reshape(t, d)
    c_pad = jnp.pad(c, ((0, 8 - batch), (0, 0)))
    ovl_t, slopes = _nsa_tables(seq)
    ret_consts = _retention_consts()

    for l in range(depth):
        ada = _ada(c_pad, w_ada[l], b_ada[l][None, :])[:batch].reshape(batch, N_ADA, 1, d)
        sh1, sc1, gt1, sh2, sc2, gt2, sh3, sc3, gt3 = [ada[:, i] for i in range(N_ADA)]

        x2 = _ffn(x2, sh1, sc1, gt1, g_norm[l, 0][None, :], w_ffn_gate[l, 0].astype(BF16),
                  w_ffn_up[l, 0].astype(BF16), w_ffn_down[l, 0].astype(BF16), seq)

        u = _mod(x2, sh2, sc2, g_norm[l, 1][None, :], seq)
        w = w_in[l]
        off = [0]

        def take(width):
            w_part = w[:, off[0]:off[0] + width].astype(BF16)
            off[0] += width
            return w_part

        gq = g_qk[l]
        q_nsa = _proj(u, take(NSA_WIDTH), BF16, "headnorm", gq[0][None, :], name="proj_q")
        cmp_raw = _proj(u, take(2 * KV_WIDTH), F32, name="proj_cmp_raw")
        k_slc = _proj(u, take(KV_WIDTH), BF16, "headnorm", gq[2][None, :], name="proj_k_slc")
        v_slc = _proj(u, take(KV_WIDTH), BF16, name="proj_v_slc")
        k_win = _proj(u, take(KV_WIDTH), BF16, "headnorm", gq[3][None, :], name="proj_k_win")
        v_win = _proj(u, take(KV_WIDTH), BF16, name="proj_v_win")
        w_gl = take(3 * NSA_HEADS).reshape(d, 3, NSA_KV_GROUPS, HEADS_PER_GROUP).transpose(0, 2, 1, 3)
        w_gl = jnp.pad(w_gl.reshape(d, NSA_KV_GROUPS, 3 * HEADS_PER_GROUP),
                       ((0, 0), (0, 0), (0, LANES - 3 * HEADS_PER_GROUP))).reshape(d, NSA_KV_GROUPS * LANES)
        gates = _proj(u, w_gl, F32, "sigmoid", name="proj_gates")
        q_r = _proj(u, take(RET_QK_WIDTH), F32, name="proj_q_ret")
        k_r = _proj(u, take(RET_QK_WIDTH), F32, "keyscale", name="proj_k_ret")
        v_r = _proj(u, take(RET_V_WIDTH), BF16, name="proj_v_ret")
        g_r = _proj(u, take(RET_V_WIDTH), F32, name="proj_g_ret")
        ga = _proj(u, take(d), F32, name="proj_ga")
        gb = _proj(u, take(d), F32, name="proj_gb")

        kv_cmp = _compress(cmp_raw.reshape(batch, seq, 2 * KV_WIDTH), cmp_pos[l], cmp_w1[l].astype(BF16),
                           cmp_b1[l][:, None, :], cmp_w2[l].astype(BF16), gq[1][None, :])
        o_nsa = _nsa(q_nsa, kv_cmp, k_slc, v_slc, k_win, v_win, gates, slopes, ovl_t, batch, seq)
        o_ret = _retention(q_r, k_r, v_r, g_r, ret_gn_gain[l].reshape(1, RET_V_WIDTH), ret_consts, batch, seq)

        x2 = _merge(x2, gt2, o_nsa, o_ret, ga, gb, w_proj_nsa[l].astype(BF16), w_proj_ret[l].astype(BF16),
                    w_out[l].astype(BF16), seq)

        x2 = _ffn(x2, sh3, sc3, gt3, g_norm[l, 2][None, :], w_ffn_gate[l, 1].astype(BF16),
                  w_ffn_up[l, 1].astype(BF16), w_ffn_down[l, 1].astype(BF16), seq)

    return x2.reshape(batch, seq, d)
```

```python
import functools
import math

import numpy as np
import jax
import jax.numpy as jnp
from jax import lax
from jax.experimental import pallas as pl
from jax.experimental.pallas import tpu as pltpu

F32 = jnp.float32
BF16 = jnp.bfloat16

D_MODEL = 2048
NSA_HEADS = 16
NSA_KV_GROUPS = 4
HEADS_PER_GROUP = NSA_HEADS // NSA_KV_GROUPS
NSA_HEAD_DIM = 128
CMP_BLOCK = 32
CMP_STRIDE = 16
SLC_BLOCK = 64
SLC_TOP_N = 16
WINDOW = 512
RET_HEADS = 8
RET_KEY_DIM = 128
RET_VAL_DIM = 256
RET_CHUNK = 128
D_FF = 5632
N_ADA = 9
EPS = 1e-6
SEL_FORCE = 1e4

NSA_WIDTH = NSA_HEADS * NSA_HEAD_DIM
KV_WIDTH = NSA_KV_GROUPS * NSA_HEAD_DIM
RET_QK_WIDTH = RET_HEADS * RET_KEY_DIM
RET_V_WIDTH = RET_HEADS * RET_VAL_DIM

LANES = 128
VMEM_LIMIT_BYTES = 56 * 1024 * 1024
NEG_BIG = -1e30
F32_TINY = float(np.finfo(np.float32).tiny)
LOG2E = math.log2(math.e)

Q_BLOCK = 128
KV_TILE = 256
GATE_ROWS = 16


def _params(*sem):
    return pltpu.CompilerParams(dimension_semantics=sem, vmem_limit_bytes=VMEM_LIMIT_BYTES)


def _silu(x):
    return x * jax.nn.sigmoid(x)


def _rms(x, g):
    return x * lax.rsqrt(jnp.mean(x * x, axis=-1, keepdims=True) + EPS) * g


def _modulate(x, g, shift, scale):
    return _rms(x, g) * (1.0 + scale) + shift


def _dot(a, b):
    return jnp.dot(a, b, preferred_element_type=F32)


def _dot_nt(a, b):
    return lax.dot_general(a, b, (((1,), (1,)), ((), ())), preferred_element_type=F32)


def _ada_kernel(c_ref, w_ref, b_ref, o_ref):
    cond = _silu(c_ref[...]).astype(BF16)
    o_ref[...] = _dot(cond, w_ref[...].astype(BF16)) + b_ref[...]


def _ada(c_pad, w, b):
    rows, d = c_pad.shape
    n = w.shape[1]
    tn = 1024
    return pl.pallas_call(
        _ada_kernel,
        grid=(n // tn,),
        in_specs=[pl.BlockSpec((rows, d), lambda j: (0, 0)),
                  pl.BlockSpec((d, tn), lambda j: (0, j)),
                  pl.BlockSpec((1, tn), lambda j: (0, j))],
        out_specs=pl.BlockSpec((rows, tn), lambda j: (0, j)),
        out_shape=jax.ShapeDtypeStruct((rows, n), F32),
        compiler_params=_params("parallel"),
        name="ada",
    )(c_pad, w, b)


def _ffn_kernel(x_ref, sh_ref, sc_ref, gt_ref, g_ref, wg_ref, wu_ref, wd_ref, o_ref, h_ref, acc_ref):
    j = pl.program_id(1)

    @pl.when(j == 0)
    def _():
        h_ref[...] = _modulate(x_ref[...], g_ref[...], sh_ref[0], sc_ref[0]).astype(BF16)
        acc_ref[...] = jnp.zeros_like(acc_ref)

    h = h_ref[...]
    a = _dot(h, wg_ref[...])
    b = _dot(h, wu_ref[...])
    act = (_silu(a) * b).astype(BF16)
    acc_ref[...] += _dot(act, wd_ref[...])

    @pl.when(j == pl.num_programs(1) - 1)
    def _():
        o_ref[...] = x_ref[...] + (0.5 * gt_ref[0]) * acc_ref[...]


def _ffn(x2, shift, scale, gate, g, wg, wu, wd, seq):
    t, d = x2.shape
    ff = wg.shape[1]
    tm, tf = 512, 512
    per_b = seq // tm
    mod_spec = pl.BlockSpec((1, 1, d), lambda i, j: (i // per_b, 0, 0))
    return pl.pallas_call(
        _ffn_kernel,
        grid=(t // tm, ff // tf),
        in_specs=[pl.BlockSpec((tm, d), lambda i, j: (i, 0)),
                  mod_spec, mod_spec, mod_spec,
                  pl.BlockSpec((1, d), lambda i, j: (0, 0)),
                  pl.BlockSpec((d, tf), lambda i, j: (0, j)),
                  pl.BlockSpec((d, tf), lambda i, j: (0, j)),
                  pl.BlockSpec((tf, d), lambda i, j: (j, 0))],
        out_specs=pl.BlockSpec((tm, d), lambda i, j: (i, 0)),
        out_shape=jax.ShapeDtypeStruct((t, d), F32),
        scratch_shapes=[pltpu.VMEM((tm, d), BF16), pltpu.VMEM((tm, d), F32)],
        compiler_params=_params("parallel", "arbitrary"),
        name="ffn",
    )(x2, shift, scale, gate, g, wg, wu, wd)


def _mod_kernel(x_ref, sh_ref, sc_ref, g_ref, o_ref):
    o_ref[...] = _modulate(x_ref[...], g_ref[...], sh_ref[0], sc_ref[0]).astype(BF16)


def _mod(x2, shift, scale, g, seq):
    t, d = x2.shape
    tm = 512
    per_b = seq // tm
    mod_spec = pl.BlockSpec((1, 1, d), lambda i: (i // per_b, 0, 0))
    return pl.pallas_call(
        _mod_kernel,
        grid=(t // tm,),
        in_specs=[pl.BlockSpec((tm, d), lambda i: (i, 0)), mod_spec, mod_spec,
                  pl.BlockSpec((1, d), lambda i: (0, 0))],
        out_specs=pl.BlockSpec((tm, d), lambda i: (i, 0)),
        out_shape=jax.ShapeDtypeStruct((t, d), BF16),
        compiler_params=_params("parallel"),
        name="modulate",
    )(x2, shift, scale, g)


def _proj_kernel(u_ref, w_ref, gain_ref, o_ref, *, epilogue):
    acc = _dot(u_ref[...], w_ref[...])
    if epilogue == "headnorm":
        g = gain_ref[...]
        parts = [_rms(acc[:, k:k + NSA_HEAD_DIM], g) for k in range(0, acc.shape[1], NSA_HEAD_DIM)]
        acc = jnp.concatenate(parts, axis=1)
    elif epilogue == "keyscale":
        acc = acc * (RET_KEY_DIM ** -0.5)
    elif epilogue == "sigmoid":
        acc = jax.nn.sigmoid(acc)
    o_ref[...] = acc.astype(o_ref.dtype)


def _proj(u, w, out_dtype, epilogue="plain", gain=None, name="proj"):
    t, d = u.shape
    n = w.shape[1]
    tm = 1024
    tn = 512 if n % 512 == 0 else n
    if gain is None:
        gain = jnp.ones((1, NSA_HEAD_DIM), F32)
    return pl.pallas_call(
        functools.partial(_proj_kernel, epilogue=epilogue),
        grid=(t // tm, n // tn),
        in_specs=[pl.BlockSpec((tm, d), lambda i, j: (i, 0)),
                  pl.BlockSpec((d, tn), lambda i, j: (0, j)),
                  pl.BlockSpec((1, NSA_HEAD_DIM), lambda i, j: (0, 0))],
        out_specs=pl.BlockSpec((tm, tn), lambda i, j: (i, j)),
        out_shape=jax.ShapeDtypeStruct((t, n), out_dtype),
        compiler_params=_params("parallel", "parallel"),
        name=name,
    )(u, w, gain)


def _proj_t_kernel(u_ref, wt_ref, gain_ref, o_ref, *, headnorm):
    acc = _dot_nt(wt_ref[...], u_ref[...])
    if headnorm:
        g = gain_ref[...]
        parts = []
        for k in range(0, acc.shape[0], NSA_HEAD_DIM):
            xh = acc[k:k + NSA_HEAD_DIM]
            parts.append(xh * lax.rsqrt(jnp.mean(xh * xh, axis=0, keepdims=True) + EPS) * g)
        acc = jnp.concatenate(parts, axis=0)
    o_ref[...] = acc.astype(o_ref.dtype)


def _proj_t(u, wt, out_dtype, gain=None, name="proj_t"):
    t, d = u.shape
    n = wt.shape[0]
    tm, tn = 1024, 512
    headnorm = gain is not None
    if gain is None:
        gain = jnp.ones((NSA_HEAD_DIM,), F32)
    return pl.pallas_call(
        functools.partial(_proj_t_kernel, headnorm=headnorm),
        grid=(t // tm, n // tn),
        in_specs=[pl.BlockSpec((tm, d), lambda i, j: (i, 0)),
                  pl.BlockSpec((tn, d), lambda i, j: (j, 0)),
                  pl.BlockSpec((NSA_HEAD_DIM, 1), lambda i, j: (0, 0))],
        out_specs=pl.BlockSpec((tn, tm), lambda i, j: (j, i)),
        out_shape=jax.ShapeDtypeStruct((n, t), out_dtype),
        compiler_params=_params("parallel", "parallel"),
        name=name,
    )(u, wt, gain[:, None])


def _cmp_kernel(z_ref, pos_ref, w1_ref, b1_ref, w2_ref, gk_ref, o_ref):
    half = CMP_BLOCK // 2
    n_rows = z_ref.shape[1] // CMP_STRIDE
    y_lo = jnp.zeros((n_rows, NSA_HEAD_DIM), F32)
    y_hi = jnp.zeros((n_rows, NSA_HEAD_DIM), F32)
    for l in range(half):
        zl = z_ref[0, pl.ds(l, n_rows, stride=CMP_STRIDE), :]
        lo = (zl + pos_ref[0, l:l + 1, :]).astype(BF16)
        hi = (zl + pos_ref[0, half + l:half + l + 1, :]).astype(BF16)
        y_lo += _dot(lo, w1_ref[0, l * NSA_HEAD_DIM:(l + 1) * NSA_HEAD_DIM, :])
        y_hi += _dot(hi, w1_ref[0, (half + l) * NSA_HEAD_DIM:(half + l + 1) * NSA_HEAD_DIM, :])
    y = y_lo + pltpu.roll(y_hi, n_rows - 1, 0)
    hdn = jax.nn.gelu(y + b1_ref[0]).astype(BF16)
    out = _dot(hdn, w2_ref[0])
    normed = _rms(out, gk_ref[...])
    is_key = pl.program_id(0) == 0
    o_ref[0, 0, 0] = jnp.where(is_key, normed, out.T).astype(o_ref.dtype)


def _compress(raw, pos, w1, b1, w2, gk):
    b, s, _ = raw.shape
    g = NSA_KV_GROUPS
    n_rows = s // CMP_STRIDE
    assert n_rows == NSA_HEAD_DIM, "key / transposed-value tiles share one square output block"
    return pl.pallas_call(
        _cmp_kernel,
        grid=(2, b, g),
        in_specs=[pl.BlockSpec((1, s, NSA_HEAD_DIM), lambda i, bb, gg: (bb, 0, i * g + gg)),
                  pl.BlockSpec((1, CMP_BLOCK, NSA_HEAD_DIM), lambda i, bb, gg: (i, 0, 0)),
                  pl.BlockSpec((1, CMP_BLOCK * NSA_HEAD_DIM, NSA_HEAD_DIM), lambda i, bb, gg: (i, 0, 0)),
                  pl.BlockSpec((1, 1, NSA_HEAD_DIM), lambda i, bb, gg: (i, 0, 0)),
                  pl.BlockSpec((1, NSA_HEAD_DIM, NSA_HEAD_DIM), lambda i, bb, gg: (i, 0, 0)),
                  pl.BlockSpec((1, NSA_HEAD_DIM), lambda i, bb, gg: (0, 0))],
        out_specs=pl.BlockSpec((1, 1, 1, n_rows, NSA_HEAD_DIM), lambda i, bb, gg: (i, bb, gg, 0, 0)),
        out_shape=jax.ShapeDtypeStruct((2, b, g, n_rows, NSA_HEAD_DIM), BF16),
        compiler_params=_params("parallel", "parallel", "parallel"),
        name="compress",
    )(raw, pos, w1, b1, w2, gk)


def _head(x, h):
    return x[:, h * Q_BLOCK:(h + 1) * Q_BLOCK]


def _nsa_kernel(slope_ref, qt_ref, kc_ref, vct_ref, ks_ref, vst_ref, kw_ref, vwt_ref, gt_ref, ovl_ref, exp_ref,
                o_ref):
    hg, dh = HEADS_PER_GROUP, NSA_HEAD_DIM
    grp = pl.program_id(1)
    t0 = pl.program_id(2) * Q_BLOCK
    c1 = (dh ** -0.5) * LOG2E
    n_slc = ovl_ref.shape[0]
    slope2 = [slope_ref[grp * hg + h] for h in range(hg)]

    qt = jnp.concatenate([qt_ref[h * dh:(h + 1) * dh, :] for h in range(hg)], axis=1)
    t_row = t0 + lax.broadcasted_iota(jnp.int32, (1, Q_BLOCK), 1)

    n_cmp_pad = kc_ref.shape[3]
    cstart = lax.broadcasted_iota(jnp.int32, (n_cmp_pad, Q_BLOCK), 0) * CMP_STRIDE
    centre_dist = t_row.astype(F32) - (cstart.astype(F32) + (CMP_BLOCK - 1) / 2)
    valid = (cstart + (CMP_BLOCK - 1)) <= t_row
    sc = _dot(kc_ref[0, 0, 0], qt)
    p_heads = []
    for h in range(hg):
        s = _head(sc, h) * c1 - slope2[h] * centre_dist
        s = jnp.where(valid, s, -jnp.inf)
        m = jnp.max(s, axis=0, keepdims=True)
        m = jnp.where(jnp.isfinite(m), m, 0.0)
        e = jnp.where(valid, jnp.exp2(s - m), 0.0)
        p = e / jnp.maximum(jnp.sum(e, axis=0, keepdims=True), F32_TINY)
        p_heads.append(p.astype(BF16))
    o_cmp = _dot(vct_ref[0, 0, 0], jnp.concatenate(p_heads, axis=1))

    ovl = ovl_ref[...]
    imp = _dot(ovl, p_heads[0])
    for h in range(1, hg):
        imp += _dot(ovl, p_heads[h])
    jj = lax.broadcasted_iota(jnp.int32, (n_slc, Q_BLOCK), 0)
    blk_t = t_row // SLC_BLOCK
    forced = (jj == 0) | (jj == blk_t) | (jj == blk_t - 1)
    imp = jnp.where(forced, SEL_FORCE, jnp.where(jj <= blk_t, imp, -SEL_FORCE))
    rank = jnp.zeros((n_slc, Q_BLOCK), jnp.int32)
    for i in range(n_slc):
        ri = imp[i:i + 1, :]
        before = (ri > imp) | ((ri == imp) & (jj > i))
        rank += before.astype(jnp.int32)
    member = jnp.where(rank < min(SLC_TOP_N, n_slc), 1.0, 0.0).astype(BF16)

    key_local = lax.broadcasted_iota(jnp.int32, (KV_TILE, Q_BLOCK), 0)

    def online(carry, k_tile, vt_tile, pos0, mask_fn):
        m_old, l_old, acc = carry
        sc = _dot(k_tile, qt)
        dist = t_row - (pos0 + key_local)
        mask = mask_fn(dist)
        dist_f = dist.astype(F32)
        m_new, l_new, alpha, p_t = [], [], [], []
        for h in range(hg):
            s = jnp.where(mask, _head(sc, h) * c1 - slope2[h] * dist_f, NEG_BIG)
            m_o = _head(m_old, h)
            m_n = jnp.maximum(m_o, jnp.max(s, axis=0, keepdims=True))
            a = jnp.exp2(m_o - m_n)
            p = jnp.exp2(s - m_n)
            m_new.append(m_n)
            alpha.append(a)
            l_new.append(a * _head(l_old, h) + jnp.sum(p, axis=0, keepdims=True))
            p_t.append(p.astype(BF16))
        acc = jnp.concatenate(alpha, axis=1) * acc + _dot(vt_tile, jnp.concatenate(p_t, axis=1))
        return jnp.concatenate(m_new, axis=1), jnp.concatenate(l_new, axis=1), acc

    init = (jnp.full((1, hg * Q_BLOCK), NEG_BIG, F32), jnp.zeros((1, hg * Q_BLOCK), F32),
            jnp.zeros((dh, hg * Q_BLOCK), F32))

    def slc_body(j, carry):
        pos0 = pl.multiple_of(j * KV_TILE, KV_TILE)
        sel = _dot(exp_ref[pl.ds(pos0, KV_TILE), :], member)
        return online(carry, ks_ref[pl.ds(pos0, KV_TILE), :], vst_ref[:, pl.ds(pos0, KV_TILE)], pos0,
                      lambda dist: (sel > 0.5) & (dist >= 0))

    n_tiles = (t0 + Q_BLOCK + KV_TILE - 1) // KV_TILE
    _, l_s, acc_s = lax.fori_loop(0, n_tiles, slc_body, init)

    def win_body(j, carry):
        pos0 = pl.multiple_of(j * KV_TILE, KV_TILE)
        return online(carry, kw_ref[pl.ds(pos0, KV_TILE), :], vwt_ref[:, pl.ds(pos0, KV_TILE)], pos0,
                      lambda dist: (dist >= 0) & (dist < WINDOW))

    first = jnp.maximum(t0 - (WINDOW - 1), 0) // KV_TILE
    _, l_w, acc_w = lax.fori_loop(first, n_tiles, win_body, init)

    gt = gt_ref[0]

    def gate(br):
        return jnp.concatenate([gt[br * hg + h:br * hg + h + 1, :] for h in range(hg)], axis=1)

    o = (gate(0) * o_cmp + (gate(1) / jnp.maximum(l_s, F32_TINY)) * acc_s
         + (gate(2) / jnp.maximum(l_w, F32_TINY)) * acc_w)
    o_ref[...] = jnp.concatenate([_head(o, h).T for h in range(hg)], axis=1).astype(o_ref.dtype)


def _nsa(qt, kv_cmp, k_slc, vt_slc, k_win, vt_win, gates_t, slopes2, ovl, expand, batch, seq):
    g = NSA_KV_GROUPS
    gw = HEADS_PER_GROUP * NSA_HEAD_DIM
    nq = seq // Q_BLOCK
    n_cmp_pad = kv_cmp.shape[3]
    k_spec = pl.BlockSpec((seq, NSA_HEAD_DIM), lambda b, gg, qi: (b, gg))
    vt_spec = pl.BlockSpec((NSA_HEAD_DIM, seq), lambda b, gg, qi: (gg, b))
    return pl.pallas_call(
        _nsa_kernel,
        grid=(batch, g, nq),
        in_specs=[pl.BlockSpec(memory_space=pltpu.SMEM),
                  pl.BlockSpec((gw, Q_BLOCK), lambda b, gg, qi: (gg, b * nq + qi)),
                  pl.BlockSpec((1, 1, 1, n_cmp_pad, NSA_HEAD_DIM), lambda b, gg, qi: (0, b, gg, 0, 0)),
                  pl.BlockSpec((1, 1, 1, NSA_HEAD_DIM, n_cmp_pad), lambda b, gg, qi: (1, b, gg, 0, 0)),
                  k_spec, vt_spec, k_spec, vt_spec,
                  pl.BlockSpec((1, GATE_ROWS, Q_BLOCK), lambda b, gg, qi: (gg, 0, b * nq + qi)),
                  pl.BlockSpec(ovl.shape, lambda b, gg, qi: (0, 0)),
                  pl.BlockSpec(expand.shape, lambda b, gg, qi: (0, 0))],
        out_specs=pl.BlockSpec((Q_BLOCK, gw), lambda b, gg, qi: (b * nq + qi, gg)),
        out_shape=jax.ShapeDtypeStruct((batch * seq, NSA_WIDTH), BF16),
        compiler_params=_params("parallel", "parallel", "arbitrary"),
        name="nsa",
    )(slopes2, qt, kv_cmp, kv_cmp, k_slc, vt_slc, k_win, vt_win, gates_t, ovl, expand)


def _ret_kernel(cd_ref, q_ref, k_ref, v_ref, g_ref, gain_ref, decay_ref, zeta_ref, xi_ref, o_ref, state_ref):
    @pl.when(pl.program_id(1) == 0)
    def _():
        state_ref[...] = jnp.zeros_like(state_ref)

    dk, dv = RET_KEY_DIM, RET_VAL_DIM
    for h in range(RET_HEADS):
        q = q_ref[:, h * dk:(h + 1) * dk]
        k = k_ref[:, h * dk:(h + 1) * dk]
        v = v_ref[:, h * dv:(h + 1) * dv]
        sc = _dot_nt(q.astype(BF16), k.astype(BF16)) * decay_ref[h]
        inner = _dot(sc.astype(BF16), v)
        st = state_ref[h]
        cross = _dot((q * xi_ref[h]).astype(BF16), st.astype(BF16))
        kz_t = (k * zeta_ref[h]).T.astype(BF16)
        state_ref[h] = st * cd_ref[h] + _dot(kz_t, v)
        y = inner + cross
        yc = y - jnp.mean(y, axis=-1, keepdims=True)
        yn = yc * lax.rsqrt(jnp.mean(yc * yc, axis=-1, keepdims=True) + EPS) * gain_ref[:, h * dv:(h + 1) * dv]
        o_ref[:, h * dv:(h + 1) * dv] = (_silu(g_ref[:, h * dv:(h + 1) * dv]) * yn).astype(o_ref.dtype)


def _retention(q, k, v, g, gain, consts, batch, seq):
    chunk_decay, decay, zeta, xi = consts
    c = RET_CHUNK
    nc = seq // c
    row = lambda b, ci: (b * nc + ci, 0)
    whole3 = lambda b, ci: (0, 0, 0)
    return pl.pallas_call(
        _ret_kernel,
        grid=(batch, nc),
        in_specs=[pl.BlockSpec(memory_space=pltpu.SMEM),
                  pl.BlockSpec((c, RET_QK_WIDTH), row),
                  pl.BlockSpec((c, RET_QK_WIDTH), row),
                  pl.BlockSpec((c, RET_V_WIDTH), row),
                  pl.BlockSpec((c, RET_V_WIDTH), row),
                  pl.BlockSpec((1, RET_V_WIDTH), lambda b, ci: (0, 0)),
                  pl.BlockSpec((RET_HEADS, c, c), whole3),
                  pl.BlockSpec((RET_HEADS, c, 1), whole3),
                  pl.BlockSpec((RET_HEADS, c, 1), whole3)],
        out_specs=pl.BlockSpec((c, RET_V_WIDTH), row),
        out_shape=jax.ShapeDtypeStruct((batch * seq, RET_V_WIDTH), BF16),
        scratch_shapes=[pltpu.VMEM((RET_HEADS, RET_KEY_DIM, RET_VAL_DIM), F32)],
        compiler_params=_params("parallel", "arbitrary"),
        name="retention",
    )(chunk_decay, q, k, v, g, gain, decay, zeta, xi)


def _retention_consts():
    c = RET_CHUNK
    log_gamma = jnp.log1p(-jnp.exp2(-5.0 - jnp.arange(RET_HEADS, dtype=F32)))
    n = jnp.arange(c, dtype=F32)
    diff = n[:, None] - n[None, :]
    decay = jnp.where(diff >= 0, jnp.exp(log_gamma[:, None, None] * jnp.maximum(diff, 0.0)), 0.0)
    zeta = jnp.exp(log_gamma[:, None] * (c - 1 - n)[None, :])[:, :, None]
    xi = jnp.exp(log_gamma[:, None] * (n + 1.0)[None, :])[:, :, None]
    chunk_decay = jnp.exp(log_gamma * c)
    return chunk_decay, decay, zeta, xi


def _merge_kernel(x_ref, gt_ref, on_ref, or_ref, ga_ref, gb_ref, wpn_ref, wpr_ref, wo_ref, o_ref, acc_ref):
    j = pl.program_id(1)

    @pl.when(j == 0)
    def _():
        acc_ref[...] = jnp.zeros_like(acc_ref)

    a = _dot(on_ref[...], wpn_ref[...])
    b = _dot(or_ref[...], wpr_ref[...])
    merged = jax.nn.sigmoid(ga_ref[...]) * a + jax.nn.sigmoid(gb_ref[...]) * b
    acc_ref[...] += _dot(merged.astype(BF16), wo_ref[...])

    @pl.when(j == pl.num_programs(1) - 1)
    def _():
        o_ref[...] = x_ref[...] + gt_ref[0] * acc_ref[...]


def _merge(x2, gate, o_nsa, o_ret, ga, gb, wpn, wpr, wo, seq):
    t, d = x2.shape
    tm, tn = 512, 512
    per_b = seq // tm
    return pl.pallas_call(
        _merge_kernel,
        grid=(t // tm, d // tn),
        in_specs=[pl.BlockSpec((tm, d), lambda i, j: (i, 0)),
                  pl.BlockSpec((1, 1, d), lambda i, j: (i // per_b, 0, 0)),
                  pl.BlockSpec((tm, o_nsa.shape[1]), lambda i, j: (i, 0)),
                  pl.BlockSpec((tm, o_ret.shape[1]), lambda i, j: (i, 0)),
                  pl.BlockSpec((tm, tn), lambda i, j: (i, j)),
                  pl.BlockSpec((tm, tn), lambda i, j: (i, j)),
                  pl.BlockSpec((wpn.shape[0], tn), lambda i, j: (0, j)),
                  pl.BlockSpec((wpr.shape[0], tn), lambda i, j: (0, j)),
                  pl.BlockSpec((tn, d), lambda i, j: (j, 0))],
        out_specs=pl.BlockSpec((tm, d), lambda i, j: (i, 0)),
        out_shape=jax.ShapeDtypeStruct((t, d), F32),
        scratch_shapes=[pltpu.VMEM((tm, d), F32)],
        compiler_params=_params("parallel", "arbitrary"),
        name="merge",
    )(x2, gate, o_nsa, o_ret, ga, gb, wpn, wpr, wo)


def _nsa_tables(seq):
    n_cmp = (seq - CMP_BLOCK) // CMP_STRIDE + 1
    n_cmp_pad = seq // CMP_STRIDE
    n_slc = seq // SLC_BLOCK
    cs = (np.arange(n_cmp) * CMP_STRIDE)[:, None]
    js = (np.arange(n_slc) * SLC_BLOCK)[None, :]
    overlap = np.clip(np.minimum(cs + CMP_BLOCK, js + SLC_BLOCK) - np.maximum(cs, js), 0, None) / CMP_BLOCK
    ovl = np.zeros((n_slc, n_cmp_pad), np.float32)
    ovl[:, :n_cmp] = overlap.T
    expand = (np.arange(seq)[:, None] // SLC_BLOCK == np.arange(n_slc)[None, :]).astype(np.float32)
    slopes2 = jnp.exp2(-8.0 * jnp.arange(1, NSA_HEADS + 1, dtype=F32) / NSA_HEADS) * LOG2E
    return jnp.asarray(ovl, BF16), jnp.asarray(expand, BF16), slopes2


def kernel(x, c, w_ada, b_ada, g_norm, w_ffn_gate, w_ffn_up, w_ffn_down, w_in, g_qk, cmp_pos, cmp_w1, cmp_b1,
           cmp_w2, ret_gn_gain, w_proj_nsa, w_proj_ret, w_out):
    batch, seq, d = x.shape
    depth = w_ada.shape[0]
    t = batch * seq
    x2 = x.reshape(t, d)
    c_pad = jnp.pad(c, ((0, 8 - batch), (0, 0)))
    ovl, expand, slopes2 = _nsa_tables(seq)
    ret_consts = _retention_consts()
    hg = HEADS_PER_GROUP

    for l in range(depth):
        ada = _ada(c_pad, w_ada[l], b_ada[l][None, :])[:batch].reshape(batch, N_ADA, 1, d)
        sh1, sc1, gt1, sh2, sc2, gt2, sh3, sc3, gt3 = [ada[:, i] for i in range(N_ADA)]

        x2 = _ffn(x2, sh1, sc1, gt1, g_norm[l, 0][None, :], w_ffn_gate[l, 0].astype(BF16),
                  w_ffn_up[l, 0].astype(BF16), w_ffn_down[l, 0].astype(BF16), seq)

        u = _mod(x2, sh2, sc2, g_norm[l, 1][None, :], seq)
        w = w_in[l]
        off = [0]

        def take(width, transposed=False):
            w_part = w[:, off[0]:off[0] + width]
            off[0] += width
            return (w_part.T if transposed else w_part).astype(BF16)

        gq = g_qk[l]
        qt_nsa = _proj_t(u, take(NSA_WIDTH, True), BF16, gq[0], name="proj_q")
        cmp_raw = _proj(u, take(2 * KV_WIDTH), F32, name="proj_cmp_raw")
        k_slc = _proj(u, take(KV_WIDTH), BF16, "headnorm", gq[2][None, :], name="proj_k_slc")
        vt_slc = _proj_t(u, take(KV_WIDTH, True), BF16, name="proj_v_slc")
        k_win = _proj(u, take(KV_WIDTH), BF16, "headnorm", gq[3][None, :], name="proj_k_win")
        vt_win = _proj_t(u, take(KV_WIDTH, True), BF16, name="proj_v_win")
        w_gl = take(3 * NSA_HEADS).reshape(d, 3, NSA_KV_GROUPS, hg).transpose(0, 2, 1, 3)
        w_gl = jnp.pad(w_gl.reshape(d, NSA_KV_GROUPS, 3 * hg),
                       ((0, 0), (0, 0), (0, LANES - 3 * hg))).reshape(d, NSA_KV_GROUPS * LANES)
        gates = _proj(u, w_gl, F32, "sigmoid", name="proj_gates")
        gates_t = gates.reshape(t, NSA_KV_GROUPS, LANES)[:, :, :GATE_ROWS].transpose(1, 2, 0)
        q_r = _proj(u, take(RET_QK_WIDTH), F32, name="proj_q_ret")
        k_r = _proj(u, take(RET_QK_WIDTH), F32, "keyscale", name="proj_k_ret")
        v_r = _proj(u, take(RET_V_WIDTH), BF16, name="proj_v_ret")
        g_r = _proj(u, take(RET_V_WIDTH), F32, name="proj_g_ret")
        ga = _proj(u, take(d), F32, name="proj_ga")
        gb = _proj(u, take(d), F32, name="proj_gb")

        kv_cmp = _compress(cmp_raw.reshape(batch, seq, 2 * KV_WIDTH), cmp_pos[l], cmp_w1[l].astype(BF16),
                           cmp_b1[l][:, None, :], cmp_w2[l].astype(BF16), gq[1][None, :])
        o_nsa = _nsa(qt_nsa, kv_cmp, k_slc, vt_slc, k_win, vt_win, gates_t, slopes2, ovl, expand, batch, seq)
        o_ret = _retention(q_r, k_r, v_r, g_r, ret_gn_gain[l].reshape(1, RET_V_WIDTH), ret_consts, batch, seq)

        x2 = _merge(x2, gt2, o_nsa, o_ret, ga, gb, w_proj_nsa[l].astype(BF16), w_proj_ret[l].astype(BF16),
                    w_out[l].astype(BF16), seq)

        x2 = _ffn(x2, sh3, sc3, gt3, g_norm[l, 2][None, :], w_ffn_gate[l, 1].astype(BF16),
                  w_ffn_up[l, 1].astype(BF16), w_ffn_down[l, 1].astype(BF16), seq)

    return x2.reshape(batch, seq, d)
```

```python
import functools
import math

import numpy as np
import jax
import jax.numpy as jnp
from jax import lax
from jax.experimental import pallas as pl
from jax.experimental.pallas import tpu as pltpu

F32 = jnp.float32
BF16 = jnp.bfloat16

D_MODEL = 2048
NSA_HEADS = 16
NSA_KV_GROUPS = 4
HEADS_PER_GROUP = NSA_HEADS // NSA_KV_GROUPS
NSA_HEAD_DIM = 128
CMP_BLOCK = 32
CMP_STRIDE = 16
SLC_BLOCK = 64
SLC_TOP_N = 16
WINDOW = 512
RET_HEADS = 8
RET_KEY_DIM = 128
RET_VAL_DIM = 256
RET_CHUNK = 128
D_FF = 5632
N_ADA = 9
EPS = 1e-6
SEL_FORCE = 1e4

NSA_WIDTH = NSA_HEADS * NSA_HEAD_DIM
KV_WIDTH = NSA_KV_GROUPS * NSA_HEAD_DIM
RET_QK_WIDTH = RET_HEADS * RET_KEY_DIM
RET_V_WIDTH = RET_HEADS * RET_VAL_DIM

LANES = 128
VMEM_LIMIT_BYTES = 56 * 1024 * 1024
NEG_BIG = -1e30
F32_TINY = float(np.finfo(np.float32).tiny)
LOG2E = math.log2(math.e)

Q_BLOCK = 128
Q_CHAINS = 2
Q_STEP = Q_CHAINS * Q_BLOCK
KV_TILE = 256
GATE_ROWS = 16


def _params(*sem):
    return pltpu.CompilerParams(dimension_semantics=sem, vmem_limit_bytes=VMEM_LIMIT_BYTES)


def _silu(x):
    return x * jax.nn.sigmoid(x)


def _rms(x, g):
    return x * lax.rsqrt(jnp.mean(x * x, axis=-1, keepdims=True) + EPS) * g


def _modulate(x, g, shift, scale):
    return _rms(x, g) * (1.0 + scale) + shift


def _dot(a, b):
    return jnp.dot(a, b, preferred_element_type=F32)


def _dot_nt(a, b):
    return lax.dot_general(a, b, (((1,), (1,)), ((), ())), preferred_element_type=F32)


def _ada_kernel(c_ref, w_ref, b_ref, o_ref):
    cond = _silu(c_ref[...]).astype(BF16)
    o_ref[...] = _dot(cond, w_ref[...].astype(BF16)) + b_ref[...]


def _ada(c_pad, w, b):
    rows, d = c_pad.shape
    n = w.shape[1]
    tn = 1024
    return pl.pallas_call(
        _ada_kernel,
        grid=(n // tn,),
        in_specs=[pl.BlockSpec((rows, d), lambda j: (0, 0)),
                  pl.BlockSpec((d, tn), lambda j: (0, j)),
                  pl.BlockSpec((1, tn), lambda j: (0, j))],
        out_specs=pl.BlockSpec((rows, tn), lambda j: (0, j)),
        out_shape=jax.ShapeDtypeStruct((rows, n), F32),
        compiler_params=_params("parallel"),
        name="ada",
    )(c_pad, w, b)


def _ffn_kernel(x_ref, sh_ref, sc_ref, gt_ref, g_ref, wg_ref, wu_ref, wd_ref, o_ref, h_ref):
    j = pl.program_id(1)

    @pl.when(j == 0)
    def _():
        h_ref[...] = _modulate(x_ref[...], g_ref[...], sh_ref[0], sc_ref[0]).astype(BF16)
        o_ref[...] = jnp.zeros_like(o_ref)

    h = h_ref[...]
    a = _dot(h, wg_ref[...].astype(BF16))
    b = _dot(h, wu_ref[...].astype(BF16))
    act = (_silu(a) * b).astype(BF16)
    o_ref[...] += _dot(act, wd_ref[...].astype(BF16))

    @pl.when(j == pl.num_programs(1) - 1)
    def _():
        o_ref[...] = x_ref[...] + (0.5 * gt_ref[0]) * o_ref[...]


def _ffn(x2, shift, scale, gate, g, wg, wu, wd, seq):
    t, d = x2.shape
    ff = wg.shape[1]
    tm, tf = 1024, 256
    per_b = seq // tm
    mod_spec = pl.BlockSpec((1, 1, d), lambda i, j: (i // per_b, 0, 0))
    return pl.pallas_call(
        _ffn_kernel,
        grid=(t // tm, ff // tf),
        in_specs=[pl.BlockSpec((tm, d), lambda i, j: (i, 0), pipeline_mode=pl.Buffered(1)),
                  mod_spec, mod_spec, mod_spec,
                  pl.BlockSpec((1, d), lambda i, j: (0, 0)),
                  pl.BlockSpec((d, tf), lambda i, j: (0, j)),
                  pl.BlockSpec((d, tf), lambda i, j: (0, j)),
                  pl.BlockSpec((tf, d), lambda i, j: (j, 0))],
        out_specs=pl.BlockSpec((tm, d), lambda i, j: (i, 0)),
        out_shape=jax.ShapeDtypeStruct((t, d), F32),
        scratch_shapes=[pltpu.VMEM((tm, d), BF16)],
        compiler_params=_params("parallel", "arbitrary"),
        name="ffn",
    )(x2, shift, scale, gate, g, wg, wu, wd)


def _mod_kernel(x_ref, sh_ref, sc_ref, g_ref, o_ref):
    o_ref[...] = _modulate(x_ref[...], g_ref[...], sh_ref[0], sc_ref[0]).astype(BF16)


def _mod(x2, shift, scale, g, seq):
    t, d = x2.shape
    tm = 512
    per_b = seq // tm
    mod_spec = pl.BlockSpec((1, 1, d), lambda i: (i // per_b, 0, 0))
    return pl.pallas_call(
        _mod_kernel,
        grid=(t // tm,),
        in_specs=[pl.BlockSpec((tm, d), lambda i: (i, 0)), mod_spec, mod_spec,
                  pl.BlockSpec((1, d), lambda i: (0, 0))],
        out_specs=pl.BlockSpec((tm, d), lambda i: (i, 0)),
        out_shape=jax.ShapeDtypeStruct((t, d), BF16),
        compiler_params=_params("parallel"),
        name="modulate",
    )(x2, shift, scale, g)


def _proj_kernel(u_ref, w_ref, gain_ref, o_ref, *, epilogue):
    acc = _dot(u_ref[...], w_ref[...])
    if epilogue == "headnorm":
        g = gain_ref[...]
        parts = [_rms(acc[:, k:k + NSA_HEAD_DIM], g) for k in range(0, acc.shape[1], NSA_HEAD_DIM)]
        acc = jnp.concatenate(parts, axis=1)
    elif epilogue == "keyscale":
        acc = acc * (RET_KEY_DIM ** -0.5)
    elif epilogue == "sigmoid":
        acc = jax.nn.sigmoid(acc)
    o_ref[...] = acc.astype(o_ref.dtype)


def _proj(u, w, out_dtype, epilogue="plain", gain=None, name="proj"):
    t, d = u.shape
    n = w.shape[1]
    tm = 1024
    tn = 1024 if n % 1024 == 0 else 512
    if gain is None:
        gain = jnp.ones((1, NSA_HEAD_DIM), F32)
    return pl.pallas_call(
        functools.partial(_proj_kernel, epilogue=epilogue),
        grid=(t // tm, n // tn),
        in_specs=[pl.BlockSpec((tm, d), lambda i, j: (i, 0)),
                  pl.BlockSpec((d, tn), lambda i, j: (0, j)),
                  pl.BlockSpec((1, NSA_HEAD_DIM), lambda i, j: (0, 0))],
        out_specs=pl.BlockSpec((tm, tn), lambda i, j: (i, j)),
        out_shape=jax.ShapeDtypeStruct((t, n), out_dtype),
        compiler_params=_params("parallel", "parallel"),
        name=name,
    )(u, w, gain)


def _proj_t_kernel(u_ref, wt_ref, gain_ref, o_ref, *, headnorm):
    acc = _dot_nt(wt_ref[...], u_ref[...])
    if headnorm:
        g = gain_ref[...]
        parts = []
        for k in range(0, acc.shape[0], NSA_HEAD_DIM):
            xh = acc[k:k + NSA_HEAD_DIM]
            parts.append(xh * lax.rsqrt(jnp.mean(xh * xh, axis=0, keepdims=True) + EPS) * g)
        acc = jnp.concatenate(parts, axis=0)
    o_ref[...] = acc.astype(o_ref.dtype)


def _proj_t(u, wt, out_dtype, gain=None, name="proj_t"):
    t, d = u.shape
    n = wt.shape[0]
    tm = 1024
    tn = 1024 if n % 1024 == 0 else 512
    headnorm = gain is not None
    if gain is None:
        gain = jnp.ones((NSA_HEAD_DIM,), F32)
    return pl.pallas_call(
        functools.partial(_proj_t_kernel, headnorm=headnorm),
        grid=(t // tm, n // tn),
        in_specs=[pl.BlockSpec((tm, d), lambda i, j: (i, 0)),
                  pl.BlockSpec((tn, d), lambda i, j: (j, 0)),
                  pl.BlockSpec((NSA_HEAD_DIM, 1), lambda i, j: (0, 0))],
        out_specs=pl.BlockSpec((tn, tm), lambda i, j: (j, i)),
        out_shape=jax.ShapeDtypeStruct((n, t), out_dtype),
        compiler_params=_params("parallel", "parallel"),
        name=name,
    )(u, wt, gain[:, None])


def _cmp_kernel(z_ref, pos_ref, w1_ref, b1_ref, w2_ref, gk_ref, o_ref):
    half = CMP_BLOCK // 2
    n_rows = z_ref.shape[1] // CMP_STRIDE
    y_lo = jnp.zeros((n_rows, NSA_HEAD_DIM), F32)
    y_hi = jnp.zeros((n_rows, NSA_HEAD_DIM), F32)
    for l in range(half):
        zl = z_ref[0, pl.ds(l, n_rows, stride=CMP_STRIDE), :]
        lo = (zl + pos_ref[0, l:l + 1, :]).astype(BF16)
        hi = (zl + pos_ref[0, half + l:half + l + 1, :]).astype(BF16)
        y_lo += _dot(lo, w1_ref[0, l * NSA_HEAD_DIM:(l + 1) * NSA_HEAD_DIM, :])
        y_hi += _dot(hi, w1_ref[0, (half + l) * NSA_HEAD_DIM:(half + l + 1) * NSA_HEAD_DIM, :])
    y = y_lo + pltpu.roll(y_hi, n_rows - 1, 0)
    hdn = jax.nn.gelu(y + b1_ref[0]).astype(BF16)
    out = _dot(hdn, w2_ref[0])
    normed = _rms(out, gk_ref[...])
    is_key = pl.program_id(0) == 0
    o_ref[0, 0, 0] = jnp.where(is_key, normed, out.T).astype(o_ref.dtype)


def _compress(raw, pos, w1, b1, w2, gk):
    b, s, _ = raw.shape
    g = NSA_KV_GROUPS
    n_rows = s // CMP_STRIDE
    assert n_rows == NSA_HEAD_DIM, "key / transposed-value tiles share one square output block"
    return pl.pallas_call(
        _cmp_kernel,
        grid=(2, b, g),
        in_specs=[pl.BlockSpec((1, s, NSA_HEAD_DIM), lambda i, bb, gg: (bb, 0, i * g + gg)),
                  pl.BlockSpec((1, CMP_BLOCK, NSA_HEAD_DIM), lambda i, bb, gg: (i, 0, 0)),
                  pl.BlockSpec((1, CMP_BLOCK * NSA_HEAD_DIM, NSA_HEAD_DIM), lambda i, bb, gg: (i, 0, 0)),
                  pl.BlockSpec((1, 1, NSA_HEAD_DIM), lambda i, bb, gg: (i, 0, 0)),
                  pl.BlockSpec((1, NSA_HEAD_DIM, NSA_HEAD_DIM), lambda i, bb, gg: (i, 0, 0)),
                  pl.BlockSpec((1, NSA_HEAD_DIM), lambda i, bb, gg: (0, 0))],
        out_specs=pl.BlockSpec((1, 1, 1, n_rows, NSA_HEAD_DIM), lambda i, bb, gg: (i, bb, gg, 0, 0)),
        out_shape=jax.ShapeDtypeStruct((2, b, g, n_rows, NSA_HEAD_DIM), BF16),
        compiler_params=_params("parallel", "parallel", "parallel"),
        name="compress",
    )(raw, pos, w1, b1, w2, gk)


def _head(x, h):
    return x[:, h * Q_BLOCK:(h + 1) * Q_BLOCK]


def _round_up(x, m):
    return (x + m - 1) // m * m


def _pad_rows(x, n):
    if x.shape[0] == n:
        return x
    return jnp.concatenate([x, jnp.zeros((n - x.shape[0], x.shape[1]), x.dtype)], axis=0)


def _nsa_kernel(slope_ref, qt_ref, kc_ref, vct_ref, ks_ref, vst_ref, kw_ref, vwt_ref, gt_ref, ovl_ref, exp_ref,
                prev_ref, o_ref, *, step):
    del prev_ref
    hg, dh = HEADS_PER_GROUP, NSA_HEAD_DIM
    chains = range(Q_CHAINS)
    grp = pl.program_id(1)
    t0 = [step * Q_STEP + c * Q_BLOCK for c in chains]
    t_end = (step + 1) * Q_STEP
    c1 = (dh ** -0.5) * LOG2E
    slope2 = [slope_ref[grp * hg + h] for h in range(hg)]

    qt = [jnp.concatenate([qt_ref[h * dh:(h + 1) * dh, c * Q_BLOCK:(c + 1) * Q_BLOCK] for h in range(hg)], axis=1)
          for c in chains]
    t_row = [t0[c] + lax.broadcasted_iota(jnp.int32, (1, Q_BLOCK), 1) for c in chains]

    n_cmp_pad = kc_ref.shape[3]
    n_slc = ovl_ref.shape[0]
    n_cmp = min(n_cmp_pad, _round_up((t_end - CMP_BLOCK) // CMP_STRIDE + 1, 16))
    n_cand = min(n_slc, t_end // SLC_BLOCK)
    n_blk = min(n_slc, _round_up(n_cand, 8))
    cstart = lax.broadcasted_iota(jnp.int32, (n_cmp, Q_BLOCK), 0) * CMP_STRIDE
    centre = cstart.astype(F32) + (CMP_BLOCK - 1) / 2
    jj = lax.broadcasted_iota(jnp.int32, (n_blk, Q_BLOCK), 0)
    kc, vct, ovl = kc_ref[0, 0, 0, 0:n_cmp, :], vct_ref[0, 0, 0], ovl_ref[0:n_blk, :]
    o_cmp, member = [], []
    for c in chains:
        centre_dist = t_row[c].astype(F32) - centre
        valid = (cstart + (CMP_BLOCK - 1)) <= t_row[c]
        sc = _dot(kc, qt[c])
        p_heads = []
        for h in range(hg):
            s = _head(sc, h) * c1 - slope2[h] * centre_dist
            s = jnp.where(valid, s, -jnp.inf)
            m = jnp.max(s, axis=0, keepdims=True)
            m = jnp.where(jnp.isfinite(m), m, 0.0)
            e = jnp.where(valid, jnp.exp2(s - m), 0.0)
            p = e / jnp.maximum(jnp.sum(e, axis=0, keepdims=True), F32_TINY)
            p_heads.append(_pad_rows(p.astype(BF16), n_cmp_pad))
        o_cmp.append(_dot(vct, jnp.concatenate(p_heads, axis=1)))

        imp = _dot(ovl, p_heads[0])
        for h in range(1, hg):
            imp += _dot(ovl, p_heads[h])
        blk_t = t_row[c] // SLC_BLOCK
        forced = (jj == 0) | (jj == blk_t) | (jj == blk_t - 1)
        imp = jnp.where(forced, SEL_FORCE, jnp.where(jj <= blk_t, imp, -SEL_FORCE))
        rank = jnp.zeros((n_blk, Q_BLOCK), jnp.int32)
        for i in range(n_cand):
            ri = imp[i:i + 1, :]
            before = (ri > imp) | ((ri == imp) & (jj > i))
            rank += before.astype(jnp.int32)
        member.append(_pad_rows(jnp.where(rank < min(SLC_TOP_N, n_slc), 1.0, 0.0).astype(BF16), n_slc))

    key_iota = {}

    def key_local(n):
        if n not in key_iota:
            key_iota[n] = lax.broadcasted_iota(jnp.int32, (n, Q_BLOCK), 0)
        return key_iota[n]

    def tiles(lo, hi):
        return [(p, min(KV_TILE, hi - p)) for p in range(lo, hi, KV_TILE)]

    streams = []
    for c in chains:
        streams.append([("slc", c, ks_ref, vst_ref, lo, n) for lo, n in tiles(0, t0[c] + Q_BLOCK)])
    for c in chains:
        streams.append([("win", c, kw_ref, vwt_ref, lo, n)
                        for lo, n in tiles(max(t0[c] - WINDOW, 0), t0[c] + Q_BLOCK)])
    items = [s[i] for i in range(max(len(s) for s in streams)) for s in streams if i < len(s)]

    def scores(item):
        _, c, k_ref, _, lo, n = item
        return _dot(k_ref[lo:lo + n, :], qt[c])

    def finish(state, item, sc):
        branch, c, _, vt_ref, lo, n = item
        dist = t_row[c] - (lo + key_local(n))
        masks = []
        if branch == "slc":
            masks.append(_dot(exp_ref[lo:lo + n, :], member[c]) > 0.5)
        if lo + n - 1 > t0[c]:
            masks.append(dist >= 0)
        if branch == "win" and t0[c] + Q_BLOCK - 1 - lo >= WINDOW:
            masks.append(dist < WINDOW)
        mask = functools.reduce(jnp.logical_and, masks) if masks else None
        dist_f = dist.astype(F32)
        m_new, l_new, alpha, p_t = [], [], [], []
        for h in range(hg):
            s = _head(sc, h) * c1 - slope2[h] * dist_f
            if mask is not None:
                s = jnp.where(mask, s, NEG_BIG)
            m_n = jnp.max(s, axis=0, keepdims=True)
            if state is not None:
                m_o = _head(state[0], h)
                m_n = jnp.maximum(m_o, m_n)
                a = jnp.exp2(m_o - m_n)
                alpha.append(a)
            p = jnp.exp2(s - m_n)
            l_n = jnp.sum(p, axis=0, keepdims=True)
            if state is not None:
                l_n = a * _head(state[1], h) + l_n
            m_new.append(m_n)
            l_new.append(l_n)
            p_t.append(p.astype(BF16))
        acc = _dot(vt_ref[:, lo:lo + n], jnp.concatenate(p_t, axis=1))
        if state is not None:
            acc = jnp.concatenate(alpha, axis=1) * state[2] + acc
        return jnp.concatenate(m_new, axis=1), jnp.concatenate(l_new, axis=1), acc

    states = {}
    sc_next = scores(items[0])
    for i, item in enumerate(items):
        sc = sc_next
        if i + 1 < len(items):
            sc_next = scores(items[i + 1])
        states[item[:2]] = finish(states.get(item[:2]), item, sc)

    for c in chains:
        gt = gt_ref[0, :, c * Q_BLOCK:(c + 1) * Q_BLOCK]

        def gate(br):
            return jnp.concatenate([gt[br * hg + h:br * hg + h + 1, :] for h in range(hg)], axis=1)

        (_, l_s, acc_s), (_, l_w, acc_w) = states[("slc", c)], states[("win", c)]
        o = (gate(0) * o_cmp[c] + (gate(1) / jnp.maximum(l_s, F32_TINY)) * acc_s
             + (gate(2) / jnp.maximum(l_w, F32_TINY)) * acc_w)
        o_ref[c * Q_BLOCK:(c + 1) * Q_BLOCK, :] = jnp.concatenate(
            [_head(o, h).T for h in range(hg)], axis=1).astype(o_ref.dtype)


def _nsa(qt, kv_cmp, k_slc, vt_slc, k_win, vt_win, gates_t, slopes2, ovl, expand, batch, seq):
    g = NSA_KV_GROUPS
    gw = HEADS_PER_GROUP * NSA_HEAD_DIM
    nq = seq // Q_STEP
    n_cmp_pad = kv_cmp.shape[3]
    out = jnp.zeros((batch * seq, NSA_WIDTH), BF16)
    for step in range(nq):
        per_seq = 1
        while seq // (2 * per_seq) >= (step + 1) * Q_STEP:
            per_seq *= 2
        n_keys = seq // per_seq
        k_spec = pl.BlockSpec((n_keys, NSA_HEAD_DIM), lambda b, gg, per_seq=per_seq: (b * per_seq, gg))
        vt_spec = pl.BlockSpec((NSA_HEAD_DIM, n_keys), lambda b, gg, per_seq=per_seq: (gg, b * per_seq))
        row_blk = lambda b, gg, step=step: (b * nq + step, gg)
        out = pl.pallas_call(
            functools.partial(_nsa_kernel, step=step),
            grid=(batch, g),
            in_specs=[pl.BlockSpec(memory_space=pltpu.SMEM),
                      pl.BlockSpec((gw, Q_STEP), lambda b, gg, step=step: (gg, b * nq + step)),
                      pl.BlockSpec((1, 1, 1, n_cmp_pad, NSA_HEAD_DIM), lambda b, gg: (0, b, gg, 0, 0)),
                      pl.BlockSpec((1, 1, 1, NSA_HEAD_DIM, n_cmp_pad), lambda b, gg: (1, b, gg, 0, 0)),
                      k_spec, vt_spec, k_spec, vt_spec,
                      pl.BlockSpec((1, GATE_ROWS, Q_STEP), lambda b, gg, step=step: (gg, 0, b * nq + step)),
                      pl.BlockSpec(ovl.shape, lambda b, gg: (0, 0)),
                      pl.BlockSpec(expand.shape, lambda b, gg: (0, 0)),
                      pl.BlockSpec(memory_space=pl.ANY)],
            out_specs=pl.BlockSpec((Q_STEP, gw), row_blk),
            out_shape=jax.ShapeDtypeStruct((batch * seq, NSA_WIDTH), BF16),
            input_output_aliases={11: 0},
            compiler_params=_params("parallel", "parallel"),
            name=f"nsa{step}",
        )(slopes2, qt, kv_cmp, kv_cmp, k_slc, vt_slc, k_win, vt_win, gates_t, ovl, expand, out)
    return out


def _ret_kernel(cd_ref, q_ref, k_ref, v_ref, g_ref, gain_ref, decay_ref, zeta_ref, xi_ref, o_ref, state_ref):
    @pl.when(pl.program_id(1) == 0)
    def _():
        state_ref[...] = jnp.zeros_like(state_ref)

    dk, dv = RET_KEY_DIM, RET_VAL_DIM
    for h in range(RET_HEADS):
        q = q_ref[:, h * dk:(h + 1) * dk]
        k = k_ref[:, h * dk:(h + 1) * dk]
        v = v_ref[:, h * dv:(h + 1) * dv]
        sc = _dot_nt(q.astype(BF16), k.astype(BF16)) * decay_ref[h]
        inner = _dot(sc.astype(BF16), v)
        st = state_ref[h]
        cross = _dot((q * xi_ref[h]).astype(BF16), st.astype(BF16))
        kz_t = (k * zeta_ref[h]).T.astype(BF16)
        state_ref[h] = st * cd_ref[h] + _dot(kz_t, v)
        y = inner + cross
        yc = y - jnp.mean(y, axis=-1, keepdims=True)
        yn = yc * lax.rsqrt(jnp.mean(yc * yc, axis=-1, keepdims=True) + EPS) * gain_ref[:, h * dv:(h + 1) * dv]
        o_ref[:, h * dv:(h + 1) * dv] = (_silu(g_ref[:, h * dv:(h + 1) * dv]) * yn).astype(o_ref.dtype)


def _retention(q, k, v, g, gain, consts, batch, seq):
    chunk_decay, decay, zeta, xi = consts
    c = RET_CHUNK
    nc = seq // c
    row = lambda b, ci: (b * nc + ci, 0)
    whole3 = lambda b, ci: (0, 0, 0)
    return pl.pallas_call(
        _ret_kernel,
        grid=(batch, nc),
        in_specs=[pl.BlockSpec(memory_space=pltpu.SMEM),
                  pl.BlockSpec((c, RET_QK_WIDTH), row),
                  pl.BlockSpec((c, RET_QK_WIDTH), row),
                  pl.BlockSpec((c, RET_V_WIDTH), row),
                  pl.BlockSpec((c, RET_V_WIDTH), row),
                  pl.BlockSpec((1, RET_V_WIDTH), lambda b, ci: (0, 0)),
                  pl.BlockSpec((RET_HEADS, c, c), whole3),
                  pl.BlockSpec((RET_HEADS, c, 1), whole3),
                  pl.BlockSpec((RET_HEADS, c, 1), whole3)],
        out_specs=pl.BlockSpec((c, RET_V_WIDTH), row),
        out_shape=jax.ShapeDtypeStruct((batch * seq, RET_V_WIDTH), BF16),
        scratch_shapes=[pltpu.VMEM((RET_HEADS, RET_KEY_DIM, RET_VAL_DIM), F32)],
        compiler_params=_params("parallel", "arbitrary"),
        name="retention",
    )(chunk_decay, q, k, v, g, gain, decay, zeta, xi)


def _retention_consts():
    c = RET_CHUNK
    log_gamma = jnp.log1p(-jnp.exp2(-5.0 - jnp.arange(RET_HEADS, dtype=F32)))
    n = jnp.arange(c, dtype=F32)
    diff = n[:, None] - n[None, :]
    decay = jnp.where(diff >= 0, jnp.exp(log_gamma[:, None, None] * jnp.maximum(diff, 0.0)), 0.0)
    zeta = jnp.exp(log_gamma[:, None] * (c - 1 - n)[None, :])[:, :, None]
    xi = jnp.exp(log_gamma[:, None] * (n + 1.0)[None, :])[:, :, None]
    chunk_decay = jnp.exp(log_gamma * c)
    return chunk_decay, decay, zeta, xi


def _merge_kernel(x_ref, gt_ref, on_ref, or_ref, ga_ref, gb_ref, wpn_ref, wpr_ref, wo_ref, o_ref, acc_ref):
    j = pl.program_id(1)

    @pl.when(j == 0)
    def _():
        acc_ref[...] = jnp.zeros_like(acc_ref)

    a = _dot(on_ref[...], wpn_ref[...])
    b = _dot(or_ref[...], wpr_ref[...])
    merged = jax.nn.sigmoid(ga_ref[...]) * a + jax.nn.sigmoid(gb_ref[...]) * b
    acc_ref[...] += _dot(merged.astype(BF16), wo_ref[...])

    @pl.when(j == pl.num_programs(1) - 1)
    def _():
        o_ref[...] = x_ref[...] + gt_ref[0] * acc_ref[...]


def _merge(x2, gate, o_nsa, o_ret, ga, gb, wpn, wpr, wo, seq):
    t, d = x2.shape
    tm, tn = 512, 512
    per_b = seq // tm
    return pl.pallas_call(
        _merge_kernel,
        grid=(t // tm, d // tn),
        in_specs=[pl.BlockSpec((tm, d), lambda i, j: (i, 0)),
                  pl.BlockSpec((1, 1, d), lambda i, j: (i // per_b, 0, 0)),
                  pl.BlockSpec((tm, o_nsa.shape[1]), lambda i, j: (i, 0)),
                  pl.BlockSpec((tm, o_ret.shape[1]), lambda i, j: (i, 0)),
                  pl.BlockSpec((tm, tn), lambda i, j: (i, j)),
                  pl.BlockSpec((tm, tn), lambda i, j: (i, j)),
                  pl.BlockSpec((wpn.shape[0], tn), lambda i, j: (0, j)),
                  pl.BlockSpec((wpr.shape[0], tn), lambda i, j: (0, j)),
                  pl.BlockSpec((tn, d), lambda i, j: (j, 0))],
        out_specs=pl.BlockSpec((tm, d), lambda i, j: (i, 0)),
        out_shape=jax.ShapeDtypeStruct((t, d), F32),
        scratch_shapes=[pltpu.VMEM((tm, d), F32)],
        compiler_params=_params("parallel", "arbitrary"),
        name="merge",
    )(x2, gate, o_nsa, o_ret, ga, gb, wpn, wpr, wo)


def _nsa_tables(seq):
    n_cmp = (seq - CMP_BLOCK) // CMP_STRIDE + 1
    n_cmp_pad = seq // CMP_STRIDE
    n_slc = seq // SLC_BLOCK
    cs = (np.arange(n_cmp) * CMP_STRIDE)[:, None]
    js = (np.arange(n_slc) * SLC_BLOCK)[None, :]
    overlap = np.clip(np.minimum(cs + CMP_BLOCK, js + SLC_BLOCK) - np.maximum(cs, js), 0, None) / CMP_BLOCK
    ovl = np.zeros((n_slc, n_cmp_pad), np.float32)
    ovl[:, :n_cmp] = overlap.T
    expand = (np.arange(seq)[:, None] // SLC_BLOCK == np.arange(n_slc)[None, :]).astype(np.float32)
    slopes2 = jnp.exp2(-8.0 * jnp.arange(1, NSA_HEADS + 1, dtype=F32) / NSA_HEADS) * LOG2E
    return jnp.asarray(ovl, BF16), jnp.asarray(expand, BF16), slopes2


def kernel(x, c, w_ada, b_ada, g_norm, w_ffn_gate, w_ffn_up, w_ffn_down, w_in, g_qk, cmp_pos, cmp_w1, cmp_b1,
           cmp_w2, ret_gn_gain, w_proj_nsa, w_proj_ret, w_out):
    batch, seq, d = x.shape
    depth = w_ada.shape[0]
    t = batch * seq
    x2 = x.reshape(t, d)
    c_pad = jnp.pad(c, ((0, 8 - batch), (0, 0)))
    ovl, expand, slopes2 = _nsa_tables(seq)
    ret_consts = _retention_consts()
    hg = HEADS_PER_GROUP

    for l in range(depth):
        ada = _ada(c_pad, w_ada[l], b_ada[l][None, :])[:batch].reshape(batch, N_ADA, 1, d)
        sh1, sc1, gt1, sh2, sc2, gt2, sh3, sc3, gt3 = [ada[:, i] for i in range(N_ADA)]

        x2 = _ffn(x2, sh1, sc1, gt1, g_norm[l, 0][None, :], w_ffn_gate[l, 0], w_ffn_up[l, 0], w_ffn_down[l, 0], seq)

        u = _mod(x2, sh2, sc2, g_norm[l, 1][None, :], seq)
        w = w_in[l]
        off = [0]

        def take(width, transposed=False):
            w_part = w[:, off[0]:off[0] + width]
            off[0] += width
            return (w_part.T if transposed else w_part).astype(BF16)

        gq = g_qk[l]
        qt_nsa = _proj_t(u, take(NSA_WIDTH, True), BF16, gq[0], name="proj_q")
        cmp_raw = _proj(u, take(2 * KV_WIDTH), F32, name="proj_cmp_raw")
        k_slc = _proj(u, take(KV_WIDTH), BF16, "headnorm", gq[2][None, :], name="proj_k_slc")
        vt_slc = _proj_t(u, take(KV_WIDTH, True), BF16, name="proj_v_slc")
        k_win = _proj(u, take(KV_WIDTH), BF16, "headnorm", gq[3][None, :], name="proj_k_win")
        vt_win = _proj_t(u, take(KV_WIDTH, True), BF16, name="proj_v_win")
        w_gl = take(3 * NSA_HEADS).reshape(d, 3, NSA_KV_GROUPS, hg).transpose(0, 2, 1, 3)
        w_gl = jnp.pad(w_gl.reshape(d, NSA_KV_GROUPS, 3 * hg),
                       ((0, 0), (0, 0), (0, LANES - 3 * hg))).reshape(d, NSA_KV_GROUPS * LANES)
        gates = _proj(u, w_gl, F32, "sigmoid", name="proj_gates")
        gates_t = gates.reshape(t, NSA_KV_GROUPS, LANES)[:, :, :GATE_ROWS].transpose(1, 2, 0)
        q_r = _proj(u, take(RET_QK_WIDTH), F32, name="proj_q_ret")
        k_r = _proj(u, take(RET_QK_WIDTH), F32, "keyscale", name="proj_k_ret")
        v_r = _proj(u, take(RET_V_WIDTH), BF16, name="proj_v_ret")
        g_r = _proj(u, take(RET_V_WIDTH), F32, name="proj_g_ret")
        ga = _proj(u, take(d), F32, name="proj_ga")
        gb = _proj(u, take(d), F32, name="proj_gb")

        kv_cmp = _compress(cmp_raw.reshape(batch, seq, 2 * KV_WIDTH), cmp_pos[l], cmp_w1[l].astype(BF16),
                           cmp_b1[l][:, None, :], cmp_w2[l].astype(BF16), gq[1][None, :])
        o_nsa = _nsa(qt_nsa, kv_cmp, k_slc, vt_slc, k_win, vt_win, gates_t, slopes2, ovl, expand, batch, seq)
        o_ret = _retention(q_r, k_r, v_r, g_r, ret_gn_gain[l].reshape(1, RET_V_WIDTH), ret_consts, batch, seq)

        x2 = _merge(x2, gt2, o_nsa, o_ret, ga, gb, w_proj_nsa[l].astype(BF16), w_proj_ret[l].astype(BF16),
                    w_out[l].astype(BF16), seq)

        x2 = _ffn(x2, sh3, sc3, gt3, g_norm[l, 2][None, :], w_ffn_gate[l, 1], w_ffn_up[l, 1], w_ffn_down[l, 1], seq)

    return x2.reshape(batch, seq, d)
```

```python
import functools
import math

import numpy as np
import jax
import jax.numpy as jnp
from jax import lax
from jax.experimental import pallas as pl
from jax.experimental.pallas import tpu as pltpu

F32 = jnp.float32
BF16 = jnp.bfloat16

D_MODEL = 2048
NSA_HEADS = 16
NSA_KV_GROUPS = 4
HEADS_PER_GROUP = NSA_HEADS // NSA_KV_GROUPS
NSA_HEAD_DIM = 128
CMP_BLOCK = 32
CMP_STRIDE = 16
SLC_BLOCK = 64
SLC_TOP_N = 16
WINDOW = 512
RET_HEADS = 8
RET_KEY_DIM = 128
RET_VAL_DIM = 256
RET_CHUNK = 128
D_FF = 5632
N_ADA = 9
EPS = 1e-6
SEL_FORCE = 1e4

NSA_WIDTH = NSA_HEADS * NSA_HEAD_DIM
KV_WIDTH = NSA_KV_GROUPS * NSA_HEAD_DIM
RET_QK_WIDTH = RET_HEADS * RET_KEY_DIM
RET_V_WIDTH = RET_HEADS * RET_VAL_DIM

LANES = 128
VMEM_LIMIT_BYTES = 56 * 1024 * 1024
NEG_BIG = -1e30
F32_TINY = float(np.finfo(np.float32).tiny)
LOG2E = math.log2(math.e)

Q_BLOCK = 128
Q_CHAINS = 2
Q_STEP = Q_CHAINS * Q_BLOCK
KV_TILE = 256
GATE_ROWS = 16


def _params(*sem):
    return pltpu.CompilerParams(dimension_semantics=sem, vmem_limit_bytes=VMEM_LIMIT_BYTES)


def _silu(x):
    return x * jax.nn.sigmoid(x)


def _rms(x, g):
    return x * lax.rsqrt(jnp.mean(x * x, axis=-1, keepdims=True) + EPS) * g


def _modulate(x, g, shift, scale):
    return _rms(x, g) * (1.0 + scale) + shift


def _dot(a, b):
    return jnp.dot(a, b, preferred_element_type=F32)


def _dot_nt(a, b):
    return lax.dot_general(a, b, (((1,), (1,)), ((), ())), preferred_element_type=F32)


def _ada_kernel(c_ref, w_ref, b_ref, o_ref):
    cond = _silu(c_ref[...]).astype(BF16)
    o_ref[...] = _dot(cond, w_ref[...].astype(BF16)) + b_ref[...]


def _ada(c_pad, w, b):
    rows, d = c_pad.shape
    n = w.shape[1]
    tn = 1024
    return pl.pallas_call(
        _ada_kernel,
        grid=(n // tn,),
        in_specs=[pl.BlockSpec((rows, d), lambda j: (0, 0)),
                  pl.BlockSpec((d, tn), lambda j: (0, j)),
                  pl.BlockSpec((1, tn), lambda j: (0, j))],
        out_specs=pl.BlockSpec((rows, tn), lambda j: (0, j)),
        out_shape=jax.ShapeDtypeStruct((rows, n), F32),
        compiler_params=_params("parallel"),
        name="ada",
    )(c_pad, w, b)


def _ffn_kernel(x_ref, sh_ref, sc_ref, gt_ref, g_ref, wg_ref, wu_ref, wd_ref, o_ref, h_ref):
    j = pl.program_id(1)

    @pl.when(j == 0)
    def _():
        h_ref[...] = _modulate(x_ref[...], g_ref[...], sh_ref[0], sc_ref[0]).astype(BF16)
        o_ref[...] = jnp.zeros_like(o_ref)

    h = h_ref[...]
    a = _dot(h, wg_ref[...])
    b = _dot(h, wu_ref[...])
    act = (_silu(a) * b).astype(BF16)
    o_ref[...] += _dot(act, wd_ref[...])

    @pl.when(j == pl.num_programs(1) - 1)
    def _():
        o_ref[...] = x_ref[...] + (0.5 * gt_ref[0]) * o_ref[...]


def _ffn(x2, shift, scale, gate, g, wg, wu, wd, seq):
    t, d = x2.shape
    ff = wg.shape[1]
    tm, tf = 1024, 512
    per_b = seq // tm
    mod_spec = pl.BlockSpec((1, 1, d), lambda i, j: (i // per_b, 0, 0))
    return pl.pallas_call(
        _ffn_kernel,
        grid=(t // tm, ff // tf),
        in_specs=[pl.BlockSpec((tm, d), lambda i, j: (i, 0), pipeline_mode=pl.Buffered(1)),
                  mod_spec, mod_spec, mod_spec,
                  pl.BlockSpec((1, d), lambda i, j: (0, 0)),
                  pl.BlockSpec((d, tf), lambda i, j: (0, j)),
                  pl.BlockSpec((d, tf), lambda i, j: (0, j)),
                  pl.BlockSpec((tf, d), lambda i, j: (j, 0))],
        out_specs=pl.BlockSpec((tm, d), lambda i, j: (i, 0)),
        out_shape=jax.ShapeDtypeStruct((t, d), F32),
        scratch_shapes=[pltpu.VMEM((tm, d), BF16)],
        compiler_params=_params("parallel", "arbitrary"),
        name="ffn",
    )(x2, shift, scale, gate, g, wg, wu, wd)


def _mod_kernel(x_ref, sh_ref, sc_ref, g_ref, o_ref):
    o_ref[...] = _modulate(x_ref[...], g_ref[...], sh_ref[0], sc_ref[0]).astype(BF16)


def _mod(x2, shift, scale, g, seq):
    t, d = x2.shape
    tm = 512
    per_b = seq // tm
    mod_spec = pl.BlockSpec((1, 1, d), lambda i: (i // per_b, 0, 0))
    return pl.pallas_call(
        _mod_kernel,
        grid=(t // tm,),
        in_specs=[pl.BlockSpec((tm, d), lambda i: (i, 0)), mod_spec, mod_spec,
                  pl.BlockSpec((1, d), lambda i: (0, 0))],
        out_specs=pl.BlockSpec((tm, d), lambda i: (i, 0)),
        out_shape=jax.ShapeDtypeStruct((t, d), BF16),
        compiler_params=_params("parallel"),
        name="modulate",
    )(x2, shift, scale, g)


def _proj_kernel(u_ref, w_ref, gain_ref, o_ref, *, epilogue):
    acc = _dot(u_ref[...], w_ref[...])
    if epilogue == "headnorm":
        g = gain_ref[...]
        parts = [_rms(acc[:, k:k + NSA_HEAD_DIM], g) for k in range(0, acc.shape[1], NSA_HEAD_DIM)]
        acc = jnp.concatenate(parts, axis=1)
    elif epilogue == "keyscale":
        acc = acc * (RET_KEY_DIM ** -0.5)
    elif epilogue == "sigmoid":
        acc = jax.nn.sigmoid(acc)
    o_ref[...] = acc.astype(o_ref.dtype)


def _proj(u, w, out_dtype, epilogue="plain", gain=None, name="proj"):
    t, d = u.shape
    n = w.shape[1]
    tm = 1024
    tn = 1024 if n % 1024 == 0 else 512
    if gain is None:
        gain = jnp.ones((1, NSA_HEAD_DIM), F32)
    return pl.pallas_call(
        functools.partial(_proj_kernel, epilogue=epilogue),
        grid=(t // tm, n // tn),
        in_specs=[pl.BlockSpec((tm, d), lambda i, j: (i, 0)),
                  pl.BlockSpec((d, tn), lambda i, j: (0, j)),
                  pl.BlockSpec((1, NSA_HEAD_DIM), lambda i, j: (0, 0))],
        out_specs=pl.BlockSpec((tm, tn), lambda i, j: (i, j)),
        out_shape=jax.ShapeDtypeStruct((t, n), out_dtype),
        compiler_params=_params("parallel", "parallel"),
        name=name,
    )(u, w, gain)


def _proj_t_kernel(u_ref, wt_ref, gain_ref, o_ref, *, headnorm):
    acc = _dot_nt(wt_ref[...], u_ref[...])
    if headnorm:
        g = gain_ref[...]
        parts = []
        for k in range(0, acc.shape[0], NSA_HEAD_DIM):
            xh = acc[k:k + NSA_HEAD_DIM]
            parts.append(xh * lax.rsqrt(jnp.mean(xh * xh, axis=0, keepdims=True) + EPS) * g)
        acc = jnp.concatenate(parts, axis=0)
    o_ref[...] = acc.astype(o_ref.dtype)


def _proj_t(u, wt, out_dtype, gain=None, name="proj_t"):
    t, d = u.shape
    n = wt.shape[0]
    tm = 1024
    tn = 1024 if n % 1024 == 0 else 512
    headnorm = gain is not None
    if gain is None:
        gain = jnp.ones((NSA_HEAD_DIM,), F32)
    return pl.pallas_call(
        functools.partial(_proj_t_kernel, headnorm=headnorm),
        grid=(t // tm, n // tn),
        in_specs=[pl.BlockSpec((tm, d), lambda i, j: (i, 0)),
                  pl.BlockSpec((tn, d), lambda i, j: (j, 0)),
                  pl.BlockSpec((NSA_HEAD_DIM, 1), lambda i, j: (0, 0))],
        out_specs=pl.BlockSpec((tn, tm), lambda i, j: (j, i)),
        out_shape=jax.ShapeDtypeStruct((n, t), out_dtype),
        compiler_params=_params("parallel", "parallel"),
        name=name,
    )(u, wt, gain[:, None])


def _cmp_kernel(z_ref, pos_ref, w1_ref, b1_ref, w2_ref, gk_ref, o_ref):
    half = CMP_BLOCK // 2
    n_rows = z_ref.shape[1] // CMP_STRIDE
    y_lo = jnp.zeros((n_rows, NSA_HEAD_DIM), F32)
    y_hi = jnp.zeros((n_rows, NSA_HEAD_DIM), F32)
    for l in range(half):
        zl = z_ref[0, pl.ds(l, n_rows, stride=CMP_STRIDE), :]
        lo = (zl + pos_ref[0, l:l + 1, :]).astype(BF16)
        hi = (zl + pos_ref[0, half + l:half + l + 1, :]).astype(BF16)
        y_lo += _dot(lo, w1_ref[0, l * NSA_HEAD_DIM:(l + 1) * NSA_HEAD_DIM, :])
        y_hi += _dot(hi, w1_ref[0, (half + l) * NSA_HEAD_DIM:(half + l + 1) * NSA_HEAD_DIM, :])
    y = y_lo + pltpu.roll(y_hi, n_rows - 1, 0)
    hdn = jax.nn.gelu(y + b1_ref[0]).astype(BF16)
    out = _dot(hdn, w2_ref[0])
    normed = _rms(out, gk_ref[...])
    is_key = pl.program_id(0) == 0
    o_ref[0, 0, 0] = jnp.where(is_key, normed, out.T).astype(o_ref.dtype)


def _compress(raw, pos, w1, b1, w2, gk):
    b, s, _ = raw.shape
    g = NSA_KV_GROUPS
    n_rows = s // CMP_STRIDE
    assert n_rows == NSA_HEAD_DIM, "key / transposed-value tiles share one square output block"
    return pl.pallas_call(
        _cmp_kernel,
        grid=(2, b, g),
        in_specs=[pl.BlockSpec((1, s, NSA_HEAD_DIM), lambda i, bb, gg: (bb, 0, i * g + gg)),
                  pl.BlockSpec((1, CMP_BLOCK, NSA_HEAD_DIM), lambda i, bb, gg: (i, 0, 0)),
                  pl.BlockSpec((1, CMP_BLOCK * NSA_HEAD_DIM, NSA_HEAD_DIM), lambda i, bb, gg: (i, 0, 0)),
                  pl.BlockSpec((1, 1, NSA_HEAD_DIM), lambda i, bb, gg: (i, 0, 0)),
                  pl.BlockSpec((1, NSA_HEAD_DIM, NSA_HEAD_DIM), lambda i, bb, gg: (i, 0, 0)),
                  pl.BlockSpec((1, NSA_HEAD_DIM), lambda i, bb, gg: (0, 0))],
        out_specs=pl.BlockSpec((1, 1, 1, n_rows, NSA_HEAD_DIM), lambda i, bb, gg: (i, bb, gg, 0, 0)),
        out_shape=jax.ShapeDtypeStruct((2, b, g, n_rows, NSA_HEAD_DIM), BF16),
        compiler_params=_params("parallel", "parallel", "parallel"),
        name="compress",
    )(raw, pos, w1, b1, w2, gk)


def _head(x, h):
    return x[:, h * Q_BLOCK:(h + 1) * Q_BLOCK]


def _round_up(x, m):
    return (x + m - 1) // m * m


def _pad_rows(x, n):
    if x.shape[0] == n:
        return x
    return jnp.concatenate([x, jnp.zeros((n - x.shape[0], x.shape[1]), x.dtype)], axis=0)


def _nsa_kernel(slope_ref, qt_ref, kc_ref, vct_ref, ks_ref, vst_ref, kw_ref, vwt_ref, gt_ref, ovl_ref, exp_ref,
                prev_ref, o_ref, *, step):
    del prev_ref
    hg, dh = HEADS_PER_GROUP, NSA_HEAD_DIM
    chains = range(Q_CHAINS)
    grp = pl.program_id(1)
    t0 = [step * Q_STEP + c * Q_BLOCK for c in chains]
    t_end = (step + 1) * Q_STEP
    c1 = (dh ** -0.5) * LOG2E
    slope2 = [slope_ref[grp * hg + h] for h in range(hg)]

    qt = [jnp.concatenate([qt_ref[h * dh:(h + 1) * dh, c * Q_BLOCK:(c + 1) * Q_BLOCK] for h in range(hg)], axis=1)
          for c in chains]
    t_row = [t0[c] + lax.broadcasted_iota(jnp.int32, (1, Q_BLOCK), 1) for c in chains]

    n_cmp_pad = kc_ref.shape[3]
    n_slc = ovl_ref.shape[0]
    n_cmp = min(n_cmp_pad, _round_up((t_end - CMP_BLOCK) // CMP_STRIDE + 1, 16))
    n_cand = min(n_slc, t_end // SLC_BLOCK)
    n_blk = min(n_slc, _round_up(n_cand, 8))
    cstart = lax.broadcasted_iota(jnp.int32, (n_cmp, Q_BLOCK), 0) * CMP_STRIDE
    centre = cstart.astype(F32) + (CMP_BLOCK - 1) / 2
    jj = lax.broadcasted_iota(jnp.int32, (n_blk, Q_BLOCK), 0)
    kc, vct, ovl = kc_ref[0, 0, 0, 0:n_cmp, :], vct_ref[0, 0, 0], ovl_ref[0:n_blk, :]
    o_cmp, member = [], []
    for c in chains:
        centre_dist = t_row[c].astype(F32) - centre
        valid = (cstart + (CMP_BLOCK - 1)) <= t_row[c]
        sc = _dot(kc, qt[c])
        p_heads = []
        for h in range(hg):
            s = _head(sc, h) * c1 - slope2[h] * centre_dist
            s = jnp.where(valid, s, -jnp.inf)
            m = jnp.max(s, axis=0, keepdims=True)
            m = jnp.where(jnp.isfinite(m), m, 0.0)
            e = jnp.where(valid, jnp.exp2(s - m), 0.0)
            p = e / jnp.maximum(jnp.sum(e, axis=0, keepdims=True), F32_TINY)
            p_heads.append(_pad_rows(p.astype(BF16), n_cmp_pad))
        o_cmp.append(_dot(vct, jnp.concatenate(p_heads, axis=1)))

        imp = _dot(ovl, p_heads[0])
        for h in range(1, hg):
            imp += _dot(ovl, p_heads[h])
        blk_t = t_row[c] // SLC_BLOCK
        forced = (jj == 0) | (jj == blk_t) | (jj == blk_t - 1)
        imp = jnp.where(forced, SEL_FORCE, jnp.where(jj <= blk_t, imp, -SEL_FORCE))
        rank = jnp.zeros((n_blk, Q_BLOCK), jnp.int32)
        for i in range(n_cand):
            ri = imp[i:i + 1, :]
            before = (ri > imp) | ((ri == imp) & (jj > i))
            rank += before.astype(jnp.int32)
        member.append(_pad_rows(jnp.where(rank < min(SLC_TOP_N, n_slc), 1.0, 0.0).astype(BF16), n_slc))

    key_iota = {}

    def key_local(n):
        if n not in key_iota:
            key_iota[n] = lax.broadcasted_iota(jnp.int32, (n, Q_BLOCK), 0)
        return key_iota[n]

    def tiles(lo, hi):
        return [(p, min(KV_TILE, hi - p)) for p in range(lo, hi, KV_TILE)]

    streams = []
    for c in chains:
        streams.append([("slc", c, ks_ref, vst_ref, lo, n) for lo, n in tiles(0, t0[c] + Q_BLOCK)])
    for c in chains:
        streams.append([("win", c, kw_ref, vwt_ref, lo, n)
                        for lo, n in tiles(max(t0[c] - WINDOW, 0), t0[c] + Q_BLOCK)])
    items = [s[i] for i in range(max(len(s) for s in streams)) for s in streams if i < len(s)]

    def scores(item):
        _, c, k_ref, _, lo, n = item
        return _dot(k_ref[lo:lo + n, :], qt[c])

    def finish(state, item, sc):
        branch, c, _, vt_ref, lo, n = item
        dist = t_row[c] - (lo + key_local(n))
        masks = []
        if branch == "slc":
            masks.append(_dot(exp_ref[lo:lo + n, :], member[c]) > 0.5)
        if lo + n - 1 > t0[c]:
            masks.append(dist >= 0)
        if branch == "win" and t0[c] + Q_BLOCK - 1 - lo >= WINDOW:
            masks.append(dist < WINDOW)
        mask = functools.reduce(jnp.logical_and, masks) if masks else None
        dist_f = dist.astype(F32)
        m_new, l_new, alpha, p_t = [], [], [], []
        for h in range(hg):
            s = _head(sc, h) * c1 - slope2[h] * dist_f
            if mask is not None:
                s = jnp.where(mask, s, NEG_BIG)
            m_n = jnp.max(s, axis=0, keepdims=True)
            if state is not None:
                m_o = _head(state[0], h)
                m_n = jnp.maximum(m_o, m_n)
                a = jnp.exp2(m_o - m_n)
                alpha.append(a)
            p = jnp.exp2(s - m_n)
            l_n = jnp.sum(p, axis=0, keepdims=True)
            if state is not None:
                l_n = a * _head(state[1], h) + l_n
            m_new.append(m_n)
            l_new.append(l_n)
            p_t.append(p.astype(BF16))
        acc = _dot(vt_ref[:, lo:lo + n], jnp.concatenate(p_t, axis=1))
        if state is not None:
            acc = jnp.concatenate(alpha, axis=1) * state[2] + acc
        return jnp.concatenate(m_new, axis=1), jnp.concatenate(l_new, axis=1), acc

    states = {}
    sc_next = scores(items[0])
    for i, item in enumerate(items):
        sc = sc_next
        if i + 1 < len(items):
            sc_next = scores(items[i + 1])
        states[item[:2]] = finish(states.get(item[:2]), item, sc)

    for c in chains:
        gt = gt_ref[0, :, c * Q_BLOCK:(c + 1) * Q_BLOCK]

        def gate(br):
            return jnp.concatenate([gt[br * hg + h:br * hg + h + 1, :] for h in range(hg)], axis=1)

        (_, l_s, acc_s), (_, l_w, acc_w) = states[("slc", c)], states[("win", c)]
        o = (gate(0) * o_cmp[c] + (gate(1) / jnp.maximum(l_s, F32_TINY)) * acc_s
             + (gate(2) / jnp.maximum(l_w, F32_TINY)) * acc_w)
        o_ref[c * Q_BLOCK:(c + 1) * Q_BLOCK, :] = jnp.concatenate(
            [_head(o, h).T for h in range(hg)], axis=1).astype(o_ref.dtype)


def _nsa(qt, kv_cmp, k_slc, vt_slc, k_win, vt_win, gates_t, slopes2, ovl, expand, batch, seq):
    g = NSA_KV_GROUPS
    gw = HEADS_PER_GROUP * NSA_HEAD_DIM
    nq = seq // Q_STEP
    n_cmp_pad = kv_cmp.shape[3]
    out = jnp.zeros((batch * seq, NSA_WIDTH), BF16)
    for step in range(nq):
        per_seq = 1
        while seq // (2 * per_seq) >= (step + 1) * Q_STEP:
            per_seq *= 2
        n_keys = seq // per_seq
        k_spec = pl.BlockSpec((n_keys, NSA_HEAD_DIM), lambda b, gg, per_seq=per_seq: (b * per_seq, gg))
        vt_spec = pl.BlockSpec((NSA_HEAD_DIM, n_keys), lambda b, gg, per_seq=per_seq: (gg, b * per_seq))
        row_blk = lambda b, gg, step=step: (b * nq + step, gg)
        out = pl.pallas_call(
            functools.partial(_nsa_kernel, step=step),
            grid=(batch, g),
            in_specs=[pl.BlockSpec(memory_space=pltpu.SMEM),
                      pl.BlockSpec((gw, Q_STEP), lambda b, gg, step=step: (gg, b * nq + step)),
                      pl.BlockSpec((1, 1, 1, n_cmp_pad, NSA_HEAD_DIM), lambda b, gg: (0, b, gg, 0, 0)),
                      pl.BlockSpec((1, 1, 1, NSA_HEAD_DIM, n_cmp_pad), lambda b, gg: (1, b, gg, 0, 0)),
                      k_spec, vt_spec, k_spec, vt_spec,
                      pl.BlockSpec((1, GATE_ROWS, Q_STEP), lambda b, gg, step=step: (gg, 0, b * nq + step)),
                      pl.BlockSpec(ovl.shape, lambda b, gg: (0, 0)),
                      pl.BlockSpec(expand.shape, lambda b, gg: (0, 0)),
                      pl.BlockSpec(memory_space=pl.ANY)],
            out_specs=pl.BlockSpec((Q_STEP, gw), row_blk),
            out_shape=jax.ShapeDtypeStruct((batch * seq, NSA_WIDTH), BF16),
            input_output_aliases={11: 0},
            compiler_params=_params("parallel", "parallel"),
            name=f"nsa{step}",
        )(slopes2, qt, kv_cmp, kv_cmp, k_slc, vt_slc, k_win, vt_win, gates_t, ovl, expand, out)
    return out


def _ret_kernel(cd_ref, q_ref, k_ref, v_ref, g_ref, gain_ref, decay_ref, zeta_ref, xi_ref, o_ref, state_ref):
    @pl.when(pl.program_id(1) == 0)
    def _():
        state_ref[...] = jnp.zeros_like(state_ref)

    dk, dv = RET_KEY_DIM, RET_VAL_DIM
    for h in range(RET_HEADS):
        q = q_ref[:, h * dk:(h + 1) * dk]
        k = k_ref[:, h * dk:(h + 1) * dk]
        v = v_ref[:, h * dv:(h + 1) * dv]
        sc = _dot_nt(q.astype(BF16), k.astype(BF16)) * decay_ref[h]
        inner = _dot(sc.astype(BF16), v)
        st = state_ref[h]
        cross = _dot((q * xi_ref[h]).astype(BF16), st.astype(BF16))
        kz_t = (k * zeta_ref[h]).T.astype(BF16)
        state_ref[h] = st * cd_ref[h] + _dot(kz_t, v)
        y = inner + cross
        yc = y - jnp.mean(y, axis=-1, keepdims=True)
        yn = yc * lax.rsqrt(jnp.mean(yc * yc, axis=-1, keepdims=True) + EPS) * gain_ref[:, h * dv:(h + 1) * dv]
        o_ref[:, h * dv:(h + 1) * dv] = (_silu(g_ref[:, h * dv:(h + 1) * dv]) * yn).astype(o_ref.dtype)


def _retention(q, k, v, g, gain, consts, batch, seq):
    chunk_decay, decay, zeta, xi = consts
    c = RET_CHUNK
    nc = seq // c
    row = lambda b, ci: (b * nc + ci, 0)
    whole3 = lambda b, ci: (0, 0, 0)
    return pl.pallas_call(
        _ret_kernel,
        grid=(batch, nc),
        in_specs=[pl.BlockSpec(memory_space=pltpu.SMEM),
                  pl.BlockSpec((c, RET_QK_WIDTH), row),
                  pl.BlockSpec((c, RET_QK_WIDTH), row),
                  pl.BlockSpec((c, RET_V_WIDTH), row),
                  pl.BlockSpec((c, RET_V_WIDTH), row),
                  pl.BlockSpec((1, RET_V_WIDTH), lambda b, ci: (0, 0)),
                  pl.BlockSpec((RET_HEADS, c, c), whole3),
                  pl.BlockSpec((RET_HEADS, c, 1), whole3),
                  pl.BlockSpec((RET_HEADS, c, 1), whole3)],
        out_specs=pl.BlockSpec((c, RET_V_WIDTH), row),
        out_shape=jax.ShapeDtypeStruct((batch * seq, RET_V_WIDTH), BF16),
        scratch_shapes=[pltpu.VMEM((RET_HEADS, RET_KEY_DIM, RET_VAL_DIM), F32)],
        compiler_params=_params("parallel", "arbitrary"),
        name="retention",
    )(chunk_decay, q, k, v, g, gain, decay, zeta, xi)


def _retention_consts():
    c = RET_CHUNK
    log_gamma = jnp.log1p(-jnp.exp2(-5.0 - jnp.arange(RET_HEADS, dtype=F32)))
    n = jnp.arange(c, dtype=F32)
    diff = n[:, None] - n[None, :]
    decay = jnp.where(diff >= 0, jnp.exp(log_gamma[:, None, None] * jnp.maximum(diff, 0.0)), 0.0)
    zeta = jnp.exp(log_gamma[:, None] * (c - 1 - n)[None, :])[:, :, None]
    xi = jnp.exp(log_gamma[:, None] * (n + 1.0)[None, :])[:, :, None]
    chunk_decay = jnp.exp(log_gamma * c)
    return chunk_decay, decay, zeta, xi


def _merge_kernel(x_ref, gt_ref, on_ref, or_ref, ga_ref, gb_ref, wpn_ref, wpr_ref, wo_ref, o_ref):
    j = pl.program_id(1)

    @pl.when(j == 0)
    def _():
        o_ref[...] = jnp.zeros_like(o_ref)

    a = _dot(on_ref[...], wpn_ref[...])
    b = _dot(or_ref[...], wpr_ref[...])
    merged = jax.nn.sigmoid(ga_ref[...]) * a + jax.nn.sigmoid(gb_ref[...]) * b
    o_ref[...] += _dot(merged.astype(BF16), wo_ref[...])

    @pl.when(j == pl.num_programs(1) - 1)
    def _():
        o_ref[...] = x_ref[...] + gt_ref[0] * o_ref[...]


def _merge(x2, gate, o_nsa, o_ret, ga, gb, wpn, wpr, wo, seq):
    t, d = x2.shape
    tm, tn = 1024, 256
    per_b = seq // tm
    once = pl.Buffered(1)
    return pl.pallas_call(
        _merge_kernel,
        grid=(t // tm, d // tn),
        in_specs=[pl.BlockSpec((tm, d), lambda i, j: (i, 0), pipeline_mode=once),
                  pl.BlockSpec((1, 1, d), lambda i, j: (i // per_b, 0, 0)),
                  pl.BlockSpec((tm, o_nsa.shape[1]), lambda i, j: (i, 0), pipeline_mode=once),
                  pl.BlockSpec((tm, o_ret.shape[1]), lambda i, j: (i, 0), pipeline_mode=once),
                  pl.BlockSpec((tm, tn), lambda i, j: (i, j)),
                  pl.BlockSpec((tm, tn), lambda i, j: (i, j)),
                  pl.BlockSpec((wpn.shape[0], tn), lambda i, j: (0, j)),
                  pl.BlockSpec((wpr.shape[0], tn), lambda i, j: (0, j)),
                  pl.BlockSpec((tn, d), lambda i, j: (j, 0))],
        out_specs=pl.BlockSpec((tm, d), lambda i, j: (i, 0)),
        out_shape=jax.ShapeDtypeStruct((t, d), F32),
        compiler_params=_params("parallel", "arbitrary"),
        name="merge",
    )(x2, gate, o_nsa, o_ret, ga, gb, wpn, wpr, wo)


def _nsa_tables(seq):
    n_cmp = (seq - CMP_BLOCK) // CMP_STRIDE + 1
    n_cmp_pad = seq // CMP_STRIDE
    n_slc = seq // SLC_BLOCK
    cs = (np.arange(n_cmp) * CMP_STRIDE)[:, None]
    js = (np.arange(n_slc) * SLC_BLOCK)[None, :]
    overlap = np.clip(np.minimum(cs + CMP_BLOCK, js + SLC_BLOCK) - np.maximum(cs, js), 0, None) / CMP_BLOCK
    ovl = np.zeros((n_slc, n_cmp_pad), np.float32)
    ovl[:, :n_cmp] = overlap.T
    expand = (np.arange(seq)[:, None] // SLC_BLOCK == np.arange(n_slc)[None, :]).astype(np.float32)
    slopes2 = jnp.exp2(-8.0 * jnp.arange(1, NSA_HEADS + 1, dtype=F32) / NSA_HEADS) * LOG2E
    return jnp.asarray(ovl, BF16), jnp.asarray(expand, BF16), slopes2


def kernel(x, c, w_ada, b_ada, g_norm, w_ffn_gate, w_ffn_up, w_ffn_down, w_in, g_qk, cmp_pos, cmp_w1, cmp_b1,
           cmp_w2, ret_gn_gain, w_proj_nsa, w_proj_ret, w_out):
    batch, seq, d = x.shape
    depth = w_ada.shape[0]
    t = batch * seq
    x2 = x.reshape(t, d)
    c_pad = jnp.pad(c, ((0, 8 - batch), (0, 0)))
    ovl, expand, slopes2 = _nsa_tables(seq)
    ret_consts = _retention_consts()
    hg = HEADS_PER_GROUP

    for l in range(depth):
        ada = _ada(c_pad, w_ada[l], b_ada[l][None, :])[:batch].reshape(batch, N_ADA, 1, d)
        sh1, sc1, gt1, sh2, sc2, gt2, sh3, sc3, gt3 = [ada[:, i] for i in range(N_ADA)]

        x2 = _ffn(x2, sh1, sc1, gt1, g_norm[l, 0][None, :], w_ffn_gate[l, 0].astype(BF16),
                  w_ffn_up[l, 0].astype(BF16), w_ffn_down[l, 0].astype(BF16), seq)

        u = _mod(x2, sh2, sc2, g_norm[l, 1][None, :], seq)
        w = w_in[l]
        off = [0]

        def take(width, transposed=False):
            w_part = w[:, off[0]:off[0] + width]
            off[0] += width
            return (w_part.T if transposed else w_part).astype(BF16)

        gq = g_qk[l]
        qt_nsa = _proj_t(u, take(NSA_WIDTH, True), BF16, gq[0], name="proj_q")
        cmp_raw = _proj(u, take(2 * KV_WIDTH), F32, name="proj_cmp_raw")
        k_slc = _proj(u, take(KV_WIDTH), BF16, "headnorm", gq[2][None, :], name="proj_k_slc")
        vt_slc = _proj_t(u, take(KV_WIDTH, True), BF16, name="proj_v_slc")
        k_win = _proj(u, take(KV_WIDTH), BF16, "headnorm", gq[3][None, :], name="proj_k_win")
        vt_win = _proj_t(u, take(KV_WIDTH, True), BF16, name="proj_v_win")
        w_gl = take(3 * NSA_HEADS).reshape(d, 3, NSA_KV_GROUPS, hg).transpose(0, 2, 1, 3)
        w_gl = jnp.pad(w_gl.reshape(d, NSA_KV_GROUPS, 3 * hg),
                       ((0, 0), (0, 0), (0, LANES - 3 * hg))).reshape(d, NSA_KV_GROUPS * LANES)
        gates = _proj(u, w_gl, F32, "sigmoid", name="proj_gates")
        gates_t = gates.reshape(t, NSA_KV_GROUPS, LANES)[:, :, :GATE_ROWS].transpose(1, 2, 0)
        q_r = _proj(u, take(RET_QK_WIDTH), F32, name="proj_q_ret")
        k_r = _proj(u, take(RET_QK_WIDTH), F32, "keyscale", name="proj_k_ret")
        v_r = _proj(u, take(RET_V_WIDTH), BF16, name="proj_v_ret")
        g_r = _proj(u, take(RET_V_WIDTH), F32, name="proj_g_ret")
        ga = _proj(u, take(d), F32, name="proj_ga")
        gb = _proj(u, take(d), F32, name="proj_gb")

        kv_cmp = _compress(cmp_raw.reshape(batch, seq, 2 * KV_WIDTH), cmp_pos[l], cmp_w1[l].astype(BF16),
                           cmp_b1[l][:, None, :], cmp_w2[l].astype(BF16), gq[1][None, :])
        o_nsa = _nsa(qt_nsa, kv_cmp, k_slc, vt_slc, k_win, vt_win, gates_t, slopes2, ovl, expand, batch, seq)
        o_ret = _retention(q_r, k_r, v_r, g_r, ret_gn_gain[l].reshape(1, RET_V_WIDTH), ret_consts, batch, seq)

        x2 = _merge(x2, gt2, o_nsa, o_ret, ga, gb, w_proj_nsa[l].astype(BF16), w_proj_ret[l].astype(BF16),
                    w_out[l].astype(BF16), seq)

        x2 = _ffn(x2, sh3, sc3, gt3, g_norm[l, 2][None, :], w_ffn_gate[l, 1].astype(BF16),
                  w_ffn_up[l, 1].astype(BF16), w_ffn_down[l, 1].astype(BF16), seq)

    return x2.reshape(batch, seq, d)
```

```python
import functools
import math

import numpy as np
import jax
import jax.numpy as jnp
from jax import lax
from jax.experimental import pallas as pl
from jax.experimental.pallas import tpu as pltpu

F32 = jnp.float32
BF16 = jnp.bfloat16

D_MODEL = 2048
NSA_HEADS = 16
NSA_KV_GROUPS = 4
HEADS_PER_GROUP = NSA_HEADS // NSA_KV_GROUPS
NSA_HEAD_DIM = 128
CMP_BLOCK = 32
CMP_STRIDE = 16
SLC_BLOCK = 64
SLC_TOP_N = 16
WINDOW = 512
RET_HEADS = 8
RET_KEY_DIM = 128
RET_VAL_DIM = 256
RET_CHUNK = 128
D_FF = 5632
N_ADA = 9
EPS = 1e-6
SEL_FORCE = 1e4

NSA_WIDTH = NSA_HEADS * NSA_HEAD_DIM
KV_WIDTH = NSA_KV_GROUPS * NSA_HEAD_DIM
RET_QK_WIDTH = RET_HEADS * RET_KEY_DIM
RET_V_WIDTH = RET_HEADS * RET_VAL_DIM

LANES = 128
VMEM_LIMIT_BYTES = 56 * 1024 * 1024
NEG_BIG = -1e30
F32_TINY = float(np.finfo(np.float32).tiny)
LOG2E = math.log2(math.e)

Q_BLOCK = 128
Q_CHAINS = 2
Q_STEP = Q_CHAINS * Q_BLOCK
KV_TILE = 512
GATE_ROWS = 16


def _params(*sem):
    return pltpu.CompilerParams(dimension_semantics=sem, vmem_limit_bytes=VMEM_LIMIT_BYTES)


def _silu(x):
    return x * jax.nn.sigmoid(x)


def _rms(x, g):
    return x * lax.rsqrt(jnp.mean(x * x, axis=-1, keepdims=True) + EPS) * g


def _modulate(x, g, shift, scale):
    return _rms(x, g) * (1.0 + scale) + shift


def _dot(a, b):
    return jnp.dot(a, b, preferred_element_type=F32)


def _dot_nt(a, b):
    return lax.dot_general(a, b, (((1,), (1,)), ((), ())), preferred_element_type=F32)


def _ada_kernel(c_ref, w_ref, b_ref, o_ref):
    cond = _silu(c_ref[...]).astype(BF16)
    o_ref[...] = _dot(cond, w_ref[...].astype(BF16)) + b_ref[...]


def _ada(c_pad, w, b):
    rows, d = c_pad.shape
    n = w.shape[1]
    tn = 1024
    return pl.pallas_call(
        _ada_kernel,
        grid=(n // tn,),
        in_specs=[pl.BlockSpec((rows, d), lambda j: (0, 0)),
                  pl.BlockSpec((d, tn), lambda j: (0, j)),
                  pl.BlockSpec((1, tn), lambda j: (0, j))],
        out_specs=pl.BlockSpec((rows, tn), lambda j: (0, j)),
        out_shape=jax.ShapeDtypeStruct((rows, n), F32),
        compiler_params=_params("parallel"),
        name="ada",
    )(c_pad, w, b)


def _ffn_kernel(x_ref, sh_ref, sc_ref, gt_ref, g_ref, wg_ref, wu_ref, wd_ref, o_ref, h_ref):
    j = pl.program_id(1)

    @pl.when(j == 0)
    def _():
        h_ref[...] = _modulate(x_ref[...], g_ref[...], sh_ref[0], sc_ref[0]).astype(BF16)
        o_ref[...] = jnp.zeros_like(o_ref)

    h = h_ref[...]
    a = _dot(h, wg_ref[...].astype(BF16))
    b = _dot(h, wu_ref[...].astype(BF16))
    act = (_silu(a) * b).astype(BF16)
    o_ref[...] += _dot(act, wd_ref[...].astype(BF16))

    @pl.when(j == pl.num_programs(1) - 1)
    def _():
        o_ref[...] = x_ref[...] + (0.5 * gt_ref[0]) * o_ref[...]


def _ffn(x2, shift, scale, gate, g, wg, wu, wd, seq):
    t, d = x2.shape
    ff = wg.shape[1]
    tm, tf = 1024, 256
    per_b = seq // tm
    mod_spec = pl.BlockSpec((1, 1, d), lambda i, j: (i // per_b, 0, 0))
    return pl.pallas_call(
        _ffn_kernel,
        grid=(t // tm, ff // tf),
        in_specs=[pl.BlockSpec((tm, d), lambda i, j: (i, 0), pipeline_mode=pl.Buffered(1)),
                  mod_spec, mod_spec, mod_spec,
                  pl.BlockSpec((1, d), lambda i, j: (0, 0)),
                  pl.BlockSpec((d, tf), lambda i, j: (0, j)),
                  pl.BlockSpec((d, tf), lambda i, j: (0, j)),
                  pl.BlockSpec((tf, d), lambda i, j: (j, 0))],
        out_specs=pl.BlockSpec((tm, d), lambda i, j: (i, 0)),
        out_shape=jax.ShapeDtypeStruct((t, d), F32),
        scratch_shapes=[pltpu.VMEM((tm, d), BF16)],
        compiler_params=_params("parallel", "arbitrary"),
        name="ffn",
    )(x2, shift, scale, gate, g, wg, wu, wd)


def _mod_kernel(x_ref, sh_ref, sc_ref, g_ref, o_ref):
    o_ref[...] = _modulate(x_ref[...], g_ref[...], sh_ref[0], sc_ref[0]).astype(BF16)


def _mod(x2, shift, scale, g, seq):
    t, d = x2.shape
    tm = 512
    per_b = seq // tm
    mod_spec = pl.BlockSpec((1, 1, d), lambda i: (i // per_b, 0, 0))
    return pl.pallas_call(
        _mod_kernel,
        grid=(t // tm,),
        in_specs=[pl.BlockSpec((tm, d), lambda i: (i, 0)), mod_spec, mod_spec,
                  pl.BlockSpec((1, d), lambda i: (0, 0))],
        out_specs=pl.BlockSpec((tm, d), lambda i: (i, 0)),
        out_shape=jax.ShapeDtypeStruct((t, d), BF16),
        compiler_params=_params("parallel"),
        name="modulate",
    )(x2, shift, scale, g)


def _proj_kernel(u_ref, w_ref, gain_ref, o_ref, *, epilogue):
    acc = _dot(u_ref[...], w_ref[...])
    if epilogue == "headnorm":
        g = gain_ref[...]
        parts = [_rms(acc[:, k:k + NSA_HEAD_DIM], g) for k in range(0, acc.shape[1], NSA_HEAD_DIM)]
        acc = jnp.concatenate(parts, axis=1)
    elif epilogue == "keyscale":
        acc = acc * (RET_KEY_DIM ** -0.5)
    elif epilogue == "sigmoid":
        acc = jax.nn.sigmoid(acc)
    o_ref[...] = acc.astype(o_ref.dtype)


def _proj(u, w, out_dtype, epilogue="plain", gain=None, name="proj"):
    t, d = u.shape
    n = w.shape[1]
    tm = 1024
    tn = 1024 if n % 1024 == 0 else 512
    if gain is None:
        gain = jnp.ones((1, NSA_HEAD_DIM), F32)
    return pl.pallas_call(
        functools.partial(_proj_kernel, epilogue=epilogue),
        grid=(t // tm, n // tn),
        in_specs=[pl.BlockSpec((tm, d), lambda i, j: (i, 0)),
                  pl.BlockSpec((d, tn), lambda i, j: (0, j)),
                  pl.BlockSpec((1, NSA_HEAD_DIM), lambda i, j: (0, 0))],
        out_specs=pl.BlockSpec((tm, tn), lambda i, j: (i, j)),
        out_shape=jax.ShapeDtypeStruct((t, n), out_dtype),
        compiler_params=_params("parallel", "parallel"),
        name=name,
    )(u, w, gain)


def _proj_t_kernel(u_ref, w_ref, gain_ref, o_ref, *, headnorm):
    acc = lax.dot_general(w_ref[...], u_ref[...], (((0,), (1,)), ((), ())), preferred_element_type=F32)
    if headnorm:
        g = gain_ref[...]
        parts = []
        for k in range(0, acc.shape[0], NSA_HEAD_DIM):
            xh = acc[k:k + NSA_HEAD_DIM]
            parts.append(xh * lax.rsqrt(jnp.mean(xh * xh, axis=0, keepdims=True) + EPS) * g)
        acc = jnp.concatenate(parts, axis=0)
    o_ref[...] = acc.astype(o_ref.dtype)


def _proj_t(u, w, out_dtype, gain=None, name="proj_t"):
    t, d = u.shape
    n = w.shape[1]
    tm = 1024
    tn = 1024 if n % 1024 == 0 else 512
    headnorm = gain is not None
    if gain is None:
        gain = jnp.ones((NSA_HEAD_DIM,), F32)
    return pl.pallas_call(
        functools.partial(_proj_t_kernel, headnorm=headnorm),
        grid=(t // tm, n // tn),
        in_specs=[pl.BlockSpec((tm, d), lambda i, j: (i, 0)),
                  pl.BlockSpec((d, tn), lambda i, j: (0, j)),
                  pl.BlockSpec((NSA_HEAD_DIM, 1), lambda i, j: (0, 0))],
        out_specs=pl.BlockSpec((tn, tm), lambda i, j: (j, i)),
        out_shape=jax.ShapeDtypeStruct((n, t), out_dtype),
        compiler_params=_params("parallel", "parallel"),
        name=name,
    )(u, w, gain[:, None])


def _cmp_kernel(z_ref, pos_ref, w1_ref, b1_ref, w2_ref, gk_ref, o_ref):
    half = CMP_BLOCK // 2
    n_rows = z_ref.shape[1] // CMP_STRIDE
    y_lo = jnp.zeros((n_rows, NSA_HEAD_DIM), F32)
    y_hi = jnp.zeros((n_rows, NSA_HEAD_DIM), F32)
    for l in range(half):
        zl = z_ref[0, pl.ds(l, n_rows, stride=CMP_STRIDE), :]
        lo = (zl + pos_ref[0, l:l + 1, :]).astype(BF16)
        hi = (zl + pos_ref[0, half + l:half + l + 1, :]).astype(BF16)
        y_lo += _dot(lo, w1_ref[0, l * NSA_HEAD_DIM:(l + 1) * NSA_HEAD_DIM, :])
        y_hi += _dot(hi, w1_ref[0, (half + l) * NSA_HEAD_DIM:(half + l + 1) * NSA_HEAD_DIM, :])
    y = y_lo + pltpu.roll(y_hi, n_rows - 1, 0)
    hdn = jax.nn.gelu(y + b1_ref[0]).astype(BF16)
    out = _dot(hdn, w2_ref[0])
    normed = _rms(out, gk_ref[...])
    is_key = pl.program_id(0) == 0
    o_ref[0, 0, 0] = jnp.where(is_key, normed, out.T).astype(o_ref.dtype)


def _compress(raw, pos, w1, b1, w2, gk):
    b, s, _ = raw.shape
    g = NSA_KV_GROUPS
    n_rows = s // CMP_STRIDE
    assert n_rows == NSA_HEAD_DIM, "key / transposed-value tiles share one square output block"
    return pl.pallas_call(
        _cmp_kernel,
        grid=(2, b, g),
        in_specs=[pl.BlockSpec((1, s, NSA_HEAD_DIM), lambda i, bb, gg: (bb, 0, i * g + gg)),
                  pl.BlockSpec((1, CMP_BLOCK, NSA_HEAD_DIM), lambda i, bb, gg: (i, 0, 0)),
                  pl.BlockSpec((1, CMP_BLOCK * NSA_HEAD_DIM, NSA_HEAD_DIM), lambda i, bb, gg: (i, 0, 0)),
                  pl.BlockSpec((1, 1, NSA_HEAD_DIM), lambda i, bb, gg: (i, 0, 0)),
                  pl.BlockSpec((1, NSA_HEAD_DIM, NSA_HEAD_DIM), lambda i, bb, gg: (i, 0, 0)),
                  pl.BlockSpec((1, NSA_HEAD_DIM), lambda i, bb, gg: (0, 0))],
        out_specs=pl.BlockSpec((1, 1, 1, n_rows, NSA_HEAD_DIM), lambda i, bb, gg: (i, bb, gg, 0, 0)),
        out_shape=jax.ShapeDtypeStruct((2, b, g, n_rows, NSA_HEAD_DIM), BF16),
        compiler_params=_params("parallel", "parallel", "parallel"),
        name="compress",
    )(raw, pos, w1, b1, w2, gk)


def _head(x, h):
    return x[:, h * Q_BLOCK:(h + 1) * Q_BLOCK]


def _round_up(x, m):
    return (x + m - 1) // m * m


def _pad_rows(x, n):
    if x.shape[0] == n:
        return x
    return jnp.concatenate([x, jnp.zeros((n - x.shape[0], x.shape[1]), x.dtype)], axis=0)


def _nsa_kernel(slope_ref, qt_ref, kc_ref, vct_ref, ks_ref, vst_ref, kw_ref, vwt_ref, gt_ref, ovl_ref, exp_ref,
                prev_ref, o_ref, *, step):
    del prev_ref
    hg, dh = HEADS_PER_GROUP, NSA_HEAD_DIM
    chains = range(Q_CHAINS)
    grp = pl.program_id(1)
    t0 = [step * Q_STEP + c * Q_BLOCK for c in chains]
    t_end = (step + 1) * Q_STEP
    c1 = (dh ** -0.5) * LOG2E
    slope2 = [slope_ref[grp * hg + h] for h in range(hg)]

    qt = [jnp.concatenate([qt_ref[h * dh:(h + 1) * dh, c * Q_BLOCK:(c + 1) * Q_BLOCK] for h in range(hg)], axis=1)
          for c in chains]
    t_row = [t0[c] + lax.broadcasted_iota(jnp.int32, (1, Q_BLOCK), 1) for c in chains]

    n_cmp_pad = kc_ref.shape[3]
    n_slc = ovl_ref.shape[0]
    n_cmp = min(n_cmp_pad, _round_up((t_end - CMP_BLOCK) // CMP_STRIDE + 1, 16))
    n_cand = min(n_slc, t_end // SLC_BLOCK)
    n_blk = min(n_slc, _round_up(n_cand, 8))
    cstart = lax.broadcasted_iota(jnp.int32, (n_cmp, Q_BLOCK), 0) * CMP_STRIDE
    centre = cstart.astype(F32) + (CMP_BLOCK - 1) / 2
    jj = lax.broadcasted_iota(jnp.int32, (n_blk, Q_BLOCK), 0)
    kc, vct, ovl = kc_ref[0, 0, 0, 0:n_cmp, :], vct_ref[0, 0, 0], ovl_ref[0:n_blk, :]
    o_cmp, member = [], []
    for c in chains:
        centre_dist = t_row[c].astype(F32) - centre
        valid = (cstart + (CMP_BLOCK - 1)) <= t_row[c]
        sc = _dot(kc, qt[c])
        p_heads = []
        for h in range(hg):
            s = _head(sc, h) * c1 - slope2[h] * centre_dist
            s = jnp.where(valid, s, -jnp.inf)
            m = jnp.max(s, axis=0, keepdims=True)
            m = jnp.where(jnp.isfinite(m), m, 0.0)
            e = jnp.where(valid, jnp.exp2(s - m), 0.0)
            p = e / jnp.maximum(jnp.sum(e, axis=0, keepdims=True), F32_TINY)
            p_heads.append(_pad_rows(p.astype(BF16), n_cmp_pad))
        o_cmp.append(_dot(vct, jnp.concatenate(p_heads, axis=1)))

        imp = _dot(ovl, p_heads[0])
        for h in range(1, hg):
            imp += _dot(ovl, p_heads[h])
        blk_t = t_row[c] // SLC_BLOCK
        forced = (jj == 0) | (jj == blk_t) | (jj == blk_t - 1)
        imp = jnp.where(forced, SEL_FORCE, jnp.where(jj <= blk_t, imp, -SEL_FORCE))
        rank = jnp.zeros((n_blk, Q_BLOCK), jnp.int32)
        for i in range(n_cand):
            ri = imp[i:i + 1, :]
            before = (ri > imp) | ((ri == imp) & (jj > i))
            rank += before.astype(jnp.int32)
        member.append(_pad_rows(jnp.where(rank < min(SLC_TOP_N, n_slc), 1.0, 0.0).astype(BF16), n_slc))

    key_iota = {}

    def key_local(n):
        if n not in key_iota:
            key_iota[n] = lax.broadcasted_iota(jnp.int32, (n, Q_BLOCK), 0)
        return key_iota[n]

    def tiles(lo, hi):
        return [(p, min(KV_TILE, hi - p)) for p in range(lo, hi, KV_TILE)]

    streams = []
    for c in chains:
        streams.append([("slc", c, ks_ref, vst_ref, lo, n) for lo, n in tiles(0, t0[c] + Q_BLOCK)])
    for c in chains:
        streams.append([("win", c, kw_ref, vwt_ref, lo, n)
                        for lo, n in tiles(max(t0[c] - WINDOW, 0), t0[c] + Q_BLOCK)])
    items = [s[i] for i in range(max(len(s) for s in streams)) for s in streams if i < len(s)]

    def scores(item):
        _, c, k_ref, _, lo, n = item
        return _dot(k_ref[lo:lo + n, :], qt[c])

    def finish(state, item, sc):
        branch, c, _, vt_ref, lo, n = item
        dist = t_row[c] - (lo + key_local(n))
        masks = []
        if branch == "slc":
            masks.append(_dot(exp_ref[lo:lo + n, :], member[c]) > 0.5)
        if lo + n - 1 > t0[c]:
            masks.append(dist >= 0)
        if branch == "win" and t0[c] + Q_BLOCK - 1 - lo >= WINDOW:
            masks.append(dist < WINDOW)
        mask = functools.reduce(jnp.logical_and, masks) if masks else None
        dist_f = dist.astype(F32)
        m_new, l_new, alpha, p_t = [], [], [], []
        for h in range(hg):
            s = _head(sc, h) * c1 - slope2[h] * dist_f
            if mask is not None:
                s = jnp.where(mask, s, NEG_BIG)
            m_n = jnp.max(s, axis=0, keepdims=True)
            if state is not None:
                m_o = _head(state[0], h)
                m_n = jnp.maximum(m_o, m_n)
                a = jnp.exp2(m_o - m_n)
                alpha.append(a)
            p = jnp.exp2(s - m_n)
            l_n = jnp.sum(p, axis=0, keepdims=True)
            if state is not None:
                l_n = a * _head(state[1], h) + l_n
            m_new.append(m_n)
            l_new.append(l_n)
            p_t.append(p.astype(BF16))
        acc = _dot(vt_ref[:, lo:lo + n], jnp.concatenate(p_t, axis=1))
        if state is not None:
            acc = jnp.concatenate(alpha, axis=1) * state[2] + acc
        return jnp.concatenate(m_new, axis=1), jnp.concatenate(l_new, axis=1), acc

    states = {}
    sc_next = scores(items[0])
    for i, item in enumerate(items):
        sc = sc_next
        if i + 1 < len(items):
            sc_next = scores(items[i + 1])
        states[item[:2]] = finish(states.get(item[:2]), item, sc)

    for c in chains:
        gt = gt_ref[0, :, c * Q_BLOCK:(c + 1) * Q_BLOCK]

        def gate(br):
            return jnp.concatenate([gt[br * hg + h:br * hg + h + 1, :] for h in range(hg)], axis=1)

        (_, l_s, acc_s), (_, l_w, acc_w) = states[("slc", c)], states[("win", c)]
        o = (gate(0) * o_cmp[c] + (gate(1) / jnp.maximum(l_s, F32_TINY)) * acc_s
             + (gate(2) / jnp.maximum(l_w, F32_TINY)) * acc_w)
        o_ref[c * Q_BLOCK:(c + 1) * Q_BLOCK, :] = jnp.concatenate(
            [_head(o, h).T for h in range(hg)], axis=1).astype(o_ref.dtype)


def _nsa(qt, kv_cmp, k_slc, vt_slc, k_win, vt_win, gates_t, slopes2, ovl, expand, batch, seq):
    g = NSA_KV_GROUPS
    gw = HEADS_PER_GROUP * NSA_HEAD_DIM
    nq = seq // Q_STEP
    n_cmp_pad = kv_cmp.shape[3]
    out = jnp.zeros((batch * seq, NSA_WIDTH), BF16)
    for step in range(nq):
        per_seq = 1
        while seq // (2 * per_seq) >= (step + 1) * Q_STEP:
            per_seq *= 2
        n_keys = seq // per_seq
        k_spec = pl.BlockSpec((n_keys, NSA_HEAD_DIM), lambda b, gg, per_seq=per_seq: (b * per_seq, gg))
        vt_spec = pl.BlockSpec((NSA_HEAD_DIM, n_keys), lambda b, gg, per_seq=per_seq: (gg, b * per_seq))
        row_blk = lambda b, gg, step=step: (b * nq + step, gg)
        out = pl.pallas_call(
            functools.partial(_nsa_kernel, step=step),
            grid=(batch, g),
            in_specs=[pl.BlockSpec(memory_space=pltpu.SMEM),
                      pl.BlockSpec((gw, Q_STEP), lambda b, gg, step=step: (gg, b * nq + step)),
                      pl.BlockSpec((1, 1, 1, n_cmp_pad, NSA_HEAD_DIM), lambda b, gg: (0, b, gg, 0, 0)),
                      pl.BlockSpec((1, 1, 1, NSA_HEAD_DIM, n_cmp_pad), lambda b, gg: (1, b, gg, 0, 0)),
                      k_spec, vt_spec, k_spec, vt_spec,
                      pl.BlockSpec((1, GATE_ROWS, Q_STEP), lambda b, gg, step=step: (gg, 0, b * nq + step)),
                      pl.BlockSpec(ovl.shape, lambda b, gg: (0, 0)),
                      pl.BlockSpec(expand.shape, lambda b, gg: (0, 0)),
                      pl.BlockSpec(memory_space=pl.ANY)],
            out_specs=pl.BlockSpec((Q_STEP, gw), row_blk),
            out_shape=jax.ShapeDtypeStruct((batch * seq, NSA_WIDTH), BF16),
            input_output_aliases={11: 0},
            compiler_params=_params("parallel", "parallel"),
            name=f"nsa{step}",
        )(slopes2, qt, kv_cmp, kv_cmp, k_slc, vt_slc, k_win, vt_win, gates_t, ovl, expand, out)
    return out


def _ret_kernel(cd_ref, q_ref, k_ref, v_ref, g_ref, gain_ref, decay_ref, zeta_ref, xi_ref, o_ref, state_ref):
    @pl.when(pl.program_id(1) == 0)
    def _():
        state_ref[...] = jnp.zeros_like(state_ref)

    dk, dv = RET_KEY_DIM, RET_VAL_DIM
    for h in range(RET_HEADS):
        q = q_ref[:, h * dk:(h + 1) * dk]
        k = k_ref[:, h * dk:(h + 1) * dk]
        v = v_ref[:, h * dv:(h + 1) * dv]
        sc = _dot_nt(q.astype(BF16), k.astype(BF16)) * decay_ref[h]
        inner = _dot(sc.astype(BF16), v)
        st = state_ref[h]
        cross = _dot((q * xi_ref[h]).astype(BF16), st.astype(BF16))
        kz_t = (k * zeta_ref[h]).T.astype(BF16)
        state_ref[h] = st * cd_ref[h] + _dot(kz_t, v)
        y = inner + cross
        yc = y - jnp.mean(y, axis=-1, keepdims=True)
        yn = yc * lax.rsqrt(jnp.mean(yc * yc, axis=-1, keepdims=True) + EPS) * gain_ref[:, h * dv:(h + 1) * dv]
        o_ref[:, h * dv:(h + 1) * dv] = (_silu(g_ref[:, h * dv:(h + 1) * dv]) * yn).astype(o_ref.dtype)


def _retention(q, k, v, g, gain, consts, batch, seq):
    chunk_decay, decay, zeta, xi = consts
    c = RET_CHUNK
    nc = seq // c
    row = lambda b, ci: (b * nc + ci, 0)
    whole3 = lambda b, ci: (0, 0, 0)
    return pl.pallas_call(
        _ret_kernel,
        grid=(batch, nc),
        in_specs=[pl.BlockSpec(memory_space=pltpu.SMEM),
                  pl.BlockSpec((c, RET_QK_WIDTH), row),
                  pl.BlockSpec((c, RET_QK_WIDTH), row),
                  pl.BlockSpec((c, RET_V_WIDTH), row),
                  pl.BlockSpec((c, RET_V_WIDTH), row),
                  pl.BlockSpec((1, RET_V_WIDTH), lambda b, ci: (0, 0)),
                  pl.BlockSpec((RET_HEADS, c, c), whole3),
                  pl.BlockSpec((RET_HEADS, c, 1), whole3),
                  pl.BlockSpec((RET_HEADS, c, 1), whole3)],
        out_specs=pl.BlockSpec((c, RET_V_WIDTH), row),
        out_shape=jax.ShapeDtypeStruct((batch * seq, RET_V_WIDTH), BF16),
        scratch_shapes=[pltpu.VMEM((RET_HEADS, RET_KEY_DIM, RET_VAL_DIM), F32)],
        compiler_params=_params("parallel", "arbitrary"),
        name="retention",
    )(chunk_decay, q, k, v, g, gain, decay, zeta, xi)


def _retention_consts():
    c = RET_CHUNK
    log_gamma = jnp.log1p(-jnp.exp2(-5.0 - jnp.arange(RET_HEADS, dtype=F32)))
    n = jnp.arange(c, dtype=F32)
    diff = n[:, None] - n[None, :]
    decay = jnp.where(diff >= 0, jnp.exp(log_gamma[:, None, None] * jnp.maximum(diff, 0.0)), 0.0)
    zeta = jnp.exp(log_gamma[:, None] * (c - 1 - n)[None, :])[:, :, None]
    xi = jnp.exp(log_gamma[:, None] * (n + 1.0)[None, :])[:, :, None]
    chunk_decay = jnp.exp(log_gamma * c)
    return chunk_decay, decay, zeta, xi


def _merge_kernel(x_ref, gt_ref, on_ref, or_ref, ga_ref, gb_ref, wpn_ref, wpr_ref, wo_ref, o_ref, acc_ref):
    j = pl.program_id(1)

    @pl.when(j == 0)
    def _():
        acc_ref[...] = jnp.zeros_like(acc_ref)

    a = _dot(on_ref[...], wpn_ref[...])
    b = _dot(or_ref[...], wpr_ref[...])
    merged = jax.nn.sigmoid(ga_ref[...]) * a + jax.nn.sigmoid(gb_ref[...]) * b
    acc_ref[...] += _dot(merged.astype(BF16), wo_ref[...])

    @pl.when(j == pl.num_programs(1) - 1)
    def _():
        o_ref[...] = x_ref[...] + gt_ref[0] * acc_ref[...]


def _merge(x2, gate, o_nsa, o_ret, ga, gb, wpn, wpr, wo, seq):
    t, d = x2.shape
    tm, tn = 512, 512
    per_b = seq // tm
    return pl.pallas_call(
        _merge_kernel,
        grid=(t // tm, d // tn),
        in_specs=[pl.BlockSpec((tm, d), lambda i, j: (i, 0)),
                  pl.BlockSpec((1, 1, d), lambda i, j: (i // per_b, 0, 0)),
                  pl.BlockSpec((tm, o_nsa.shape[1]), lambda i, j: (i, 0)),
                  pl.BlockSpec((tm, o_ret.shape[1]), lambda i, j: (i, 0)),
                  pl.BlockSpec((tm, tn), lambda i, j: (i, j)),
                  pl.BlockSpec((tm, tn), lambda i, j: (i, j)),
                  pl.BlockSpec((wpn.shape[0], tn), lambda i, j: (0, j)),
                  pl.BlockSpec((wpr.shape[0], tn), lambda i, j: (0, j)),
                  pl.BlockSpec((tn, d), lambda i, j: (j, 0))],
        out_specs=pl.BlockSpec((tm, d), lambda i, j: (i, 0)),
        out_shape=jax.ShapeDtypeStruct((t, d), F32),
        scratch_shapes=[pltpu.VMEM((tm, d), F32)],
        compiler_params=_params("parallel", "arbitrary"),
        name="merge",
    )(x2, gate, o_nsa, o_ret, ga, gb, wpn, wpr, wo)


def _nsa_tables(seq):
    n_cmp = (seq - CMP_BLOCK) // CMP_STRIDE + 1
    n_cmp_pad = seq // CMP_STRIDE
    n_slc = seq // SLC_BLOCK
    cs = (np.arange(n_cmp) * CMP_STRIDE)[:, None]
    js = (np.arange(n_slc) * SLC_BLOCK)[None, :]
    overlap = np.clip(np.minimum(cs + CMP_BLOCK, js + SLC_BLOCK) - np.maximum(cs, js), 0, None) / CMP_BLOCK
    ovl = np.zeros((n_slc, n_cmp_pad), np.float32)
    ovl[:, :n_cmp] = overlap.T
    expand = (np.arange(seq)[:, None] // SLC_BLOCK == np.arange(n_slc)[None, :]).astype(np.float32)
    slopes2 = jnp.exp2(-8.0 * jnp.arange(1, NSA_HEADS + 1, dtype=F32) / NSA_HEADS) * LOG2E
    return jnp.asarray(ovl, BF16), jnp.asarray(expand, BF16), slopes2


def kernel(x, c, w_ada, b_ada, g_norm, w_ffn_gate, w_ffn_up, w_ffn_down, w_in, g_qk, cmp_pos, cmp_w1, cmp_b1,
           cmp_w2, ret_gn_gain, w_proj_nsa, w_proj_ret, w_out):
    batch, seq, d = x.shape
    depth = w_ada.shape[0]
    t = batch * seq
    x2 = x.reshape(t, d)
    c_pad = jnp.pad(c, ((0, 8 - batch), (0, 0)))
    ovl, expand, slopes2 = _nsa_tables(seq)
    ret_consts = _retention_consts()
    hg = HEADS_PER_GROUP

    for l in range(depth):
        ada = _ada(c_pad, w_ada[l], b_ada[l][None, :])[:batch].reshape(batch, N_ADA, 1, d)
        sh1, sc1, gt1, sh2, sc2, gt2, sh3, sc3, gt3 = [ada[:, i] for i in range(N_ADA)]

        x2 = _ffn(x2, sh1, sc1, gt1, g_norm[l, 0][None, :], w_ffn_gate[l, 0], w_ffn_up[l, 0], w_ffn_down[l, 0], seq)

        u = _mod(x2, sh2, sc2, g_norm[l, 1][None, :], seq)
        w = w_in[l]
        off = [0]

        def take(width):
            w_part = w[:, off[0]:off[0] + width]
            off[0] += width
            return w_part.astype(BF16)

        gq = g_qk[l]
        qt_nsa = _proj_t(u, take(NSA_WIDTH), BF16, gq[0], name="proj_q")
        cmp_raw = _proj(u, take(2 * KV_WIDTH), F32, name="proj_cmp_raw")
        k_slc = _proj(u, take(KV_WIDTH), BF16, "headnorm", gq[2][None, :], name="proj_k_slc")
        vt_slc = _proj_t(u, take(KV_WIDTH), BF16, name="proj_v_slc")
        k_win = _proj(u, take(KV_WIDTH), BF16, "headnorm", gq[3][None, :], name="proj_k_win")
        vt_win = _proj_t(u, take(KV_WIDTH), BF16, name="proj_v_win")
        w_gl = take(3 * NSA_HEADS).reshape(d, 3, NSA_KV_GROUPS, hg).transpose(0, 2, 1, 3)
        w_gl = jnp.pad(w_gl.reshape(d, NSA_KV_GROUPS, 3 * hg),
                       ((0, 0), (0, 0), (0, LANES - 3 * hg))).reshape(d, NSA_KV_GROUPS * LANES)
        gates = _proj(u, w_gl, F32, "sigmoid", name="proj_gates")
        gates_t = gates.reshape(t, NSA_KV_GROUPS, LANES)[:, :, :GATE_ROWS].transpose(1, 2, 0)
        q_r = _proj(u, take(RET_QK_WIDTH), F32, name="proj_q_ret")
        k_r = _proj(u, take(RET_QK_WIDTH), F32, "keyscale", name="proj_k_ret")
        v_r = _proj(u, take(RET_V_WIDTH), BF16, name="proj_v_ret")
        g_r = _proj(u, take(RET_V_WIDTH), F32, name="proj_g_ret")
        ga = _proj(u, take(d), F32, name="proj_ga")
        gb = _proj(u, take(d), F32, name="proj_gb")

        kv_cmp = _compress(cmp_raw.reshape(batch, seq, 2 * KV_WIDTH), cmp_pos[l], cmp_w1[l].astype(BF16),
                           cmp_b1[l][:, None, :], cmp_w2[l].astype(BF16), gq[1][None, :])
        o_nsa = _nsa(qt_nsa, kv_cmp, k_slc, vt_slc, k_win, vt_win, gates_t, slopes2, ovl, expand, batch, seq)
        o_ret = _retention(q_r, k_r, v_r, g_r, ret_gn_gain[l].reshape(1, RET_V_WIDTH), ret_consts, batch, seq)

        x2 = _merge(x2, gt2, o_nsa, o_ret, ga, gb, w_proj_nsa[l].astype(BF16), w_proj_ret[l].astype(BF16),
                    w_out[l].astype(BF16), seq)

        x2 = _ffn(x2, sh3, sc3, gt3, g_norm[l, 2][None, :], w_ffn_gate[l, 1], w_ffn_up[l, 1], w_ffn_down[l, 1], seq)

    return x2.reshape(batch, seq, d)
```

```python
import functools
import math

import numpy as np
import jax
import jax.numpy as jnp
from jax import lax
from jax.experimental import pallas as pl
from jax.experimental.pallas import tpu as pltpu

F32 = jnp.float32
BF16 = jnp.bfloat16

D_MODEL = 2048
NSA_HEADS = 16
NSA_KV_GROUPS = 4
HEADS_PER_GROUP = NSA_HEADS // NSA_KV_GROUPS
NSA_HEAD_DIM = 128
CMP_BLOCK = 32
CMP_STRIDE = 16
SLC_BLOCK = 64
SLC_TOP_N = 16
WINDOW = 512
RET_HEADS = 8
RET_KEY_DIM = 128
RET_VAL_DIM = 256
RET_CHUNK = 128
D_FF = 5632
N_ADA = 9
EPS = 1e-6
SEL_FORCE = 1e4

NSA_WIDTH = NSA_HEADS * NSA_HEAD_DIM
KV_WIDTH = NSA_KV_GROUPS * NSA_HEAD_DIM
RET_QK_WIDTH = RET_HEADS * RET_KEY_DIM
RET_V_WIDTH = RET_HEADS * RET_VAL_DIM

LANES = 128
VMEM_LIMIT_BYTES = 56 * 1024 * 1024
NEG_BIG = -1e30
F32_TINY = float(np.finfo(np.float32).tiny)
LOG2E = math.log2(math.e)

Q_BLOCK = 128
Q_CHAINS = 2
Q_STEP = Q_CHAINS * Q_BLOCK
KV_TILE = 512
GATE_ROWS = 16


def _params(*sem):
    return pltpu.CompilerParams(dimension_semantics=sem, vmem_limit_bytes=VMEM_LIMIT_BYTES)


def _silu(x):
    return x * jax.nn.sigmoid(x)


def _rms(x, g):
    return x * lax.rsqrt(jnp.mean(x * x, axis=-1, keepdims=True) + EPS) * g


def _modulate(x, g, shift, scale):
    return _rms(x, g) * (1.0 + scale) + shift


def _dot(a, b):
    return jnp.dot(a, b, preferred_element_type=F32)


def _dot_nt(a, b):
    return lax.dot_general(a, b, (((1,), (1,)), ((), ())), preferred_element_type=F32)


def _ada_kernel(c_ref, w_ref, b_ref, o_ref):
    cond = _silu(c_ref[...]).astype(BF16)
    o_ref[...] = _dot(cond, w_ref[...].astype(BF16)) + b_ref[...]


def _ada(c_pad, w, b):
    rows, d = c_pad.shape
    n = w.shape[1]
    tn = 1024
    return pl.pallas_call(
        _ada_kernel,
        grid=(n // tn,),
        in_specs=[pl.BlockSpec((rows, d), lambda j: (0, 0)),
                  pl.BlockSpec((d, tn), lambda j: (0, j)),
                  pl.BlockSpec((1, tn), lambda j: (0, j))],
        out_specs=pl.BlockSpec((rows, tn), lambda j: (0, j)),
        out_shape=jax.ShapeDtypeStruct((rows, n), F32),
        compiler_params=_params("parallel"),
        name="ada",
    )(c_pad, w, b)


def _ffn_kernel(x_ref, sh_ref, sc_ref, gt_ref, g_ref, wg_ref, wu_ref, wd_ref, o_ref, h_ref):
    j = pl.program_id(1)

    @pl.when(j == 0)
    def _():
        h_ref[...] = _modulate(x_ref[...], g_ref[...], sh_ref[0], sc_ref[0]).astype(BF16)
        o_ref[...] = jnp.zeros_like(o_ref)

    h = h_ref[...]
    a = _dot(h, wg_ref[...].astype(BF16))
    b = _dot(h, wu_ref[...].astype(BF16))
    act = (_silu(a) * b).astype(BF16)
    o_ref[...] += _dot(act, wd_ref[...].astype(BF16))

    @pl.when(j == pl.num_programs(1) - 1)
    def _():
        o_ref[...] = x_ref[...] + (0.5 * gt_ref[0]) * o_ref[...]


def _ffn(x2, shift, scale, gate, g, wg, wu, wd, which, seq):
    t, d = x2.shape
    ff = wg.shape[2]
    tm, tf = 1024, 256
    per_b = seq // tm
    mod_spec = pl.BlockSpec((1, 1, d), lambda i, j: (i // per_b, 0, 0))
    return pl.pallas_call(
        _ffn_kernel,
        grid=(t // tm, ff // tf),
        in_specs=[pl.BlockSpec((tm, d), lambda i, j: (i, 0), pipeline_mode=pl.Buffered(1)),
                  mod_spec, mod_spec, mod_spec,
                  pl.BlockSpec((1, d), lambda i, j: (0, 0)),
                  pl.BlockSpec((None, d, tf), lambda i, j: (which, 0, j)),
                  pl.BlockSpec((None, d, tf), lambda i, j: (which, 0, j)),
                  pl.BlockSpec((None, tf, d), lambda i, j: (which, j, 0))],
        out_specs=pl.BlockSpec((tm, d), lambda i, j: (i, 0)),
        out_shape=jax.ShapeDtypeStruct((t, d), F32),
        scratch_shapes=[pltpu.VMEM((tm, d), BF16)],
        compiler_params=_params("parallel", "arbitrary"),
        name="ffn",
    )(x2, shift, scale, gate, g, wg, wu, wd)


def _mod_kernel(x_ref, sh_ref, sc_ref, g_ref, o_ref):
    o_ref[...] = _modulate(x_ref[...], g_ref[...], sh_ref[0], sc_ref[0]).astype(BF16)


def _mod(x2, shift, scale, g, seq):
    t, d = x2.shape
    tm = 512
    per_b = seq // tm
    mod_spec = pl.BlockSpec((1, 1, d), lambda i: (i // per_b, 0, 0))
    return pl.pallas_call(
        _mod_kernel,
        grid=(t // tm,),
        in_specs=[pl.BlockSpec((tm, d), lambda i: (i, 0)), mod_spec, mod_spec,
                  pl.BlockSpec((1, d), lambda i: (0, 0))],
        out_specs=pl.BlockSpec((tm, d), lambda i: (i, 0)),
        out_shape=jax.ShapeDtypeStruct((t, d), BF16),
        compiler_params=_params("parallel"),
        name="modulate",
    )(x2, shift, scale, g)


def _proj_kernel(u_ref, w_ref, gain_ref, o_ref, *, epilogue):
    acc = _dot(u_ref[...], w_ref[...])
    if epilogue == "headnorm":
        g = gain_ref[...]
        parts = [_rms(acc[:, k:k + NSA_HEAD_DIM], g) for k in range(0, acc.shape[1], NSA_HEAD_DIM)]
        acc = jnp.concatenate(parts, axis=1)
    elif epilogue == "keyscale":
        acc = acc * (RET_KEY_DIM ** -0.5)
    elif epilogue == "sigmoid":
        acc = jax.nn.sigmoid(acc)
    o_ref[...] = acc.astype(o_ref.dtype)


def _proj(u, w, out_dtype, epilogue="plain", gain=None, name="proj"):
    t, d = u.shape
    n = w.shape[1]
    tm = 1024
    tn = 1024 if n % 1024 == 0 else 512
    if gain is None:
        gain = jnp.ones((1, NSA_HEAD_DIM), F32)
    return pl.pallas_call(
        functools.partial(_proj_kernel, epilogue=epilogue),
        grid=(t // tm, n // tn),
        in_specs=[pl.BlockSpec((tm, d), lambda i, j: (i, 0)),
                  pl.BlockSpec((d, tn), lambda i, j: (0, j)),
                  pl.BlockSpec((1, NSA_HEAD_DIM), lambda i, j: (0, 0))],
        out_specs=pl.BlockSpec((tm, tn), lambda i, j: (i, j)),
        out_shape=jax.ShapeDtypeStruct((t, n), out_dtype),
        compiler_params=_params("parallel", "parallel"),
        name=name,
    )(u, w, gain)


def _proj_t_kernel(u_ref, w_ref, gain_ref, o_ref, *, headnorm):
    acc = lax.dot_general(w_ref[...], u_ref[...], (((0,), (1,)), ((), ())), preferred_element_type=F32)
    if headnorm:
        g = gain_ref[...]
        parts = []
        for k in range(0, acc.shape[0], NSA_HEAD_DIM):
            xh = acc[k:k + NSA_HEAD_DIM]
            parts.append(xh * lax.rsqrt(jnp.mean(xh * xh, axis=0, keepdims=True) + EPS) * g)
        acc = jnp.concatenate(parts, axis=0)
    o_ref[...] = acc.astype(o_ref.dtype)


def _proj_t(u, w, out_dtype, gain=None, name="proj_t"):
    t, d = u.shape
    n = w.shape[1]
    tm = 1024
    tn = 1024 if n % 1024 == 0 else 512
    headnorm = gain is not None
    if gain is None:
        gain = jnp.ones((NSA_HEAD_DIM,), F32)
    return pl.pallas_call(
        functools.partial(_proj_t_kernel, headnorm=headnorm),
        grid=(t // tm, n // tn),
        in_specs=[pl.BlockSpec((tm, d), lambda i, j: (i, 0)),
                  pl.BlockSpec((d, tn), lambda i, j: (0, j)),
                  pl.BlockSpec((NSA_HEAD_DIM, 1), lambda i, j: (0, 0))],
        out_specs=pl.BlockSpec((tn, tm), lambda i, j: (j, i)),
        out_shape=jax.ShapeDtypeStruct((n, t), out_dtype),
        compiler_params=_params("parallel", "parallel"),
        name=name,
    )(u, w, gain[:, None])


def _cmp_kernel(z_ref, pos_ref, w1_ref, b1_ref, w2_ref, gk_ref, o_ref):
    half = CMP_BLOCK // 2
    n_rows = z_ref.shape[1] // CMP_STRIDE
    y_lo = jnp.zeros((n_rows, NSA_HEAD_DIM), F32)
    y_hi = jnp.zeros((n_rows, NSA_HEAD_DIM), F32)
    for l in range(half):
        zl = z_ref[0, pl.ds(l, n_rows, stride=CMP_STRIDE), :]
        lo = (zl + pos_ref[0, l:l + 1, :]).astype(BF16)
        hi = (zl + pos_ref[0, half + l:half + l + 1, :]).astype(BF16)
        y_lo += _dot(lo, w1_ref[0, l * NSA_HEAD_DIM:(l + 1) * NSA_HEAD_DIM, :])
        y_hi += _dot(hi, w1_ref[0, (half + l) * NSA_HEAD_DIM:(half + l + 1) * NSA_HEAD_DIM, :])
    y = y_lo + pltpu.roll(y_hi, n_rows - 1, 0)
    hdn = jax.nn.gelu(y + b1_ref[0]).astype(BF16)
    out = _dot(hdn, w2_ref[0])
    normed = _rms(out, gk_ref[...])
    is_key = pl.program_id(0) == 0
    o_ref[0, 0, 0] = jnp.where(is_key, normed, out.T).astype(o_ref.dtype)


def _compress(raw, pos, w1, b1, w2, gk):
    b, s, _ = raw.shape
    g = NSA_KV_GROUPS
    n_rows = s // CMP_STRIDE
    assert n_rows == NSA_HEAD_DIM, "key / transposed-value tiles share one square output block"
    return pl.pallas_call(
        _cmp_kernel,
        grid=(2, b, g),
        in_specs=[pl.BlockSpec((1, s, NSA_HEAD_DIM), lambda i, bb, gg: (bb, 0, i * g + gg)),
                  pl.BlockSpec((1, CMP_BLOCK, NSA_HEAD_DIM), lambda i, bb, gg: (i, 0, 0)),
                  pl.BlockSpec((1, CMP_BLOCK * NSA_HEAD_DIM, NSA_HEAD_DIM), lambda i, bb, gg: (i, 0, 0)),
                  pl.BlockSpec((1, 1, NSA_HEAD_DIM), lambda i, bb, gg: (i, 0, 0)),
                  pl.BlockSpec((1, NSA_HEAD_DIM, NSA_HEAD_DIM), lambda i, bb, gg: (i, 0, 0)),
                  pl.BlockSpec((1, NSA_HEAD_DIM), lambda i, bb, gg: (0, 0))],
        out_specs=pl.BlockSpec((1, 1, 1, n_rows, NSA_HEAD_DIM), lambda i, bb, gg: (i, bb, gg, 0, 0)),
        out_shape=jax.ShapeDtypeStruct((2, b, g, n_rows, NSA_HEAD_DIM), BF16),
        compiler_params=_params("parallel", "parallel", "parallel"),
        name="compress",
    )(raw, pos, w1, b1, w2, gk)


def _head(x, h):
    return x[:, h * Q_BLOCK:(h + 1) * Q_BLOCK]


def _round_up(x, m):
    return (x + m - 1) // m * m


def _pad_rows(x, n):
    if x.shape[0] == n:
        return x
    return jnp.concatenate([x, jnp.zeros((n - x.shape[0], x.shape[1]), x.dtype)], axis=0)


def _nsa_kernel(slope_ref, qt_ref, kc_ref, vct_ref, ks_ref, vst_ref, kw_ref, vwt_ref, gt_ref, ovl_ref, exp_ref,
                prev_ref, o_ref, *, step):
    del prev_ref
    hg, dh = HEADS_PER_GROUP, NSA_HEAD_DIM
    chains = range(Q_CHAINS)
    grp = pl.program_id(1)
    t0 = [step * Q_STEP + c * Q_BLOCK for c in chains]
    t_end = (step + 1) * Q_STEP
    c1 = (dh ** -0.5) * LOG2E
    slope2 = [slope_ref[grp * hg + h] for h in range(hg)]

    qt = [jnp.concatenate([qt_ref[h * dh:(h + 1) * dh, c * Q_BLOCK:(c + 1) * Q_BLOCK] for h in range(hg)], axis=1)
          for c in chains]
    t_row = [t0[c] + lax.broadcasted_iota(jnp.int32, (1, Q_BLOCK), 1) for c in chains]

    n_cmp_pad = kc_ref.shape[3]
    n_slc = ovl_ref.shape[0]
    n_cmp = min(n_cmp_pad, _round_up((t_end - CMP_BLOCK) // CMP_STRIDE + 1, 16))
    n_cand = min(n_slc, t_end // SLC_BLOCK)
    n_blk = min(n_slc, _round_up(n_cand, 8))
    cstart = lax.broadcasted_iota(jnp.int32, (n_cmp, Q_BLOCK), 0) * CMP_STRIDE
    centre = cstart.astype(F32) + (CMP_BLOCK - 1) / 2
    jj = lax.broadcasted_iota(jnp.int32, (n_blk, Q_BLOCK), 0)
    kc, vct, ovl = kc_ref[0, 0, 0, 0:n_cmp, :], vct_ref[0, 0, 0], ovl_ref[0:n_blk, :]
    o_cmp, member = [], []
    for c in chains:
        centre_dist = t_row[c].astype(F32) - centre
        valid = (cstart + (CMP_BLOCK - 1)) <= t_row[c]
        sc = _dot(kc, qt[c])
        p_heads = []
        for h in range(hg):
            s = _head(sc, h) * c1 - slope2[h] * centre_dist
            s = jnp.where(valid, s, -jnp.inf)
            m = jnp.max(s, axis=0, keepdims=True)
            m = jnp.where(jnp.isfinite(m), m, 0.0)
            e = jnp.where(valid, jnp.exp2(s - m), 0.0)
            p = e / jnp.maximum(jnp.sum(e, axis=0, keepdims=True), F32_TINY)
            p_heads.append(_pad_rows(p.astype(BF16), n_cmp_pad))
        o_cmp.append(_dot(vct, jnp.concatenate(p_heads, axis=1)))

        imp = _dot(ovl, p_heads[0])
        for h in range(1, hg):
            imp += _dot(ovl, p_heads[h])
        blk_t = t_row[c] // SLC_BLOCK
        forced = (jj == 0) | (jj == blk_t) | (jj == blk_t - 1)
        imp = jnp.where(forced, SEL_FORCE, jnp.where(jj <= blk_t, imp, -SEL_FORCE))
        rank = jnp.zeros((n_blk, Q_BLOCK), jnp.int32)
        for i in range(n_cand):
            ri = imp[i:i + 1, :]
            before = (ri > imp) | ((ri == imp) & (jj > i))
            rank += before.astype(jnp.int32)
        member.append(_pad_rows(jnp.where(rank < min(SLC_TOP_N, n_slc), 1.0, 0.0).astype(BF16), n_slc))

    key_iota = {}

    def key_local(n):
        if n not in key_iota:
            key_iota[n] = lax.broadcasted_iota(jnp.int32, (n, Q_BLOCK), 0)
        return key_iota[n]

    def tiles(lo, hi):
        return [(p, min(KV_TILE, hi - p)) for p in range(lo, hi, KV_TILE)]

    streams = []
    for c in chains:
        streams.append([("slc", c, ks_ref, vst_ref, lo, n) for lo, n in tiles(0, t0[c] + Q_BLOCK)])
    for c in chains:
        streams.append([("win", c, kw_ref, vwt_ref, lo, n)
                        for lo, n in tiles(max(t0[c] - WINDOW, 0), t0[c] + Q_BLOCK)])
    items = [s[i] for i in range(max(len(s) for s in streams)) for s in streams if i < len(s)]

    def scores(item):
        _, c, k_ref, _, lo, n = item
        return _dot(k_ref[lo:lo + n, :], qt[c])

    def finish(state, item, sc):
        branch, c, _, vt_ref, lo, n = item
        dist = t_row[c] - (lo + key_local(n))
        masks = []
        if branch == "slc":
            masks.append(_dot(exp_ref[lo:lo + n, :], member[c]) > 0.5)
        if lo + n - 1 > t0[c]:
            masks.append(dist >= 0)
        if branch == "win" and t0[c] + Q_BLOCK - 1 - lo >= WINDOW:
            masks.append(dist < WINDOW)
        mask = functools.reduce(jnp.logical_and, masks) if masks else None
        dist_f = dist.astype(F32)
        m_new, l_new, alpha, p_t = [], [], [], []
        for h in range(hg):
            s = _head(sc, h) * c1 - slope2[h] * dist_f
            if mask is not None:
                s = jnp.where(mask, s, NEG_BIG)
            m_n = jnp.max(s, axis=0, keepdims=True)
            if state is not None:
                m_o = _head(state[0], h)
                m_n = jnp.maximum(m_o, m_n)
                a = jnp.exp2(m_o - m_n)
                alpha.append(a)
            p = jnp.exp2(s - m_n)
            l_n = jnp.sum(p, axis=0, keepdims=True)
            if state is not None:
                l_n = a * _head(state[1], h) + l_n
            m_new.append(m_n)
            l_new.append(l_n)
            p_t.append(p.astype(BF16))
        acc = _dot(vt_ref[:, lo:lo + n], jnp.concatenate(p_t, axis=1))
        if state is not None:
            acc = jnp.concatenate(alpha, axis=1) * state[2] + acc
        return jnp.concatenate(m_new, axis=1), jnp.concatenate(l_new, axis=1), acc

    states = {}
    sc_next = scores(items[0])
    for i, item in enumerate(items):
        sc = sc_next
        if i + 1 < len(items):
            sc_next = scores(items[i + 1])
        states[item[:2]] = finish(states.get(item[:2]), item, sc)

    for c in chains:
        gt = gt_ref[0, :, c * Q_BLOCK:(c + 1) * Q_BLOCK]

        def gate(br):
            return jnp.concatenate([gt[br * hg + h:br * hg + h + 1, :] for h in range(hg)], axis=1)

        (_, l_s, acc_s), (_, l_w, acc_w) = states[("slc", c)], states[("win", c)]
        o = (gate(0) * o_cmp[c] + (gate(1) / jnp.maximum(l_s, F32_TINY)) * acc_s
             + (gate(2) / jnp.maximum(l_w, F32_TINY)) * acc_w)
        o_ref[c * Q_BLOCK:(c + 1) * Q_BLOCK, :] = jnp.concatenate(
            [_head(o, h).T for h in range(hg)], axis=1).astype(o_ref.dtype)


def _nsa(qt, kv_cmp, k_slc, vt_slc, k_win, vt_win, gates_t, slopes2, ovl, expand, batch, seq):
    g = NSA_KV_GROUPS
    gw = HEADS_PER_GROUP * NSA_HEAD_DIM
    nq = seq // Q_STEP
    n_cmp_pad = kv_cmp.shape[3]
    out = jnp.zeros((batch * seq, NSA_WIDTH), BF16)
    for step in range(nq):
        per_seq = 1
        while seq // (2 * per_seq) >= (step + 1) * Q_STEP:
            per_seq *= 2
        n_keys = seq // per_seq
        k_spec = pl.BlockSpec((n_keys, NSA_HEAD_DIM), lambda b, gg, per_seq=per_seq: (b * per_seq, gg))
        vt_spec = pl.BlockSpec((NSA_HEAD_DIM, n_keys), lambda b, gg, per_seq=per_seq: (gg, b * per_seq))
        row_blk = lambda b, gg, step=step: (b * nq + step, gg)
        out = pl.pallas_call(
            functools.partial(_nsa_kernel, step=step),
            grid=(batch, g),
            in_specs=[pl.BlockSpec(memory_space=pltpu.SMEM),
                      pl.BlockSpec((gw, Q_STEP), lambda b, gg, step=step: (gg, b * nq + step)),
                      pl.BlockSpec((1, 1, 1, n_cmp_pad, NSA_HEAD_DIM), lambda b, gg: (0, b, gg, 0, 0)),
                      pl.BlockSpec((1, 1, 1, NSA_HEAD_DIM, n_cmp_pad), lambda b, gg: (1, b, gg, 0, 0)),
                      k_spec, vt_spec, k_spec, vt_spec,
                      pl.BlockSpec((1, GATE_ROWS, Q_STEP), lambda b, gg, step=step: (gg, 0, b * nq + step)),
                      pl.BlockSpec(ovl.shape, lambda b, gg: (0, 0)),
                      pl.BlockSpec(expand.shape, lambda b, gg: (0, 0)),
                      pl.BlockSpec(memory_space=pl.ANY)],
            out_specs=pl.BlockSpec((Q_STEP, gw), row_blk),
            out_shape=jax.ShapeDtypeStruct((batch * seq, NSA_WIDTH), BF16),
            input_output_aliases={11: 0},
            compiler_params=_params("parallel", "parallel"),
            name=f"nsa{step}",
        )(slopes2, qt, kv_cmp, kv_cmp, k_slc, vt_slc, k_win, vt_win, gates_t, ovl, expand, out)
    return out


def _ret_kernel(cd_ref, q_ref, k_ref, v_ref, g_ref, gain_ref, decay_ref, zeta_ref, xi_ref, o_ref, state_ref):
    @pl.when(pl.program_id(1) == 0)
    def _():
        state_ref[...] = jnp.zeros_like(state_ref)

    dk, dv = RET_KEY_DIM, RET_VAL_DIM
    for h in range(RET_HEADS):
        q = q_ref[:, h * dk:(h + 1) * dk]
        k = k_ref[:, h * dk:(h + 1) * dk]
        v = v_ref[:, h * dv:(h + 1) * dv]
        sc = _dot_nt(q.astype(BF16), k.astype(BF16)) * decay_ref[h]
        inner = _dot(sc.astype(BF16), v)
        st = state_ref[h]
        cross = _dot((q * xi_ref[h]).astype(BF16), st.astype(BF16))
        kz_t = (k * zeta_ref[h]).T.astype(BF16)
        state_ref[h] = st * cd_ref[h] + _dot(kz_t, v)
        y = inner + cross
        yc = y - jnp.mean(y, axis=-1, keepdims=True)
        yn = yc * lax.rsqrt(jnp.mean(yc * yc, axis=-1, keepdims=True) + EPS) * gain_ref[:, h * dv:(h + 1) * dv]
        o_ref[:, h * dv:(h + 1) * dv] = (_silu(g_ref[:, h * dv:(h + 1) * dv]) * yn).astype(o_ref.dtype)


def _retention(q, k, v, g, gain, consts, batch, seq):
    chunk_decay, decay, zeta, xi = consts
    c = RET_CHUNK
    nc = seq // c
    row = lambda b, ci: (b * nc + ci, 0)
    whole3 = lambda b, ci: (0, 0, 0)
    return pl.pallas_call(
        _ret_kernel,
        grid=(batch, nc),
        in_specs=[pl.BlockSpec(memory_space=pltpu.SMEM),
                  pl.BlockSpec((c, RET_QK_WIDTH), row),
                  pl.BlockSpec((c, RET_QK_WIDTH), row),
                  pl.BlockSpec((c, RET_V_WIDTH), row),
                  pl.BlockSpec((c, RET_V_WIDTH), row),
                  pl.BlockSpec((1, RET_V_WIDTH), lambda b, ci: (0, 0)),
                  pl.BlockSpec((RET_HEADS, c, c), whole3),
                  pl.BlockSpec((RET_HEADS, c, 1), whole3),
                  pl.BlockSpec((RET_HEADS, c, 1), whole3)],
        out_specs=pl.BlockSpec((c, RET_V_WIDTH), row),
        out_shape=jax.ShapeDtypeStruct((batch * seq, RET_V_WIDTH), BF16),
        scratch_shapes=[pltpu.VMEM((RET_HEADS, RET_KEY_DIM, RET_VAL_DIM), F32)],
        compiler_params=_params("parallel", "arbitrary"),
        name="retention",
    )(chunk_decay, q, k, v, g, gain, decay, zeta, xi)


def _retention_consts():
    c = RET_CHUNK
    log_gamma = jnp.log1p(-jnp.exp2(-5.0 - jnp.arange(RET_HEADS, dtype=F32)))
    n = jnp.arange(c, dtype=F32)
    diff = n[:, None] - n[None, :]
    decay = jnp.where(diff >= 0, jnp.exp(log_gamma[:, None, None] * jnp.maximum(diff, 0.0)), 0.0)
    zeta = jnp.exp(log_gamma[:, None] * (c - 1 - n)[None, :])[:, :, None]
    xi = jnp.exp(log_gamma[:, None] * (n + 1.0)[None, :])[:, :, None]
    chunk_decay = jnp.exp(log_gamma * c)
    return chunk_decay, decay, zeta, xi


def _merge_kernel(x_ref, gt_ref, on_ref, or_ref, ga_ref, gb_ref, wpn_ref, wpr_ref, wo_ref, o_ref, acc_ref):
    j = pl.program_id(1)

    @pl.when(j == 0)
    def _():
        acc_ref[...] = jnp.zeros_like(acc_ref)

    a = _dot(on_ref[...], wpn_ref[...])
    b = _dot(or_ref[...], wpr_ref[...])
    merged = jax.nn.sigmoid(ga_ref[...]) * a + jax.nn.sigmoid(gb_ref[...]) * b
    acc_ref[...] += _dot(merged.astype(BF16), wo_ref[...])

    @pl.when(j == pl.num_programs(1) - 1)
    def _():
        o_ref[...] = x_ref[...] + gt_ref[0] * acc_ref[...]


def _merge(x2, gate, o_nsa, o_ret, ga, gb, wpn, wpr, wo, seq):
    t, d = x2.shape
    tm, tn = 512, 512
    per_b = seq // tm
    return pl.pallas_call(
        _merge_kernel,
        grid=(t // tm, d // tn),
        in_specs=[pl.BlockSpec((tm, d), lambda i, j: (i, 0)),
                  pl.BlockSpec((1, 1, d), lambda i, j: (i // per_b, 0, 0)),
                  pl.BlockSpec((tm, o_nsa.shape[1]), lambda i, j: (i, 0)),
                  pl.BlockSpec((tm, o_ret.shape[1]), lambda i, j: (i, 0)),
                  pl.BlockSpec((tm, tn), lambda i, j: (i, j)),
                  pl.BlockSpec((tm, tn), lambda i, j: (i, j)),
                  pl.BlockSpec((wpn.shape[0], tn), lambda i, j: (0, j)),
                  pl.BlockSpec((wpr.shape[0], tn), lambda i, j: (0, j)),
                  pl.BlockSpec((tn, d), lambda i, j: (j, 0))],
        out_specs=pl.BlockSpec((tm, d), lambda i, j: (i, 0)),
        out_shape=jax.ShapeDtypeStruct((t, d), F32),
        scratch_shapes=[pltpu.VMEM((tm, d), F32)],
        compiler_params=_params("parallel", "arbitrary"),
        name="merge",
    )(x2, gate, o_nsa, o_ret, ga, gb, wpn, wpr, wo)


def _nsa_tables(seq):
    n_cmp = (seq - CMP_BLOCK) // CMP_STRIDE + 1
    n_cmp_pad = seq // CMP_STRIDE
    n_slc = seq // SLC_BLOCK
    cs = (np.arange(n_cmp) * CMP_STRIDE)[:, None]
    js = (np.arange(n_slc) * SLC_BLOCK)[None, :]
    overlap = np.clip(np.minimum(cs + CMP_BLOCK, js + SLC_BLOCK) - np.maximum(cs, js), 0, None) / CMP_BLOCK
    ovl = np.zeros((n_slc, n_cmp_pad), np.float32)
    ovl[:, :n_cmp] = overlap.T
    expand = (np.arange(seq)[:, None] // SLC_BLOCK == np.arange(n_slc)[None, :]).astype(np.float32)
    slopes2 = jnp.exp2(-8.0 * jnp.arange(1, NSA_HEADS + 1, dtype=F32) / NSA_HEADS) * LOG2E
    return jnp.asarray(ovl, BF16), jnp.asarray(expand, BF16), slopes2


def kernel(x, c, w_ada, b_ada, g_norm, w_ffn_gate, w_ffn_up, w_ffn_down, w_in, g_qk, cmp_pos, cmp_w1, cmp_b1,
           cmp_w2, ret_gn_gain, w_proj_nsa, w_proj_ret, w_out):
    batch, seq, d = x.shape
    depth = w_ada.shape[0]
    t = batch * seq
    x2 = x.reshape(t, d)
    c_pad = jnp.pad(c, ((0, 8 - batch), (0, 0)))
    ovl, expand, slopes2 = _nsa_tables(seq)
    ret_consts = _retention_consts()
    hg = HEADS_PER_GROUP

    for l in range(depth):
        ada = _ada(c_pad, w_ada[l], b_ada[l][None, :])[:batch].reshape(batch, N_ADA, 1, d)
        sh1, sc1, gt1, sh2, sc2, gt2, sh3, sc3, gt3 = [ada[:, i] for i in range(N_ADA)]

        x2 = _ffn(x2, sh1, sc1, gt1, g_norm[l, 0][None, :], w_ffn_gate[l], w_ffn_up[l], w_ffn_down[l], 0, seq)

        u = _mod(x2, sh2, sc2, g_norm[l, 1][None, :], seq)
        w = w_in[l]
        off = [0]

        def take(width):
            w_part = w[:, off[0]:off[0] + width]
            off[0] += width
            return w_part.astype(BF16)

        gq = g_qk[l]
        qt_nsa = _proj_t(u, take(NSA_WIDTH), BF16, gq[0], name="proj_q")
        cmp_raw = _proj(u, take(2 * KV_WIDTH), F32, name="proj_cmp_raw")
        k_slc = _proj(u, take(KV_WIDTH), BF16, "headnorm", gq[2][None, :], name="proj_k_slc")
        vt_slc = _proj_t(u, take(KV_WIDTH), BF16, name="proj_v_slc")
        k_win = _proj(u, take(KV_WIDTH), BF16, "headnorm", gq[3][None, :], name="proj_k_win")
        vt_win = _proj_t(u, take(KV_WIDTH), BF16, name="proj_v_win")
        w_gl = take(3 * NSA_HEADS).reshape(d, 3, NSA_KV_GROUPS, hg).transpose(0, 2, 1, 3)
        w_gl = jnp.pad(w_gl.reshape(d, NSA_KV_GROUPS, 3 * hg),
                       ((0, 0), (0, 0), (0, LANES - 3 * hg))).reshape(d, NSA_KV_GROUPS * LANES)
        gates = _proj(u, w_gl, F32, "sigmoid", name="proj_gates")
        gates_t = gates.reshape(t, NSA_KV_GROUPS, LANES)[:, :, :GATE_ROWS].transpose(1, 2, 0)
        q_r = _proj(u, take(RET_QK_WIDTH), F32, name="proj_q_ret")
        k_r = _proj(u, take(RET_QK_WIDTH), F32, "keyscale", name="proj_k_ret")
        v_r = _proj(u, take(RET_V_WIDTH), BF16, name="proj_v_ret")
        g_r = _proj(u, take(RET_V_WIDTH), F32, name="proj_g_ret")
        ga = _proj(u, take(d), F32, name="proj_ga")
        gb = _proj(u, take(d), F32, name="proj_gb")

        kv_cmp = _compress(cmp_raw.reshape(batch, seq, 2 * KV_WIDTH), cmp_pos[l], cmp_w1[l].astype(BF16),
                           cmp_b1[l][:, None, :], cmp_w2[l].astype(BF16), gq[1][None, :])
        o_nsa = _nsa(qt_nsa, kv_cmp, k_slc, vt_slc, k_win, vt_win, gates_t, slopes2, ovl, expand, batch, seq)
        o_ret = _retention(q_r, k_r, v_r, g_r, ret_gn_gain[l].reshape(1, RET_V_WIDTH), ret_consts, batch, seq)

        x2 = _merge(x2, gt2, o_nsa, o_ret, ga, gb, w_proj_nsa[l].astype(BF16), w_proj_ret[l].astype(BF16),
                    w_out[l].astype(BF16), seq)

        x2 = _ffn(x2, sh3, sc3, gt3, g_norm[l, 2][None, :], w_ffn_gate[l], w_ffn_up[l], w_ffn_down[l], 1, seq)

    return x2.reshape(batch, seq, d)
```

```python
import functools
import math

import numpy as np
import jax
import jax.numpy as jnp
from jax import lax
from jax.experimental import pallas as pl
from jax.experimental.pallas import tpu as pltpu

F32 = jnp.float32
BF16 = jnp.bfloat16

D_MODEL = 2048
NSA_HEADS = 16
NSA_KV_GROUPS = 4
HEADS_PER_GROUP = NSA_HEADS // NSA_KV_GROUPS
NSA_HEAD_DIM = 128
CMP_BLOCK = 32
CMP_STRIDE = 16
SLC_BLOCK = 64
SLC_TOP_N = 16
WINDOW = 512
RET_HEADS = 8
RET_KEY_DIM = 128
RET_VAL_DIM = 256
RET_CHUNK = 128
D_FF = 5632
N_ADA = 9
EPS = 1e-6
SEL_FORCE = 1e4

NSA_WIDTH = NSA_HEADS * NSA_HEAD_DIM
KV_WIDTH = NSA_KV_GROUPS * NSA_HEAD_DIM
RET_QK_WIDTH = RET_HEADS * RET_KEY_DIM
RET_V_WIDTH = RET_HEADS * RET_VAL_DIM

LANES = 128
VMEM_LIMIT_BYTES = 56 * 1024 * 1024
NEG_BIG = -1e30
F32_TINY = float(np.finfo(np.float32).tiny)
LOG2E = math.log2(math.e)

Q_BLOCK = 128
Q_CHAINS = 2
Q_STEP = Q_CHAINS * Q_BLOCK
KV_TILE = 512
GATE_ROWS = 16


def _params(*sem):
    return pltpu.CompilerParams(dimension_semantics=sem, vmem_limit_bytes=VMEM_LIMIT_BYTES)


def _silu(x):
    return x * jax.nn.sigmoid(x)


def _rms(x, g):
    return x * lax.rsqrt(jnp.mean(x * x, axis=-1, keepdims=True) + EPS) * g


def _modulate(x, g, shift, scale):
    return _rms(x, g) * (1.0 + scale) + shift


def _dot(a, b):
    return jnp.dot(a, b, preferred_element_type=F32)


def _dot_nt(a, b):
    return lax.dot_general(a, b, (((1,), (1,)), ((), ())), preferred_element_type=F32)


def _ada_kernel(c_ref, w_ref, b_ref, o_ref):
    cond = _silu(c_ref[...]).astype(BF16)
    o_ref[...] = _dot(cond, w_ref[...].astype(BF16)) + b_ref[...]


def _ada(c_pad, w, b):
    rows, d = c_pad.shape
    n = w.shape[1]
    tn = 1024
    return pl.pallas_call(
        _ada_kernel,
        grid=(n // tn,),
        in_specs=[pl.BlockSpec((rows, d), lambda j: (0, 0)),
                  pl.BlockSpec((d, tn), lambda j: (0, j)),
                  pl.BlockSpec((1, tn), lambda j: (0, j))],
        out_specs=pl.BlockSpec((rows, tn), lambda j: (0, j)),
        out_shape=jax.ShapeDtypeStruct((rows, n), F32),
        compiler_params=_params("parallel"),
        name="ada",
    )(c_pad, w, b)


def _ffn_kernel(x_ref, sh_ref, sc_ref, gt_ref, g_ref, wg_ref, wu_ref, wd_ref, o_ref, h_ref):
    j = pl.program_id(1)

    @pl.when(j == 0)
    def _():
        h_ref[...] = _modulate(x_ref[...], g_ref[...], sh_ref[0], sc_ref[0]).astype(BF16)
        o_ref[...] = jnp.zeros_like(o_ref)

    h = h_ref[...]
    a = _dot(h, wg_ref[...].astype(BF16))
    b = _dot(h, wu_ref[...].astype(BF16))
    act = (_silu(a) * b).astype(BF16)
    o_ref[...] += _dot(act, wd_ref[...].astype(BF16))

    @pl.when(j == pl.num_programs(1) - 1)
    def _():
        o_ref[...] = x_ref[...] + (0.5 * gt_ref[0]) * o_ref[...]


def _ffn(x2, shift, scale, gate, g, wg, wu, wd, which, seq):
    t, d = x2.shape
    ff = wg.shape[2]
    tm, tf = 1024, 256
    per_b = seq // tm
    mod_spec = pl.BlockSpec((1, 1, d), lambda i, j: (i // per_b, 0, 0))
    return pl.pallas_call(
        _ffn_kernel,
        grid=(t // tm, ff // tf),
        in_specs=[pl.BlockSpec((tm, d), lambda i, j: (i, 0), pipeline_mode=pl.Buffered(1)),
                  mod_spec, mod_spec, mod_spec,
                  pl.BlockSpec((1, d), lambda i, j: (0, 0)),
                  pl.BlockSpec((None, d, tf), lambda i, j: (which, 0, j)),
                  pl.BlockSpec((None, d, tf), lambda i, j: (which, 0, j)),
                  pl.BlockSpec((None, tf, d), lambda i, j: (which, j, 0))],
        out_specs=pl.BlockSpec((tm, d), lambda i, j: (i, 0)),
        out_shape=jax.ShapeDtypeStruct((t, d), F32),
        scratch_shapes=[pltpu.VMEM((tm, d), BF16)],
        compiler_params=_params("parallel", "arbitrary"),
        name="ffn",
    )(x2, shift, scale, gate, g, wg, wu, wd)


def _mod_kernel(x_ref, sh_ref, sc_ref, g_ref, o_ref):
    o_ref[...] = _modulate(x_ref[...], g_ref[...], sh_ref[0], sc_ref[0]).astype(BF16)


def _mod(x2, shift, scale, g, seq):
    t, d = x2.shape
    tm = 512
    per_b = seq // tm
    mod_spec = pl.BlockSpec((1, 1, d), lambda i: (i // per_b, 0, 0))
    return pl.pallas_call(
        _mod_kernel,
        grid=(t // tm,),
        in_specs=[pl.BlockSpec((tm, d), lambda i: (i, 0)), mod_spec, mod_spec,
                  pl.BlockSpec((1, d), lambda i: (0, 0))],
        out_specs=pl.BlockSpec((tm, d), lambda i: (i, 0)),
        out_shape=jax.ShapeDtypeStruct((t, d), BF16),
        compiler_params=_params("parallel"),
        name="modulate",
    )(x2, shift, scale, g)


def _proj_kernel(u_ref, w_ref, gain_ref, o_ref, *, epilogue):
    acc = _dot(u_ref[...], w_ref[...])
    if epilogue == "headnorm":
        g = gain_ref[...]
        parts = [_rms(acc[:, k:k + NSA_HEAD_DIM], g) for k in range(0, acc.shape[1], NSA_HEAD_DIM)]
        acc = jnp.concatenate(parts, axis=1)
    elif epilogue == "keyscale":
        acc = acc * (RET_KEY_DIM ** -0.5)
    elif epilogue == "sigmoid":
        acc = jax.nn.sigmoid(acc)
    o_ref[...] = acc.astype(o_ref.dtype)


def _proj(u, w, cols, out_dtype, epilogue="plain", gain=None, name="proj"):
    t, d = u.shape
    col0, n = cols
    tm = 1024
    tn = 1024 if n % 1024 == 0 and col0 % 1024 == 0 else 512
    assert n % tn == 0 and col0 % tn == 0
    if gain is None:
        gain = jnp.ones((1, NSA_HEAD_DIM), F32)
    return pl.pallas_call(
        functools.partial(_proj_kernel, epilogue=epilogue),
        grid=(t // tm, n // tn),
        in_specs=[pl.BlockSpec((tm, d), lambda i, j: (i, 0)),
                  pl.BlockSpec((d, tn), lambda i, j: (0, col0 // tn + j)),
                  pl.BlockSpec((1, NSA_HEAD_DIM), lambda i, j: (0, 0))],
        out_specs=pl.BlockSpec((tm, tn), lambda i, j: (i, j)),
        out_shape=jax.ShapeDtypeStruct((t, n), out_dtype),
        compiler_params=_params("parallel", "parallel"),
        name=name,
    )(u, w, gain)


def _proj_t_kernel(u_ref, w_ref, gain_ref, o_ref, *, headnorm):
    acc = lax.dot_general(w_ref[...], u_ref[...], (((0,), (1,)), ((), ())), preferred_element_type=F32)
    if headnorm:
        g = gain_ref[...]
        parts = []
        for k in range(0, acc.shape[0], NSA_HEAD_DIM):
            xh = acc[k:k + NSA_HEAD_DIM]
            parts.append(xh * lax.rsqrt(jnp.mean(xh * xh, axis=0, keepdims=True) + EPS) * g)
        acc = jnp.concatenate(parts, axis=0)
    o_ref[...] = acc.astype(o_ref.dtype)


def _proj_t(u, w, cols, out_dtype, gain=None, name="proj_t"):
    t, d = u.shape
    col0, n = cols
    tm = 1024
    tn = 1024 if n % 1024 == 0 and col0 % 1024 == 0 else 512
    assert n % tn == 0 and col0 % tn == 0
    headnorm = gain is not None
    if gain is None:
        gain = jnp.ones((NSA_HEAD_DIM,), F32)
    return pl.pallas_call(
        functools.partial(_proj_t_kernel, headnorm=headnorm),
        grid=(t // tm, n // tn),
        in_specs=[pl.BlockSpec((tm, d), lambda i, j: (i, 0)),
                  pl.BlockSpec((d, tn), lambda i, j: (0, col0 // tn + j)),
                  pl.BlockSpec((NSA_HEAD_DIM, 1), lambda i, j: (0, 0))],
        out_specs=pl.BlockSpec((tn, tm), lambda i, j: (j, i)),
        out_shape=jax.ShapeDtypeStruct((n, t), out_dtype),
        compiler_params=_params("parallel", "parallel"),
        name=name,
    )(u, w, gain[:, None])


def _cmp_kernel(z_ref, pos_ref, w1_ref, b1_ref, w2_ref, gk_ref, o_ref):
    half = CMP_BLOCK // 2
    n_rows = z_ref.shape[1] // CMP_STRIDE
    y_lo = jnp.zeros((n_rows, NSA_HEAD_DIM), F32)
    y_hi = jnp.zeros((n_rows, NSA_HEAD_DIM), F32)
    for l in range(half):
        zl = z_ref[0, pl.ds(l, n_rows, stride=CMP_STRIDE), :]
        lo = (zl + pos_ref[0, l:l + 1, :]).astype(BF16)
        hi = (zl + pos_ref[0, half + l:half + l + 1, :]).astype(BF16)
        y_lo += _dot(lo, w1_ref[0, l * NSA_HEAD_DIM:(l + 1) * NSA_HEAD_DIM, :])
        y_hi += _dot(hi, w1_ref[0, (half + l) * NSA_HEAD_DIM:(half + l + 1) * NSA_HEAD_DIM, :])
    y = y_lo + pltpu.roll(y_hi, n_rows - 1, 0)
    hdn = jax.nn.gelu(y + b1_ref[0]).astype(BF16)
    out = _dot(hdn, w2_ref[0])
    normed = _rms(out, gk_ref[...])
    is_key = pl.program_id(0) == 0
    o_ref[0, 0, 0] = jnp.where(is_key, normed, out.T).astype(o_ref.dtype)


def _compress(raw, pos, w1, b1, w2, gk):
    b, s, _ = raw.shape
    g = NSA_KV_GROUPS
    n_rows = s // CMP_STRIDE
    assert n_rows == NSA_HEAD_DIM, "key / transposed-value tiles share one square output block"
    return pl.pallas_call(
        _cmp_kernel,
        grid=(2, b, g),
        in_specs=[pl.BlockSpec((1, s, NSA_HEAD_DIM), lambda i, bb, gg: (bb, 0, i * g + gg)),
                  pl.BlockSpec((1, CMP_BLOCK, NSA_HEAD_DIM), lambda i, bb, gg: (i, 0, 0)),
                  pl.BlockSpec((1, CMP_BLOCK * NSA_HEAD_DIM, NSA_HEAD_DIM), lambda i, bb, gg: (i, 0, 0)),
                  pl.BlockSpec((1, 1, NSA_HEAD_DIM), lambda i, bb, gg: (i, 0, 0)),
                  pl.BlockSpec((1, NSA_HEAD_DIM, NSA_HEAD_DIM), lambda i, bb, gg: (i, 0, 0)),
                  pl.BlockSpec((1, NSA_HEAD_DIM), lambda i, bb, gg: (0, 0))],
        out_specs=pl.BlockSpec((1, 1, 1, n_rows, NSA_HEAD_DIM), lambda i, bb, gg: (i, bb, gg, 0, 0)),
        out_shape=jax.ShapeDtypeStruct((2, b, g, n_rows, NSA_HEAD_DIM), BF16),
        compiler_params=_params("parallel", "parallel", "parallel"),
        name="compress",
    )(raw, pos, w1, b1, w2, gk)


def _head(x, h):
    return x[:, h * Q_BLOCK:(h + 1) * Q_BLOCK]


def _round_up(x, m):
    return (x + m - 1) // m * m


def _pad_rows(x, n):
    if x.shape[0] == n:
        return x
    return jnp.concatenate([x, jnp.zeros((n - x.shape[0], x.shape[1]), x.dtype)], axis=0)


def _nsa_kernel(slope_ref, qt_ref, kc_ref, vct_ref, ks_ref, vst_ref, kw_ref, vwt_ref, gt_ref, ovl_ref, exp_ref,
                prev_ref, o_ref, *, step):
    del prev_ref
    hg, dh = HEADS_PER_GROUP, NSA_HEAD_DIM
    chains = range(Q_CHAINS)
    grp = pl.program_id(1)
    t0 = [step * Q_STEP + c * Q_BLOCK for c in chains]
    t_end = (step + 1) * Q_STEP
    c1 = (dh ** -0.5) * LOG2E
    slope2 = [slope_ref[grp * hg + h] for h in range(hg)]

    qt = [jnp.concatenate([qt_ref[h * dh:(h + 1) * dh, c * Q_BLOCK:(c + 1) * Q_BLOCK] for h in range(hg)], axis=1)
          for c in chains]
    t_row = [t0[c] + lax.broadcasted_iota(jnp.int32, (1, Q_BLOCK), 1) for c in chains]

    n_cmp_pad = kc_ref.shape[3]
    n_slc = ovl_ref.shape[0]
    n_cmp = min(n_cmp_pad, _round_up((t_end - CMP_BLOCK) // CMP_STRIDE + 1, 16))
    n_cand = min(n_slc, t_end // SLC_BLOCK)
    n_blk = min(n_slc, _round_up(n_cand, 8))
    cstart = lax.broadcasted_iota(jnp.int32, (n_cmp, Q_BLOCK), 0) * CMP_STRIDE
    centre = cstart.astype(F32) + (CMP_BLOCK - 1) / 2
    jj = lax.broadcasted_iota(jnp.int32, (n_blk, Q_BLOCK), 0)
    kc, vct, ovl = kc_ref[0, 0, 0, 0:n_cmp, :], vct_ref[0, 0, 0], ovl_ref[0:n_blk, :]
    o_cmp, member = [], []
    for c in chains:
        centre_dist = t_row[c].astype(F32) - centre
        valid = (cstart + (CMP_BLOCK - 1)) <= t_row[c]
        sc = _dot(kc, qt[c])
        p_heads = []
        for h in range(hg):
            s = _head(sc, h) * c1 - slope2[h] * centre_dist
            s = jnp.where(valid, s, -jnp.inf)
            m = jnp.max(s, axis=0, keepdims=True)
            m = jnp.where(jnp.isfinite(m), m, 0.0)
            e = jnp.where(valid, jnp.exp2(s - m), 0.0)
            p = e / jnp.maximum(jnp.sum(e, axis=0, keepdims=True), F32_TINY)
            p_heads.append(_pad_rows(p.astype(BF16), n_cmp_pad))
        o_cmp.append(_dot(vct, jnp.concatenate(p_heads, axis=1)))

        imp = _dot(ovl, p_heads[0])
        for h in range(1, hg):
            imp += _dot(ovl, p_heads[h])
        blk_t = t_row[c] // SLC_BLOCK
        forced = (jj == 0) | (jj == blk_t) | (jj == blk_t - 1)
        imp = jnp.where(forced, SEL_FORCE, jnp.where(jj <= blk_t, imp, -SEL_FORCE))
        rank = jnp.zeros((n_blk, Q_BLOCK), jnp.int32)
        for i in range(n_cand):
            ri = imp[i:i + 1, :]
            before = (ri > imp) | ((ri == imp) & (jj > i))
            rank += before.astype(jnp.int32)
        member.append(_pad_rows(jnp.where(rank < min(SLC_TOP_N, n_slc), 1.0, 0.0).astype(BF16), n_slc))

    key_iota = {}

    def key_local(n):
        if n not in key_iota:
            key_iota[n] = lax.broadcasted_iota(jnp.int32, (n, Q_BLOCK), 0)
        return key_iota[n]

    def tiles(lo, hi):
        return [(p, min(KV_TILE, hi - p)) for p in range(lo, hi, KV_TILE)]

    streams = []
    for c in chains:
        streams.append([("slc", c, ks_ref, vst_ref, lo, n) for lo, n in tiles(0, t0[c] + Q_BLOCK)])
    for c in chains:
        streams.append([("win", c, kw_ref, vwt_ref, lo, n)
                        for lo, n in tiles(max(t0[c] - WINDOW, 0), t0[c] + Q_BLOCK)])
    items = [s[i] for i in range(max(len(s) for s in streams)) for s in streams if i < len(s)]

    def scores(item):
        _, c, k_ref, _, lo, n = item
        return _dot(k_ref[lo:lo + n, :], qt[c])

    def finish(state, item, sc):
        branch, c, _, vt_ref, lo, n = item
        dist = t_row[c] - (lo + key_local(n))
        masks = []
        if branch == "slc":
            masks.append(_dot(exp_ref[lo:lo + n, :], member[c]) > 0.5)
        if lo + n - 1 > t0[c]:
            masks.append(dist >= 0)
        if branch == "win" and t0[c] + Q_BLOCK - 1 - lo >= WINDOW:
            masks.append(dist < WINDOW)
        mask = functools.reduce(jnp.logical_and, masks) if masks else None
        dist_f = dist.astype(F32)
        m_new, l_new, alpha, p_t = [], [], [], []
        for h in range(hg):
            s = _head(sc, h) * c1 - slope2[h] * dist_f
            if mask is not None:
                s = jnp.where(mask, s, NEG_BIG)
            m_n = jnp.max(s, axis=0, keepdims=True)
            if state is not None:
                m_o = _head(state[0], h)
                m_n = jnp.maximum(m_o, m_n)
                a = jnp.exp2(m_o - m_n)
                alpha.append(a)
            p = jnp.exp2(s - m_n)
            l_n = jnp.sum(p, axis=0, keepdims=True)
            if state is not None:
                l_n = a * _head(state[1], h) + l_n
            m_new.append(m_n)
            l_new.append(l_n)
            p_t.append(p.astype(BF16))
        acc = _dot(vt_ref[:, lo:lo + n], jnp.concatenate(p_t, axis=1))
        if state is not None:
            acc = jnp.concatenate(alpha, axis=1) * state[2] + acc
        return jnp.concatenate(m_new, axis=1), jnp.concatenate(l_new, axis=1), acc

    states = {}
    sc_next = scores(items[0])
    for i, item in enumerate(items):
        sc = sc_next
        if i + 1 < len(items):
            sc_next = scores(items[i + 1])
        states[item[:2]] = finish(states.get(item[:2]), item, sc)

    for c in chains:
        gt = gt_ref[0, :, c * Q_BLOCK:(c + 1) * Q_BLOCK]

        def gate(br):
            return jnp.concatenate([gt[br * hg + h:br * hg + h + 1, :] for h in range(hg)], axis=1)

        (_, l_s, acc_s), (_, l_w, acc_w) = states[("slc", c)], states[("win", c)]
        o = (gate(0) * o_cmp[c] + (gate(1) / jnp.maximum(l_s, F32_TINY)) * acc_s
             + (gate(2) / jnp.maximum(l_w, F32_TINY)) * acc_w)
        o_ref[c * Q_BLOCK:(c + 1) * Q_BLOCK, :] = jnp.concatenate(
            [_head(o, h).T for h in range(hg)], axis=1).astype(o_ref.dtype)


def _nsa(qt, kv_cmp, k_slc, vt_slc, k_win, vt_win, gates_t, slopes2, ovl, expand, batch, seq):
    g = NSA_KV_GROUPS
    gw = HEADS_PER_GROUP * NSA_HEAD_DIM
    nq = seq // Q_STEP
    n_cmp_pad = kv_cmp.shape[3]
    out = jnp.zeros((batch * seq, NSA_WIDTH), BF16)
    for step in range(nq):
        per_seq = 1
        while seq // (2 * per_seq) >= (step + 1) * Q_STEP:
            per_seq *= 2
        n_keys = seq // per_seq
        k_spec = pl.BlockSpec((n_keys, NSA_HEAD_DIM), lambda b, gg, per_seq=per_seq: (b * per_seq, gg))
        vt_spec = pl.BlockSpec((NSA_HEAD_DIM, n_keys), lambda b, gg, per_seq=per_seq: (gg, b * per_seq))
        row_blk = lambda b, gg, step=step: (b * nq + step, gg)
        out = pl.pallas_call(
            functools.partial(_nsa_kernel, step=step),
            grid=(batch, g),
            in_specs=[pl.BlockSpec(memory_space=pltpu.SMEM),
                      pl.BlockSpec((gw, Q_STEP), lambda b, gg, step=step: (gg, b * nq + step)),
                      pl.BlockSpec((1, 1, 1, n_cmp_pad, NSA_HEAD_DIM), lambda b, gg: (0, b, gg, 0, 0)),
                      pl.BlockSpec((1, 1, 1, NSA_HEAD_DIM, n_cmp_pad), lambda b, gg: (1, b, gg, 0, 0)),
                      k_spec, vt_spec, k_spec, vt_spec,
                      pl.BlockSpec((1, GATE_ROWS, Q_STEP), lambda b, gg, step=step: (gg, 0, b * nq + step)),
                      pl.BlockSpec(ovl.shape, lambda b, gg: (0, 0)),
                      pl.BlockSpec(expand.shape, lambda b, gg: (0, 0)),
                      pl.BlockSpec(memory_space=pl.ANY)],
            out_specs=pl.BlockSpec((Q_STEP, gw), row_blk),
            out_shape=jax.ShapeDtypeStruct((batch * seq, NSA_WIDTH), BF16),
            input_output_aliases={11: 0},
            compiler_params=_params("parallel", "parallel"),
            name=f"nsa{step}",
        )(slopes2, qt, kv_cmp, kv_cmp, k_slc, vt_slc, k_win, vt_win, gates_t, ovl, expand, out)
    return out


def _ret_kernel(cd_ref, q_ref, k_ref, v_ref, g_ref, gain_ref, decay_ref, zeta_ref, xi_ref, o_ref, state_ref):
    @pl.when(pl.program_id(1) == 0)
    def _():
        state_ref[...] = jnp.zeros_like(state_ref)

    dk, dv = RET_KEY_DIM, RET_VAL_DIM
    for h in range(RET_HEADS):
        q = q_ref[:, h * dk:(h + 1) * dk]
        k = k_ref[:, h * dk:(h + 1) * dk]
        v = v_ref[:, h * dv:(h + 1) * dv]
        sc = _dot_nt(q.astype(BF16), k.astype(BF16)) * decay_ref[h]
        inner = _dot(sc.astype(BF16), v)
        st = state_ref[h]
        cross = _dot((q * xi_ref[h]).astype(BF16), st.astype(BF16))
        kz_t = (k * zeta_ref[h]).T.astype(BF16)
        state_ref[h] = st * cd_ref[h] + _dot(kz_t, v)
        y = inner + cross
        yc = y - jnp.mean(y, axis=-1, keepdims=True)
        yn = yc * lax.rsqrt(jnp.mean(yc * yc, axis=-1, keepdims=True) + EPS) * gain_ref[:, h * dv:(h + 1) * dv]
        o_ref[:, h * dv:(h + 1) * dv] = (_silu(g_ref[:, h * dv:(h + 1) * dv]) * yn).astype(o_ref.dtype)


def _retention(q, k, v, g, gain, consts, batch, seq):
    chunk_decay, decay, zeta, xi = consts
    c = RET_CHUNK
    nc = seq // c
    row = lambda b, ci: (b * nc + ci, 0)
    whole3 = lambda b, ci: (0, 0, 0)
    return pl.pallas_call(
        _ret_kernel,
        grid=(batch, nc),
        in_specs=[pl.BlockSpec(memory_space=pltpu.SMEM),
                  pl.BlockSpec((c, RET_QK_WIDTH), row),
                  pl.BlockSpec((c, RET_QK_WIDTH), row),
                  pl.BlockSpec((c, RET_V_WIDTH), row),
                  pl.BlockSpec((c, RET_V_WIDTH), row),
                  pl.BlockSpec((1, RET_V_WIDTH), lambda b, ci: (0, 0)),
                  pl.BlockSpec((RET_HEADS, c, c), whole3),
                  pl.BlockSpec((RET_HEADS, c, 1), whole3),
                  pl.BlockSpec((RET_HEADS, c, 1), whole3)],
        out_specs=pl.BlockSpec((c, RET_V_WIDTH), row),
        out_shape=jax.ShapeDtypeStruct((batch * seq, RET_V_WIDTH), BF16),
        scratch_shapes=[pltpu.VMEM((RET_HEADS, RET_KEY_DIM, RET_VAL_DIM), F32)],
        compiler_params=_params("parallel", "arbitrary"),
        name="retention",
    )(chunk_decay, q, k, v, g, gain, decay, zeta, xi)


def _retention_consts():
    c = RET_CHUNK
    log_gamma = jnp.log1p(-jnp.exp2(-5.0 - jnp.arange(RET_HEADS, dtype=F32)))
    n = jnp.arange(c, dtype=F32)
    diff = n[:, None] - n[None, :]
    decay = jnp.where(diff >= 0, jnp.exp(log_gamma[:, None, None] * jnp.maximum(diff, 0.0)), 0.0)
    zeta = jnp.exp(log_gamma[:, None] * (c - 1 - n)[None, :])[:, :, None]
    xi = jnp.exp(log_gamma[:, None] * (n + 1.0)[None, :])[:, :, None]
    chunk_decay = jnp.exp(log_gamma * c)
    return chunk_decay, decay, zeta, xi


def _merge_kernel(x_ref, gt_ref, on_ref, or_ref, ga_ref, gb_ref, wpn_ref, wpr_ref, wo_ref, o_ref, acc_ref):
    j = pl.program_id(1)

    @pl.when(j == 0)
    def _():
        acc_ref[...] = jnp.zeros_like(acc_ref)

    a = _dot(on_ref[...], wpn_ref[...])
    b = _dot(or_ref[...], wpr_ref[...])
    merged = jax.nn.sigmoid(ga_ref[...]) * a + jax.nn.sigmoid(gb_ref[...]) * b
    acc_ref[...] += _dot(merged.astype(BF16), wo_ref[...])

    @pl.when(j == pl.num_programs(1) - 1)
    def _():
        o_ref[...] = x_ref[...] + gt_ref[0] * acc_ref[...]


def _merge(x2, gate, o_nsa, o_ret, ga, gb, wpn, wpr, wo, seq):
    t, d = x2.shape
    tm, tn = 512, 512
    per_b = seq // tm
    return pl.pallas_call(
        _merge_kernel,
        grid=(t // tm, d // tn),
        in_specs=[pl.BlockSpec((tm, d), lambda i, j: (i, 0)),
                  pl.BlockSpec((1, 1, d), lambda i, j: (i // per_b, 0, 0)),
                  pl.BlockSpec((tm, o_nsa.shape[1]), lambda i, j: (i, 0)),
                  pl.BlockSpec((tm, o_ret.shape[1]), lambda i, j: (i, 0)),
                  pl.BlockSpec((tm, tn), lambda i, j: (i, j)),
                  pl.BlockSpec((tm, tn), lambda i, j: (i, j)),
                  pl.BlockSpec((wpn.shape[0], tn), lambda i, j: (0, j)),
                  pl.BlockSpec((wpr.shape[0], tn), lambda i, j: (0, j)),
                  pl.BlockSpec((tn, d), lambda i, j: (j, 0))],
        out_specs=pl.BlockSpec((tm, d), lambda i, j: (i, 0)),
        out_shape=jax.ShapeDtypeStruct((t, d), F32),
        scratch_shapes=[pltpu.VMEM((tm, d), F32)],
        compiler_params=_params("parallel", "arbitrary"),
        name="merge",
    )(x2, gate, o_nsa, o_ret, ga, gb, wpn, wpr, wo)


def _nsa_tables(seq):
    n_cmp = (seq - CMP_BLOCK) // CMP_STRIDE + 1
    n_cmp_pad = seq // CMP_STRIDE
    n_slc = seq // SLC_BLOCK
    cs = (np.arange(n_cmp) * CMP_STRIDE)[:, None]
    js = (np.arange(n_slc) * SLC_BLOCK)[None, :]
    overlap = np.clip(np.minimum(cs + CMP_BLOCK, js + SLC_BLOCK) - np.maximum(cs, js), 0, None) / CMP_BLOCK
    ovl = np.zeros((n_slc, n_cmp_pad), np.float32)
    ovl[:, :n_cmp] = overlap.T
    expand = (np.arange(seq)[:, None] // SLC_BLOCK == np.arange(n_slc)[None, :]).astype(np.float32)
    slopes2 = jnp.exp2(-8.0 * jnp.arange(1, NSA_HEADS + 1, dtype=F32) / NSA_HEADS) * LOG2E
    return jnp.asarray(ovl, BF16), jnp.asarray(expand, BF16), slopes2


def kernel(x, c, w_ada, b_ada, g_norm, w_ffn_gate, w_ffn_up, w_ffn_down, w_in, g_qk, cmp_pos, cmp_w1, cmp_b1,
           cmp_w2, ret_gn_gain, w_proj_nsa, w_proj_ret, w_out):
    batch, seq, d = x.shape
    depth = w_ada.shape[0]
    t = batch * seq
    x2 = x.reshape(t, d)
    c_pad = jnp.pad(c, ((0, 8 - batch), (0, 0)))
    ovl, expand, slopes2 = _nsa_tables(seq)
    ret_consts = _retention_consts()
    hg = HEADS_PER_GROUP

    for l in range(depth):
        ada = _ada(c_pad, w_ada[l], b_ada[l][None, :])[:batch].reshape(batch, N_ADA, 1, d)
        sh1, sc1, gt1, sh2, sc2, gt2, sh3, sc3, gt3 = [ada[:, i] for i in range(N_ADA)]

        x2 = _ffn(x2, sh1, sc1, gt1, g_norm[l, 0][None, :], w_ffn_gate[l], w_ffn_up[l], w_ffn_down[l], 0, seq)

        u = _mod(x2, sh2, sc2, g_norm[l, 1][None, :], seq)
        w = w_in[l]
        segs, off = {}, 0
        for seg_name, width in (("q", NSA_WIDTH), ("cmp_raw", 2 * KV_WIDTH), ("k_slc", KV_WIDTH),
                                ("v_slc", KV_WIDTH), ("k_win", KV_WIDTH), ("v_win", KV_WIDTH),
                                ("gates", 3 * NSA_HEADS), ("q_r", RET_QK_WIDTH), ("k_r", RET_QK_WIDTH),
                                ("v_r", RET_V_WIDTH), ("g_r", RET_V_WIDTH), ("ga", d), ("gb", d)):
            segs[seg_name] = w[:, off:off + width]
            off += width
        w_gl = segs["gates"].reshape(d, 3, NSA_KV_GROUPS, hg).transpose(0, 2, 1, 3)
        segs["gates"] = jnp.pad(w_gl.reshape(d, NSA_KV_GROUPS, 3 * hg),
                                ((0, 0), (0, 0), (0, LANES - 3 * hg))).reshape(d, NSA_KV_GROUPS * LANES)
        order = ("q", "cmp_raw", "q_r", "k_r", "v_r", "g_r", "ga", "gb", "k_slc", "v_slc", "k_win", "v_win", "gates")
        w_all = jnp.concatenate([segs[k] for k in order], axis=1).astype(BF16)
        cols, off = {}, 0
        for k in order:
            cols[k] = (off, segs[k].shape[1])
            off += segs[k].shape[1]

        gq = g_qk[l]
        qt_nsa = _proj_t(u, w_all, cols["q"], BF16, gq[0], name="proj_q")
        cmp_raw = _proj(u, w_all, cols["cmp_raw"], F32, name="proj_cmp_raw")
        k_slc = _proj(u, w_all, cols["k_slc"], BF16, "headnorm", gq[2][None, :], name="proj_k_slc")
        vt_slc = _proj_t(u, w_all, cols["v_slc"], BF16, name="proj_v_slc")
        k_win = _proj(u, w_all, cols["k_win"], BF16, "headnorm", gq[3][None, :], name="proj_k_win")
        vt_win = _proj_t(u, w_all, cols["v_win"], BF16, name="proj_v_win")
        gates = _proj(u, w_all, cols["gates"], F32, "sigmoid", name="proj_gates")
        gates_t = gates.reshape(t, NSA_KV_GROUPS, LANES)[:, :, :GATE_ROWS].transpose(1, 2, 0)
        q_r = _proj(u, w_all, cols["q_r"], F32, name="proj_q_ret")
        k_r = _proj(u, w_all, cols["k_r"], F32, "keyscale", name="proj_k_ret")
        v_r = _proj(u, w_all, cols["v_r"], BF16, name="proj_v_ret")
        g_r = _proj(u, w_all, cols["g_r"], F32, name="proj_g_ret")
        ga = _proj(u, w_all, cols["ga"], F32, name="proj_ga")
        gb = _proj(u, w_all, cols["gb"], F32, name="proj_gb")

        kv_cmp = _compress(cmp_raw.reshape(batch, seq, 2 * KV_WIDTH), cmp_pos[l], cmp_w1[l].astype(BF16),
                           cmp_b1[l][:, None, :], cmp_w2[l].astype(BF16), gq[1][None, :])
        o_nsa = _nsa(qt_nsa, kv_cmp, k_slc, vt_slc, k_win, vt_win, gates_t, slopes2, ovl, expand, batch, seq)
        o_ret = _retention(q_r, k_r, v_r, g_r, ret_gn_gain[l].reshape(1, RET_V_WIDTH), ret_consts, batch, seq)

        x2 = _merge(x2, gt2, o_nsa, o_ret, ga, gb, w_proj_nsa[l].astype(BF16), w_proj_ret[l].astype(BF16),
                    w_out[l].astype(BF16), seq)

        x2 = _ffn(x2, sh3, sc3, gt3, g_norm[l, 2][None, :], w_ffn_gate[l], w_ffn_up[l], w_ffn_down[l], 1, seq)

    return x2.reshape(batch, seq, d)
```

```python
import functools
import math

import numpy as np
import jax
import jax.numpy as jnp
from jax import lax
from jax.experimental import pallas as pl
from jax.experimental.pallas import tpu as pltpu

F32 = jnp.float32
BF16 = jnp.bfloat16

D_MODEL = 2048
NSA_HEADS = 16
NSA_KV_GROUPS = 4
HEADS_PER_GROUP = NSA_HEADS // NSA_KV_GROUPS
NSA_HEAD_DIM = 128
CMP_BLOCK = 32
CMP_STRIDE = 16
SLC_BLOCK = 64
SLC_TOP_N = 16
WINDOW = 512
RET_HEADS = 8
RET_KEY_DIM = 128
RET_VAL_DIM = 256
RET_CHUNK = 128
D_FF = 5632
N_ADA = 9
EPS = 1e-6
SEL_FORCE = 1e4

NSA_WIDTH = NSA_HEADS * NSA_HEAD_DIM
KV_WIDTH = NSA_KV_GROUPS * NSA_HEAD_DIM
RET_QK_WIDTH = RET_HEADS * RET_KEY_DIM
RET_V_WIDTH = RET_HEADS * RET_VAL_DIM

LANES = 128
VMEM_LIMIT_BYTES = 56 * 1024 * 1024
NEG_BIG = -1e30
F32_TINY = float(np.finfo(np.float32).tiny)
LOG2E = math.log2(math.e)

Q_BLOCK = 128
Q_CHAINS = 2
Q_STEP = Q_CHAINS * Q_BLOCK
KV_TILE = 512
GATE_ROWS = 16


def _params(*sem):
    return pltpu.CompilerParams(dimension_semantics=sem, vmem_limit_bytes=VMEM_LIMIT_BYTES)


def _silu(x):
    return x * jax.nn.sigmoid(x)


def _rms(x, g):
    return x * lax.rsqrt(jnp.mean(x * x, axis=-1, keepdims=True) + EPS) * g


def _modulate(x, g, shift, scale):
    return _rms(x, g) * (1.0 + scale) + shift


def _dot(a, b):
    return jnp.dot(a, b, preferred_element_type=F32)


def _dot_nt(a, b):
    return lax.dot_general(a, b, (((1,), (1,)), ((), ())), preferred_element_type=F32)


def _ada_kernel(c_ref, w_ref, b_ref, o_ref):
    cond = _silu(c_ref[...]).astype(BF16)
    o_ref[...] = _dot(cond, w_ref[...].astype(BF16)) + b_ref[...]


def _ada(c_pad, w, b):
    rows, d = c_pad.shape
    n = w.shape[1]
    tn = 1024
    return pl.pallas_call(
        _ada_kernel,
        grid=(n // tn,),
        in_specs=[pl.BlockSpec((rows, d), lambda j: (0, 0)),
                  pl.BlockSpec((d, tn), lambda j: (0, j)),
                  pl.BlockSpec((1, tn), lambda j: (0, j))],
        out_specs=pl.BlockSpec((rows, tn), lambda j: (0, j)),
        out_shape=jax.ShapeDtypeStruct((rows, n), F32),
        compiler_params=_params("parallel"),
        name="ada",
    )(c_pad, w, b)


def _ffn_step(x_ref, sh_ref, sc_ref, gt_ref, g_ref, wg, wu, wd, o_ref, h_ref):
    j = pl.program_id(1)

    @pl.when(j == 0)
    def _():
        h_ref[...] = _modulate(x_ref[...], g_ref[...], sh_ref[0], sc_ref[0]).astype(BF16)
        o_ref[...] = jnp.zeros_like(o_ref)

    h = h_ref[...]
    act = (_silu(_dot(h, wg)) * _dot(h, wu)).astype(BF16)
    o_ref[...] += _dot(act, wd)

    @pl.when(j == pl.num_programs(1) - 1)
    def _():
        o_ref[...] = x_ref[...] + (0.5 * gt_ref[0]) * o_ref[...]


def _ffn_first_kernel(x_ref, sh_ref, sc_ref, gt_ref, g_ref, wg_ref, wu_ref, wd_ref, o_ref, wg_o, wu_o, wd_o, h_ref):
    wg_o[...] = wg_ref[...].astype(BF16)
    wu_o[...] = wu_ref[...].astype(BF16)
    wd_o[...] = wd_ref[...].astype(BF16)
    _ffn_step(x_ref, sh_ref, sc_ref, gt_ref, g_ref, wg_o[...], wu_o[...], wd_o[...], o_ref, h_ref)


def _ffn_rest_kernel(x_ref, sh_ref, sc_ref, gt_ref, g_ref, wg_ref, wu_ref, wd_ref, prev_ref, o_ref, h_ref):
    del prev_ref
    _ffn_step(x_ref, sh_ref, sc_ref, gt_ref, g_ref, wg_ref[...], wu_ref[...], wd_ref[...], o_ref, h_ref)


def _ffn(x2, shift, scale, gate, g, wg, wu, wd, which, seq):
    t, d = x2.shape
    ff = wg.shape[2]
    tm1, tf1 = 1024, 256
    tm2, tf2 = 512, 512
    g_spec = pl.BlockSpec((1, d), lambda i, j: (0, 0))
    mod1 = pl.BlockSpec((1, 1, d), lambda i, j: (0, 0, 0))
    row1 = pl.BlockSpec((tm1, d), lambda i, j: (0, 0))
    wcol1 = pl.BlockSpec((d, tf1), lambda i, j: (0, j))
    wrow1 = pl.BlockSpec((tf1, d), lambda i, j: (j, 0))
    o, wg_b, wu_b, wd_b = pl.pallas_call(
        _ffn_first_kernel,
        grid=(1, ff // tf1),
        in_specs=[row1, mod1, mod1, mod1, g_spec,
                  pl.BlockSpec((None, d, tf1), lambda i, j: (which, 0, j)),
                  pl.BlockSpec((None, d, tf1), lambda i, j: (which, 0, j)),
                  pl.BlockSpec((None, tf1, d), lambda i, j: (which, j, 0))],
        out_specs=[row1, wcol1, wcol1, wrow1],
        out_shape=[jax.ShapeDtypeStruct((t, d), F32), jax.ShapeDtypeStruct((d, ff), BF16),
                   jax.ShapeDtypeStruct((d, ff), BF16), jax.ShapeDtypeStruct((ff, d), BF16)],
        scratch_shapes=[pltpu.VMEM((tm1, d), BF16)],
        compiler_params=_params("arbitrary", "arbitrary"),
        name="ffn_first",
    )(x2, shift, scale, gate, g, wg, wu, wd)

    skip = tm1 // tm2
    per_b = seq // tm2
    mod2 = pl.BlockSpec((1, 1, d), lambda i, j: ((i + skip) // per_b, 0, 0))
    row2 = pl.BlockSpec((tm2, d), lambda i, j: (i + skip, 0))
    return pl.pallas_call(
        _ffn_rest_kernel,
        grid=(t // tm2 - skip, ff // tf2),
        in_specs=[row2, mod2, mod2, mod2, g_spec,
                  pl.BlockSpec((d, tf2), lambda i, j: (0, j)),
                  pl.BlockSpec((d, tf2), lambda i, j: (0, j)),
                  pl.BlockSpec((tf2, d), lambda i, j: (j, 0)),
                  pl.BlockSpec(memory_space=pl.ANY)],
        out_specs=row2,
        out_shape=jax.ShapeDtypeStruct((t, d), F32),
        input_output_aliases={8: 0},
        scratch_shapes=[pltpu.VMEM((tm2, d), BF16)],
        compiler_params=_params("parallel", "arbitrary"),
        name="ffn_rest",
    )(x2, shift, scale, gate, g, wg_b, wu_b, wd_b, o)


def _mod_kernel(x_ref, sh_ref, sc_ref, g_ref, o_ref):
    o_ref[...] = _modulate(x_ref[...], g_ref[...], sh_ref[0], sc_ref[0]).astype(BF16)


def _mod(x2, shift, scale, g, seq):
    t, d = x2.shape
    tm = 512
    per_b = seq // tm
    mod_spec = pl.BlockSpec((1, 1, d), lambda i: (i // per_b, 0, 0))
    return pl.pallas_call(
        _mod_kernel,
        grid=(t // tm,),
        in_specs=[pl.BlockSpec((tm, d), lambda i: (i, 0)), mod_spec, mod_spec,
                  pl.BlockSpec((1, d), lambda i: (0, 0))],
        out_specs=pl.BlockSpec((tm, d), lambda i: (i, 0)),
        out_shape=jax.ShapeDtypeStruct((t, d), BF16),
        compiler_params=_params("parallel"),
        name="modulate",
    )(x2, shift, scale, g)


def _proj_kernel(u_ref, w_ref, gain_ref, o_ref, *, epilogue):
    acc = _dot(u_ref[...], w_ref[...])
    if epilogue == "headnorm":
        g = gain_ref[...]
        parts = [_rms(acc[:, k:k + NSA_HEAD_DIM], g) for k in range(0, acc.shape[1], NSA_HEAD_DIM)]
        acc = jnp.concatenate(parts, axis=1)
    elif epilogue == "keyscale":
        acc = acc * (RET_KEY_DIM ** -0.5)
    elif epilogue == "sigmoid":
        acc = jax.nn.sigmoid(acc)
    o_ref[...] = acc.astype(o_ref.dtype)


def _proj(u, w, out_dtype, epilogue="plain", gain=None, name="proj"):
    t, d = u.shape
    n = w.shape[1]
    tm = 1024
    tn = 1024 if n % 1024 == 0 else 512
    if gain is None:
        gain = jnp.ones((1, NSA_HEAD_DIM), F32)
    return pl.pallas_call(
        functools.partial(_proj_kernel, epilogue=epilogue),
        grid=(t // tm, n // tn),
        in_specs=[pl.BlockSpec((tm, d), lambda i, j: (i, 0)),
                  pl.BlockSpec((d, tn), lambda i, j: (0, j)),
                  pl.BlockSpec((1, NSA_HEAD_DIM), lambda i, j: (0, 0))],
        out_specs=pl.BlockSpec((tm, tn), lambda i, j: (i, j)),
        out_shape=jax.ShapeDtypeStruct((t, n), out_dtype),
        compiler_params=_params("parallel", "parallel"),
        name=name,
    )(u, w, gain)


def _proj_t_kernel(u_ref, w_ref, gain_ref, o_ref, *, headnorm):
    acc = lax.dot_general(w_ref[...], u_ref[...], (((0,), (1,)), ((), ())), preferred_element_type=F32)
    if headnorm:
        g = gain_ref[...]
        parts = []
        for k in range(0, acc.shape[0], NSA_HEAD_DIM):
            xh = acc[k:k + NSA_HEAD_DIM]
            parts.append(xh * lax.rsqrt(jnp.mean(xh * xh, axis=0, keepdims=True) + EPS) * g)
        acc = jnp.concatenate(parts, axis=0)
    o_ref[...] = acc.astype(o_ref.dtype)


def _proj_t(u, w, out_dtype, gain=None, name="proj_t"):
    t, d = u.shape
    n = w.shape[1]
    tm = 1024
    tn = 1024 if n % 1024 == 0 else 512
    headnorm = gain is not None
    if gain is None:
        gain = jnp.ones((NSA_HEAD_DIM,), F32)
    return pl.pallas_call(
        functools.partial(_proj_t_kernel, headnorm=headnorm),
        grid=(t // tm, n // tn),
        in_specs=[pl.BlockSpec((tm, d), lambda i, j: (i, 0)),
                  pl.BlockSpec((d, tn), lambda i, j: (0, j)),
                  pl.BlockSpec((NSA_HEAD_DIM, 1), lambda i, j: (0, 0))],
        out_specs=pl.BlockSpec((tn, tm), lambda i, j: (j, i)),
        out_shape=jax.ShapeDtypeStruct((n, t), out_dtype),
        compiler_params=_params("parallel", "parallel"),
        name=name,
    )(u, w, gain[:, None])


def _cmp_kernel(z_ref, pos_ref, w1_ref, b1_ref, w2_ref, gk_ref, o_ref):
    half = CMP_BLOCK // 2
    n_rows = z_ref.shape[1] // CMP_STRIDE
    y_lo = jnp.zeros((n_rows, NSA_HEAD_DIM), F32)
    y_hi = jnp.zeros((n_rows, NSA_HEAD_DIM), F32)
    for l in range(half):
        zl = z_ref[0, pl.ds(l, n_rows, stride=CMP_STRIDE), :]
        lo = (zl + pos_ref[0, l:l + 1, :]).astype(BF16)
        hi = (zl + pos_ref[0, half + l:half + l + 1, :]).astype(BF16)
        y_lo += _dot(lo, w1_ref[0, l * NSA_HEAD_DIM:(l + 1) * NSA_HEAD_DIM, :])
        y_hi += _dot(hi, w1_ref[0, (half + l) * NSA_HEAD_DIM:(half + l + 1) * NSA_HEAD_DIM, :])
    y = y_lo + pltpu.roll(y_hi, n_rows - 1, 0)
    hdn = jax.nn.gelu(y + b1_ref[0]).astype(BF16)
    out = _dot(hdn, w2_ref[0])
    normed = _rms(out, gk_ref[...])
    is_key = pl.program_id(0) == 0
    o_ref[0, 0, 0] = jnp.where(is_key, normed, out.T).astype(o_ref.dtype)


def _compress(raw, pos, w1, b1, w2, gk):
    b, s, _ = raw.shape
    g = NSA_KV_GROUPS
    n_rows = s // CMP_STRIDE
    assert n_rows == NSA_HEAD_DIM, "key / transposed-value tiles share one square output block"
    return pl.pallas_call(
        _cmp_kernel,
        grid=(2, b, g),
        in_specs=[pl.BlockSpec((1, s, NSA_HEAD_DIM), lambda i, bb, gg: (bb, 0, i * g + gg)),
                  pl.BlockSpec((1, CMP_BLOCK, NSA_HEAD_DIM), lambda i, bb, gg: (i, 0, 0)),
                  pl.BlockSpec((1, CMP_BLOCK * NSA_HEAD_DIM, NSA_HEAD_DIM), lambda i, bb, gg: (i, 0, 0)),
                  pl.BlockSpec((1, 1, NSA_HEAD_DIM), lambda i, bb, gg: (i, 0, 0)),
                  pl.BlockSpec((1, NSA_HEAD_DIM, NSA_HEAD_DIM), lambda i, bb, gg: (i, 0, 0)),
                  pl.BlockSpec((1, NSA_HEAD_DIM), lambda i, bb, gg: (0, 0))],
        out_specs=pl.BlockSpec((1, 1, 1, n_rows, NSA_HEAD_DIM), lambda i, bb, gg: (i, bb, gg, 0, 0)),
        out_shape=jax.ShapeDtypeStruct((2, b, g, n_rows, NSA_HEAD_DIM), BF16),
        compiler_params=_params("parallel", "parallel", "parallel"),
        name="compress",
    )(raw, pos, w1, b1, w2, gk)


def _head(x, h):
    return x[:, h * Q_BLOCK:(h + 1) * Q_BLOCK]


def _round_up(x, m):
    return (x + m - 1) // m * m


def _pad_rows(x, n):
    if x.shape[0] == n:
        return x
    return jnp.concatenate([x, jnp.zeros((n - x.shape[0], x.shape[1]), x.dtype)], axis=0)


def _nsa_kernel(slope_ref, qt_ref, kc_ref, vct_ref, ks_ref, vst_ref, kw_ref, vwt_ref, gt_ref, ovl_ref, exp_ref,
                prev_ref, o_ref, *, step):
    del prev_ref
    hg, dh = HEADS_PER_GROUP, NSA_HEAD_DIM
    chains = range(Q_CHAINS)
    grp = pl.program_id(1)
    t0 = [step * Q_STEP + c * Q_BLOCK for c in chains]
    t_end = (step + 1) * Q_STEP
    c1 = (dh ** -0.5) * LOG2E
    slope2 = [slope_ref[grp * hg + h] for h in range(hg)]

    qt = [jnp.concatenate([qt_ref[h * dh:(h + 1) * dh, c * Q_BLOCK:(c + 1) * Q_BLOCK] for h in range(hg)], axis=1)
          for c in chains]
    t_row = [t0[c] + lax.broadcasted_iota(jnp.int32, (1, Q_BLOCK), 1) for c in chains]

    n_cmp_pad = kc_ref.shape[3]
    n_slc = ovl_ref.shape[0]
    n_cmp = min(n_cmp_pad, _round_up((t_end - CMP_BLOCK) // CMP_STRIDE + 1, 16))
    n_cand = min(n_slc, t_end // SLC_BLOCK)
    n_blk = min(n_slc, _round_up(n_cand, 8))
    cstart = lax.broadcasted_iota(jnp.int32, (n_cmp, Q_BLOCK), 0) * CMP_STRIDE
    centre = cstart.astype(F32) + (CMP_BLOCK - 1) / 2
    jj = lax.broadcasted_iota(jnp.int32, (n_blk, Q_BLOCK), 0)
    kc, vct, ovl = kc_ref[0, 0, 0, 0:n_cmp, :], vct_ref[0, 0, 0], ovl_ref[0:n_blk, :]
    o_cmp, member = [], []
    for c in chains:
        centre_dist = t_row[c].astype(F32) - centre
        valid = (cstart + (CMP_BLOCK - 1)) <= t_row[c]
        sc = _dot(kc, qt[c])
        p_heads = []
        for h in range(hg):
            s = _head(sc, h) * c1 - slope2[h] * centre_dist
            s = jnp.where(valid, s, -jnp.inf)
            m = jnp.max(s, axis=0, keepdims=True)
            m = jnp.where(jnp.isfinite(m), m, 0.0)
            e = jnp.where(valid, jnp.exp2(s - m), 0.0)
            p = e / jnp.maximum(jnp.sum(e, axis=0, keepdims=True), F32_TINY)
            p_heads.append(_pad_rows(p.astype(BF16), n_cmp_pad))
        o_cmp.append(_dot(vct, jnp.concatenate(p_heads, axis=1)))

        imp = _dot(ovl, p_heads[0])
        for h in range(1, hg):
            imp += _dot(ovl, p_heads[h])
        blk_t = t_row[c] // SLC_BLOCK
        forced = (jj == 0) | (jj == blk_t) | (jj == blk_t - 1)
        imp = jnp.where(forced, SEL_FORCE, jnp.where(jj <= blk_t, imp, -SEL_FORCE))
        rank = jnp.zeros((n_blk, Q_BLOCK), jnp.int32)
        for i in range(n_cand):
            ri = imp[i:i + 1, :]
            before = (ri > imp) | ((ri == imp) & (jj > i))
            rank += before.astype(jnp.int32)
        member.append(_pad_rows(jnp.where(rank < min(SLC_TOP_N, n_slc), 1.0, 0.0).astype(BF16), n_slc))

    key_iota = {}

    def key_local(n):
        if n not in key_iota:
            key_iota[n] = lax.broadcasted_iota(jnp.int32, (n, Q_BLOCK), 0)
        return key_iota[n]

    def tiles(lo, hi):
        return [(p, min(KV_TILE, hi - p)) for p in range(lo, hi, KV_TILE)]

    streams = []
    for c in chains:
        streams.append([("slc", c, ks_ref, vst_ref, lo, n) for lo, n in tiles(0, t0[c] + Q_BLOCK)])
    for c in chains:
        streams.append([("win", c, kw_ref, vwt_ref, lo, n)
                        for lo, n in tiles(max(t0[c] - WINDOW, 0), t0[c] + Q_BLOCK)])
    items = [s[i] for i in range(max(len(s) for s in streams)) for s in streams if i < len(s)]

    def scores(item):
        _, c, k_ref, _, lo, n = item
        return _dot(k_ref[lo:lo + n, :], qt[c])

    def finish(state, item, sc):
        branch, c, _, vt_ref, lo, n = item
        dist = t_row[c] - (lo + key_local(n))
        masks = []
        if branch == "slc":
            masks.append(_dot(exp_ref[lo:lo + n, :], member[c]) > 0.5)
        if lo + n - 1 > t0[c]:
            masks.append(dist >= 0)
        if branch == "win" and t0[c] + Q_BLOCK - 1 - lo >= WINDOW:
            masks.append(dist < WINDOW)
        mask = functools.reduce(jnp.logical_and, masks) if masks else None
        dist_f = dist.astype(F32)
        m_new, l_new, alpha, p_t = [], [], [], []
        for h in range(hg):
            s = _head(sc, h) * c1 - slope2[h] * dist_f
            if mask is not None:
                s = jnp.where(mask, s, NEG_BIG)
            m_n = jnp.max(s, axis=0, keepdims=True)
            if state is not None:
                m_o = _head(state[0], h)
                m_n = jnp.maximum(m_o, m_n)
                a = jnp.exp2(m_o - m_n)
                alpha.append(a)
            p = jnp.exp2(s - m_n)
            l_n = jnp.sum(p, axis=0, keepdims=True)
            if state is not None:
                l_n = a * _head(state[1], h) + l_n
            m_new.append(m_n)
            l_new.append(l_n)
            p_t.append(p.astype(BF16))
        acc = _dot(vt_ref[:, lo:lo + n], jnp.concatenate(p_t, axis=1))
        if state is not None:
            acc = jnp.concatenate(alpha, axis=1) * state[2] + acc
        return jnp.concatenate(m_new, axis=1), jnp.concatenate(l_new, axis=1), acc

    states = {}
    sc_next = scores(items[0])
    for i, item in enumerate(items):
        sc = sc_next
        if i + 1 < len(items):
            sc_next = scores(items[i + 1])
        states[item[:2]] = finish(states.get(item[:2]), item, sc)

    for c in chains:
        gt = gt_ref[0, :, c * Q_BLOCK:(c + 1) * Q_BLOCK]

        def gate(br):
            return jnp.concatenate([gt[br * hg + h:br * hg + h + 1, :] for h in range(hg)], axis=1)

        (_, l_s, acc_s), (_, l_w, acc_w) = states[("slc", c)], states[("win", c)]
        o = (gate(0) * o_cmp[c] + (gate(1) / jnp.maximum(l_s, F32_TINY)) * acc_s
             + (gate(2) / jnp.maximum(l_w, F32_TINY)) * acc_w)
        o_ref[c * Q_BLOCK:(c + 1) * Q_BLOCK, :] = jnp.concatenate(
            [_head(o, h).T for h in range(hg)], axis=1).astype(o_ref.dtype)


def _nsa(qt, kv_cmp, k_slc, vt_slc, k_win, vt_win, gates_t, slopes2, ovl, expand, batch, seq):
    g = NSA_KV_GROUPS
    gw = HEADS_PER_GROUP * NSA_HEAD_DIM
    nq = seq // Q_STEP
    n_cmp_pad = kv_cmp.shape[3]
    out = jnp.zeros((batch * seq, NSA_WIDTH), BF16)
    for step in range(nq):
        per_seq = 1
        while seq // (2 * per_seq) >= (step + 1) * Q_STEP:
            per_seq *= 2
        n_keys = seq // per_seq
        k_spec = pl.BlockSpec((n_keys, NSA_HEAD_DIM), lambda b, gg, per_seq=per_seq: (b * per_seq, gg))
        vt_spec = pl.BlockSpec((NSA_HEAD_DIM, n_keys), lambda b, gg, per_seq=per_seq: (gg, b * per_seq))
        row_blk = lambda b, gg, step=step: (b * nq + step, gg)
        out = pl.pallas_call(
            functools.partial(_nsa_kernel, step=step),
            grid=(batch, g),
            in_specs=[pl.BlockSpec(memory_space=pltpu.SMEM),
                      pl.BlockSpec((gw, Q_STEP), lambda b, gg, step=step: (gg, b * nq + step)),
                      pl.BlockSpec((1, 1, 1, n_cmp_pad, NSA_HEAD_DIM), lambda b, gg: (0, b, gg, 0, 0)),
                      pl.BlockSpec((1, 1, 1, NSA_HEAD_DIM, n_cmp_pad), lambda b, gg: (1, b, gg, 0, 0)),
                      k_spec, vt_spec, k_spec, vt_spec,
                      pl.BlockSpec((1, GATE_ROWS, Q_STEP), lambda b, gg, step=step: (gg, 0, b * nq + step)),
                      pl.BlockSpec(ovl.shape, lambda b, gg: (0, 0)),
                      pl.BlockSpec(expand.shape, lambda b, gg: (0, 0)),
                      pl.BlockSpec(memory_space=pl.ANY)],
            out_specs=pl.BlockSpec((Q_STEP, gw), row_blk),
            out_shape=jax.ShapeDtypeStruct((batch * seq, NSA_WIDTH), BF16),
            input_output_aliases={11: 0},
            compiler_params=_params("parallel", "parallel"),
            name=f"nsa{step}",
        )(slopes2, qt, kv_cmp, kv_cmp, k_slc, vt_slc, k_win, vt_win, gates_t, ovl, expand, out)
    return out


def _ret_kernel(cd_ref, q_ref, k_ref, v_ref, g_ref, gain_ref, decay_ref, zeta_ref, xi_ref, o_ref, state_ref):
    @pl.when(pl.program_id(1) == 0)
    def _():
        state_ref[...] = jnp.zeros_like(state_ref)

    dk, dv = RET_KEY_DIM, RET_VAL_DIM
    for h in range(RET_HEADS):
        q = q_ref[:, h * dk:(h + 1) * dk]
        k = k_ref[:, h * dk:(h + 1) * dk]
        v = v_ref[:, h * dv:(h + 1) * dv]
        sc = _dot_nt(q.astype(BF16), k.astype(BF16)) * decay_ref[h]
        inner = _dot(sc.astype(BF16), v)
        st = state_ref[h]
        cross = _dot((q * xi_ref[h]).astype(BF16), st.astype(BF16))
        kz_t = (k * zeta_ref[h]).T.astype(BF16)
        state_ref[h] = st * cd_ref[h] + _dot(kz_t, v)
        y = inner + cross
        yc = y - jnp.mean(y, axis=-1, keepdims=True)
        yn = yc * lax.rsqrt(jnp.mean(yc * yc, axis=-1, keepdims=True) + EPS) * gain_ref[:, h * dv:(h + 1) * dv]
        o_ref[:, h * dv:(h + 1) * dv] = (_silu(g_ref[:, h * dv:(h + 1) * dv]) * yn).astype(o_ref.dtype)


def _retention(q, k, v, g, gain, consts, batch, seq):
    chunk_decay, decay, zeta, xi = consts
    c = RET_CHUNK
    nc = seq // c
    row = lambda b, ci: (b * nc + ci, 0)
    whole3 = lambda b, ci: (0, 0, 0)
    return pl.pallas_call(
        _ret_kernel,
        grid=(batch, nc),
        in_specs=[pl.BlockSpec(memory_space=pltpu.SMEM),
                  pl.BlockSpec((c, RET_QK_WIDTH), row),
                  pl.BlockSpec((c, RET_QK_WIDTH), row),
                  pl.BlockSpec((c, RET_V_WIDTH), row),
                  pl.BlockSpec((c, RET_V_WIDTH), row),
                  pl.BlockSpec((1, RET_V_WIDTH), lambda b, ci: (0, 0)),
                  pl.BlockSpec((RET_HEADS, c, c), whole3),
                  pl.BlockSpec((RET_HEADS, c, 1), whole3),
                  pl.BlockSpec((RET_HEADS, c, 1), whole3)],
        out_specs=pl.BlockSpec((c, RET_V_WIDTH), row),
        out_shape=jax.ShapeDtypeStruct((batch * seq, RET_V_WIDTH), BF16),
        scratch_shapes=[pltpu.VMEM((RET_HEADS, RET_KEY_DIM, RET_VAL_DIM), F32)],
        compiler_params=_params("parallel", "arbitrary"),
        name="retention",
    )(chunk_decay, q, k, v, g, gain, decay, zeta, xi)


def _retention_consts():
    c = RET_CHUNK
    log_gamma = jnp.log1p(-jnp.exp2(-5.0 - jnp.arange(RET_HEADS, dtype=F32)))
    n = jnp.arange(c, dtype=F32)
    diff = n[:, None] - n[None, :]
    decay = jnp.where(diff >= 0, jnp.exp(log_gamma[:, None, None] * jnp.maximum(diff, 0.0)), 0.0)
    zeta = jnp.exp(log_gamma[:, None] * (c - 1 - n)[None, :])[:, :, None]
    xi = jnp.exp(log_gamma[:, None] * (n + 1.0)[None, :])[:, :, None]
    chunk_decay = jnp.exp(log_gamma * c)
    return chunk_decay, decay, zeta, xi


def _merge_kernel(x_ref, gt_ref, on_ref, or_ref, ga_ref, gb_ref, wpn_ref, wpr_ref, wo_ref, o_ref, acc_ref):
    j = pl.program_id(1)

    @pl.when(j == 0)
    def _():
        acc_ref[...] = jnp.zeros_like(acc_ref)

    a = _dot(on_ref[...], wpn_ref[...])
    b = _dot(or_ref[...], wpr_ref[...])
    merged = jax.nn.sigmoid(ga_ref[...]) * a + jax.nn.sigmoid(gb_ref[...]) * b
    acc_ref[...] += _dot(merged.astype(BF16), wo_ref[...])

    @pl.when(j == pl.num_programs(1) - 1)
    def _():
        o_ref[...] = x_ref[...] + gt_ref[0] * acc_ref[...]


def _merge(x2, gate, o_nsa, o_ret, ga, gb, wpn, wpr, wo, seq):
    t, d = x2.shape
    tm, tn = 512, 512
    per_b = seq // tm
    return pl.pallas_call(
        _merge_kernel,
        grid=(t // tm, d // tn),
        in_specs=[pl.BlockSpec((tm, d), lambda i, j: (i, 0)),
                  pl.BlockSpec((1, 1, d), lambda i, j: (i // per_b, 0, 0)),
                  pl.BlockSpec((tm, o_nsa.shape[1]), lambda i, j: (i, 0)),
                  pl.BlockSpec((tm, o_ret.shape[1]), lambda i, j: (i, 0)),
                  pl.BlockSpec((tm, tn), lambda i, j: (i, j)),
                  pl.BlockSpec((tm, tn), lambda i, j: (i, j)),
                  pl.BlockSpec((wpn.shape[0], tn), lambda i, j: (0, j)),
                  pl.BlockSpec((wpr.shape[0], tn), lambda i, j: (0, j)),
                  pl.BlockSpec((tn, d), lambda i, j: (j, 0))],
        out_specs=pl.BlockSpec((tm, d), lambda i, j: (i, 0)),
        out_shape=jax.ShapeDtypeStruct((t, d), F32),
        scratch_shapes=[pltpu.VMEM((tm, d), F32)],
        compiler_params=_params("parallel", "arbitrary"),
        name="merge",
    )(x2, gate, o_nsa, o_ret, ga, gb, wpn, wpr, wo)


def _nsa_tables(seq):
    n_cmp = (seq - CMP_BLOCK) // CMP_STRIDE + 1
    n_cmp_pad = seq // CMP_STRIDE
    n_slc = seq // SLC_BLOCK
    cs = (np.arange(n_cmp) * CMP_STRIDE)[:, None]
    js = (np.arange(n_slc) * SLC_BLOCK)[None, :]
    overlap = np.clip(np.minimum(cs + CMP_BLOCK, js + SLC_BLOCK) - np.maximum(cs, js), 0, None) / CMP_BLOCK
    ovl = np.zeros((n_slc, n_cmp_pad), np.float32)
    ovl[:, :n_cmp] = overlap.T
    expand = (np.arange(seq)[:, None] // SLC_BLOCK == np.arange(n_slc)[None, :]).astype(np.float32)
    slopes2 = jnp.exp2(-8.0 * jnp.arange(1, NSA_HEADS + 1, dtype=F32) / NSA_HEADS) * LOG2E
    return jnp.asarray(ovl, BF16), jnp.asarray(expand, BF16), slopes2


def kernel(x, c, w_ada, b_ada, g_norm, w_ffn_gate, w_ffn_up, w_ffn_down, w_in, g_qk, cmp_pos, cmp_w1, cmp_b1,
           cmp_w2, ret_gn_gain, w_proj_nsa, w_proj_ret, w_out):
    batch, seq, d = x.shape
    depth = w_ada.shape[0]
    t = batch * seq
    x2 = x.reshape(t, d)
    c_pad = jnp.pad(c, ((0, 8 - batch), (0, 0)))
    ovl, expand, slopes2 = _nsa_tables(seq)
    ret_consts = _retention_consts()
    hg = HEADS_PER_GROUP

    for l in range(depth):
        ada = _ada(c_pad, w_ada[l], b_ada[l][None, :])[:batch].reshape(batch, N_ADA, 1, d)
        sh1, sc1, gt1, sh2, sc2, gt2, sh3, sc3, gt3 = [ada[:, i] for i in range(N_ADA)]

        x2 = _ffn(x2, sh1, sc1, gt1, g_norm[l, 0][None, :], w_ffn_gate[l], w_ffn_up[l], w_ffn_down[l], 0, seq)

        u = _mod(x2, sh2, sc2, g_norm[l, 1][None, :], seq)
        w = w_in[l]
        off = [0]

        def take(width):
            w_part = w[:, off[0]:off[0] + width]
            off[0] += width
            return w_part.astype(BF16)

        gq = g_qk[l]
        qt_nsa = _proj_t(u, take(NSA_WIDTH), BF16, gq[0], name="proj_q")
        cmp_raw = _proj(u, take(2 * KV_WIDTH), F32, name="proj_cmp_raw")
        k_slc = _proj(u, take(KV_WIDTH), BF16, "headnorm", gq[2][None, :], name="proj_k_slc")
        vt_slc = _proj_t(u, take(KV_WIDTH), BF16, name="proj_v_slc")
        k_win = _proj(u, take(KV_WIDTH), BF16, "headnorm", gq[3][None, :], name="proj_k_win")
        vt_win = _proj_t(u, take(KV_WIDTH), BF16, name="proj_v_win")
        w_gl = take(3 * NSA_HEADS).reshape(d, 3, NSA_KV_GROUPS, hg).transpose(0, 2, 1, 3)
        w_gl = jnp.pad(w_gl.reshape(d, NSA_KV_GROUPS, 3 * hg),
                       ((0, 0), (0, 0), (0, LANES - 3 * hg))).reshape(d, NSA_KV_GROUPS * LANES)
        gates = _proj(u, w_gl, F32, "sigmoid", name="proj_gates")
        gates_t = gates.reshape(t, NSA_KV_GROUPS, LANES)[:, :, :GATE_ROWS].transpose(1, 2, 0)
        q_r = _proj(u, take(RET_QK_WIDTH), F32, name="proj_q_ret")
        k_r = _proj(u, take(RET_QK_WIDTH), F32, "keyscale", name="proj_k_ret")
        v_r = _proj(u, take(RET_V_WIDTH), BF16, name="proj_v_ret")
        g_r = _proj(u, take(RET_V_WIDTH), F32, name="proj_g_ret")
        ga = _proj(u, take(d), F32, name="proj_ga")
        gb = _proj(u, take(d), F32, name="proj_gb")

        kv_cmp = _compress(cmp_raw.reshape(batch, seq, 2 * KV_WIDTH), cmp_pos[l], cmp_w1[l].astype(BF16),
                           cmp_b1[l][:, None, :], cmp_w2[l].astype(BF16), gq[1][None, :])
        o_nsa = _nsa(qt_nsa, kv_cmp, k_slc, vt_slc, k_win, vt_win, gates_t, slopes2, ovl, expand, batch, seq)
        o_ret = _retention(q_r, k_r, v_r, g_r, ret_gn_gain[l].reshape(1, RET_V_WIDTH), ret_consts, batch, seq)

        x2 = _merge(x2, gt2, o_nsa, o_ret, ga, gb, w_proj_nsa[l].astype(BF16), w_proj_ret[l].astype(BF16),
                    w_out[l].astype(BF16), seq)

        x2 = _ffn(x2, sh3, sc3, gt3, g_norm[l, 2][None, :], w_ffn_gate[l], w_ffn_up[l], w_ffn_down[l], 1, seq)

    return x2.reshape(batch, seq, d)
```

```python
import functools
import math

import numpy as np
import jax
import jax.numpy as jnp
from jax import lax
from jax.experimental import pallas as pl
from jax.experimental.pallas import tpu as pltpu

F32 = jnp.float32
BF16 = jnp.bfloat16

D_MODEL = 2048
NSA_HEADS = 16
NSA_KV_GROUPS = 4
HEADS_PER_GROUP = NSA_HEADS // NSA_KV_GROUPS
NSA_HEAD_DIM = 128
CMP_BLOCK = 32
CMP_STRIDE = 16
SLC_BLOCK = 64
SLC_TOP_N = 16
WINDOW = 512
RET_HEADS = 8
RET_KEY_DIM = 128
RET_VAL_DIM = 256
RET_CHUNK = 128
D_FF = 5632
N_ADA = 9
EPS = 1e-6
SEL_FORCE = 1e4

NSA_WIDTH = NSA_HEADS * NSA_HEAD_DIM
KV_WIDTH = NSA_KV_GROUPS * NSA_HEAD_DIM
RET_QK_WIDTH = RET_HEADS * RET_KEY_DIM
RET_V_WIDTH = RET_HEADS * RET_VAL_DIM

LANES = 128
VMEM_LIMIT_BYTES = 56 * 1024 * 1024
NEG_BIG = -1e30
F32_TINY = float(np.finfo(np.float32).tiny)
LOG2E = math.log2(math.e)

Q_BLOCK = 128
Q_CHAINS = 2
Q_STEP = Q_CHAINS * Q_BLOCK
KV_TILE = 1024
GATE_ROWS = 16


def _params(*sem):
    return pltpu.CompilerParams(dimension_semantics=sem, vmem_limit_bytes=VMEM_LIMIT_BYTES)


def _silu(x):
    return x * jax.nn.sigmoid(x)


def _rms(x, g):
    return x * lax.rsqrt(jnp.mean(x * x, axis=-1, keepdims=True) + EPS) * g


def _modulate(x, g, shift, scale):
    return _rms(x, g) * (1.0 + scale) + shift


def _dot(a, b):
    return jnp.dot(a, b, preferred_element_type=F32)


def _dot_nt(a, b):
    return lax.dot_general(a, b, (((1,), (1,)), ((), ())), preferred_element_type=F32)


def _ada_kernel(c_ref, w_ref, b_ref, o_ref):
    cond = _silu(c_ref[...]).astype(BF16)
    o_ref[...] = _dot(cond, w_ref[...].astype(BF16)) + b_ref[...]


def _ada(c_pad, w, b):
    rows, d = c_pad.shape
    n = w.shape[1]
    tn = 1024
    return pl.pallas_call(
        _ada_kernel,
        grid=(n // tn,),
        in_specs=[pl.BlockSpec((rows, d), lambda j: (0, 0)),
                  pl.BlockSpec((d, tn), lambda j: (0, j)),
                  pl.BlockSpec((1, tn), lambda j: (0, j))],
        out_specs=pl.BlockSpec((rows, tn), lambda j: (0, j)),
        out_shape=jax.ShapeDtypeStruct((rows, n), F32),
        compiler_params=_params("parallel"),
        name="ada",
    )(c_pad, w, b)


def _ffn_kernel(x_ref, sh_ref, sc_ref, gt_ref, g_ref, wg_ref, wu_ref, wd_ref, o_ref, h_ref):
    j = pl.program_id(1)

    @pl.when(j == 0)
    def _():
        h_ref[...] = _modulate(x_ref[...], g_ref[...], sh_ref[0], sc_ref[0]).astype(BF16)
        o_ref[...] = jnp.zeros_like(o_ref)

    h = h_ref[...]
    a = _dot(h, wg_ref[...].astype(BF16))
    b = _dot(h, wu_ref[...].astype(BF16))
    act = (_silu(a) * b).astype(BF16)
    o_ref[...] += _dot(act, wd_ref[...].astype(BF16))

    @pl.when(j == pl.num_programs(1) - 1)
    def _():
        o_ref[...] = x_ref[...] + (0.5 * gt_ref[0]) * o_ref[...]


def _ffn(x2, shift, scale, gate, g, wg, wu, wd, which, seq):
    t, d = x2.shape
    ff = wg.shape[2]
    tm, tf = 1024, 256
    per_b = seq // tm
    mod_spec = pl.BlockSpec((1, 1, d), lambda i, j: (i // per_b, 0, 0))
    return pl.pallas_call(
        _ffn_kernel,
        grid=(t // tm, ff // tf),
        in_specs=[pl.BlockSpec((tm, d), lambda i, j: (i, 0), pipeline_mode=pl.Buffered(1)),
                  mod_spec, mod_spec, mod_spec,
                  pl.BlockSpec((1, d), lambda i, j: (0, 0)),
                  pl.BlockSpec((None, d, tf), lambda i, j: (which, 0, j)),
                  pl.BlockSpec((None, d, tf), lambda i, j: (which, 0, j)),
                  pl.BlockSpec((None, tf, d), lambda i, j: (which, j, 0))],
        out_specs=pl.BlockSpec((tm, d), lambda i, j: (i, 0)),
        out_shape=jax.ShapeDtypeStruct((t, d), F32),
        scratch_shapes=[pltpu.VMEM((tm, d), BF16)],
        compiler_params=_params("parallel", "arbitrary"),
        name="ffn",
    )(x2, shift, scale, gate, g, wg, wu, wd)


def _mod_kernel(x_ref, sh_ref, sc_ref, g_ref, o_ref):
    o_ref[...] = _modulate(x_ref[...], g_ref[...], sh_ref[0], sc_ref[0]).astype(BF16)


def _mod(x2, shift, scale, g, seq):
    t, d = x2.shape
    tm = 512
    per_b = seq // tm
    mod_spec = pl.BlockSpec((1, 1, d), lambda i: (i // per_b, 0, 0))
    return pl.pallas_call(
        _mod_kernel,
        grid=(t // tm,),
        in_specs=[pl.BlockSpec((tm, d), lambda i: (i, 0)), mod_spec, mod_spec,
                  pl.BlockSpec((1, d), lambda i: (0, 0))],
        out_specs=pl.BlockSpec((tm, d), lambda i: (i, 0)),
        out_shape=jax.ShapeDtypeStruct((t, d), BF16),
        compiler_params=_params("parallel"),
        name="modulate",
    )(x2, shift, scale, g)


def _proj_kernel(u_ref, w_ref, gain_ref, o_ref, *, epilogue):
    acc = _dot(u_ref[...], w_ref[...])
    if epilogue == "headnorm":
        g = gain_ref[...]
        parts = [_rms(acc[:, k:k + NSA_HEAD_DIM], g) for k in range(0, acc.shape[1], NSA_HEAD_DIM)]
        acc = jnp.concatenate(parts, axis=1)
    elif epilogue == "keyscale":
        acc = acc * (RET_KEY_DIM ** -0.5)
    elif epilogue == "sigmoid":
        acc = jax.nn.sigmoid(acc)
    o_ref[...] = acc.astype(o_ref.dtype)


def _proj(u, w, out_dtype, epilogue="plain", gain=None, name="proj"):
    t, d = u.shape
    n = w.shape[1]
    tm = 1024
    tn = 1024 if n % 1024 == 0 else 512
    if gain is None:
        gain = jnp.ones((1, NSA_HEAD_DIM), F32)
    return pl.pallas_call(
        functools.partial(_proj_kernel, epilogue=epilogue),
        grid=(t // tm, n // tn),
        in_specs=[pl.BlockSpec((tm, d), lambda i, j: (i, 0)),
                  pl.BlockSpec((d, tn), lambda i, j: (0, j)),
                  pl.BlockSpec((1, NSA_HEAD_DIM), lambda i, j: (0, 0))],
        out_specs=pl.BlockSpec((tm, tn), lambda i, j: (i, j)),
        out_shape=jax.ShapeDtypeStruct((t, n), out_dtype),
        compiler_params=_params("parallel", "parallel"),
        name=name,
    )(u, w, gain)


def _proj_t_kernel(u_ref, w_ref, gain_ref, o_ref, *, headnorm):
    acc = lax.dot_general(w_ref[...], u_ref[...], (((0,), (1,)), ((), ())), preferred_element_type=F32)
    if headnorm:
        g = gain_ref[...]
        parts = []
        for k in range(0, acc.shape[0], NSA_HEAD_DIM):
            xh = acc[k:k + NSA_HEAD_DIM]
            parts.append(xh * lax.rsqrt(jnp.mean(xh * xh, axis=0, keepdims=True) + EPS) * g)
        acc = jnp.concatenate(parts, axis=0)
    o_ref[...] = acc.astype(o_ref.dtype)


def _proj_t(u, w, out_dtype, gain=None, name="proj_t"):
    t, d = u.shape
    n = w.shape[1]
    tm = 1024
    tn = 1024 if n % 1024 == 0 else 512
    headnorm = gain is not None
    if gain is None:
        gain = jnp.ones((NSA_HEAD_DIM,), F32)
    return pl.pallas_call(
        functools.partial(_proj_t_kernel, headnorm=headnorm),
        grid=(t // tm, n // tn),
        in_specs=[pl.BlockSpec((tm, d), lambda i, j: (i, 0)),
                  pl.BlockSpec((d, tn), lambda i, j: (0, j)),
                  pl.BlockSpec((NSA_HEAD_DIM, 1), lambda i, j: (0, 0))],
        out_specs=pl.BlockSpec((tn, tm), lambda i, j: (j, i)),
        out_shape=jax.ShapeDtypeStruct((n, t), out_dtype),
        compiler_params=_params("parallel", "parallel"),
        name=name,
    )(u, w, gain[:, None])


def _cmp_kernel(z_ref, pos_ref, w1_ref, b1_ref, w2_ref, gk_ref, o_ref):
    half = CMP_BLOCK // 2
    n_rows = z_ref.shape[1] // CMP_STRIDE
    y_lo = jnp.zeros((n_rows, NSA_HEAD_DIM), F32)
    y_hi = jnp.zeros((n_rows, NSA_HEAD_DIM), F32)
    for l in range(half):
        zl = z_ref[0, pl.ds(l, n_rows, stride=CMP_STRIDE), :]
        lo = (zl + pos_ref[0, l:l + 1, :]).astype(BF16)
        hi = (zl + pos_ref[0, half + l:half + l + 1, :]).astype(BF16)
        y_lo += _dot(lo, w1_ref[0, l * NSA_HEAD_DIM:(l + 1) * NSA_HEAD_DIM, :])
        y_hi += _dot(hi, w1_ref[0, (half + l) * NSA_HEAD_DIM:(half + l + 1) * NSA_HEAD_DIM, :])
    y = y_lo + pltpu.roll(y_hi, n_rows - 1, 0)
    hdn = jax.nn.gelu(y + b1_ref[0]).astype(BF16)
    out = _dot(hdn, w2_ref[0])
    normed = _rms(out, gk_ref[...])
    is_key = pl.program_id(0) == 0
    o_ref[0, 0, 0] = jnp.where(is_key, normed, out.T).astype(o_ref.dtype)


def _compress(raw, pos, w1, b1, w2, gk):
    b, s, _ = raw.shape
    g = NSA_KV_GROUPS
    n_rows = s // CMP_STRIDE
    assert n_rows == NSA_HEAD_DIM, "key / transposed-value tiles share one square output block"
    return pl.pallas_call(
        _cmp_kernel,
        grid=(2, b, g),
        in_specs=[pl.BlockSpec((1, s, NSA_HEAD_DIM), lambda i, bb, gg: (bb, 0, i * g + gg)),
                  pl.BlockSpec((1, CMP_BLOCK, NSA_HEAD_DIM), lambda i, bb, gg: (i, 0, 0)),
                  pl.BlockSpec((1, CMP_BLOCK * NSA_HEAD_DIM, NSA_HEAD_DIM), lambda i, bb, gg: (i, 0, 0)),
                  pl.BlockSpec((1, 1, NSA_HEAD_DIM), lambda i, bb, gg: (i, 0, 0)),
                  pl.BlockSpec((1, NSA_HEAD_DIM, NSA_HEAD_DIM), lambda i, bb, gg: (i, 0, 0)),
                  pl.BlockSpec((1, NSA_HEAD_DIM), lambda i, bb, gg: (0, 0))],
        out_specs=pl.BlockSpec((1, 1, 1, n_rows, NSA_HEAD_DIM), lambda i, bb, gg: (i, bb, gg, 0, 0)),
        out_shape=jax.ShapeDtypeStruct((2, b, g, n_rows, NSA_HEAD_DIM), BF16),
        compiler_params=_params("parallel", "parallel", "parallel"),
        name="compress",
    )(raw, pos, w1, b1, w2, gk)


def _head(x, h):
    return x[:, h * Q_BLOCK:(h + 1) * Q_BLOCK]


def _round_up(x, m):
    return (x + m - 1) // m * m


def _pad_rows(x, n):
    if x.shape[0] == n:
        return x
    return jnp.concatenate([x, jnp.zeros((n - x.shape[0], x.shape[1]), x.dtype)], axis=0)


def _nsa_kernel(slope_ref, qt_ref, kc_ref, vct_ref, ks_ref, vst_ref, kw_ref, vwt_ref, gt_ref, ovl_ref, exp_ref,
                prev_ref, o_ref, *, step):
    del prev_ref
    hg, dh = HEADS_PER_GROUP, NSA_HEAD_DIM
    chains = range(Q_CHAINS)
    grp = pl.program_id(1)
    t0 = [step * Q_STEP + c * Q_BLOCK for c in chains]
    t_end = (step + 1) * Q_STEP
    c1 = (dh ** -0.5) * LOG2E
    slope2 = [slope_ref[grp * hg + h] for h in range(hg)]

    qt = [jnp.concatenate([qt_ref[h * dh:(h + 1) * dh, c * Q_BLOCK:(c + 1) * Q_BLOCK] for h in range(hg)], axis=1)
          for c in chains]
    t_row = [t0[c] + lax.broadcasted_iota(jnp.int32, (1, Q_BLOCK), 1) for c in chains]

    n_cmp_pad = kc_ref.shape[3]
    n_slc = ovl_ref.shape[0]
    n_cmp = min(n_cmp_pad, _round_up((t_end - CMP_BLOCK) // CMP_STRIDE + 1, 16))
    n_cand = min(n_slc, t_end // SLC_BLOCK)
    n_blk = min(n_slc, _round_up(n_cand, 8))
    cstart = lax.broadcasted_iota(jnp.int32, (n_cmp, Q_BLOCK), 0) * CMP_STRIDE
    centre = cstart.astype(F32) + (CMP_BLOCK - 1) / 2
    jj = lax.broadcasted_iota(jnp.int32, (n_blk, Q_BLOCK), 0)
    kc, vct, ovl = kc_ref[0, 0, 0, 0:n_cmp, :], vct_ref[0, 0, 0], ovl_ref[0:n_blk, :]
    o_cmp, member = [], []
    for c in chains:
        centre_dist = t_row[c].astype(F32) - centre
        valid = (cstart + (CMP_BLOCK - 1)) <= t_row[c]
        sc = _dot(kc, qt[c])
        p_heads = []
        for h in range(hg):
            s = _head(sc, h) * c1 - slope2[h] * centre_dist
            s = jnp.where(valid, s, -jnp.inf)
            m = jnp.max(s, axis=0, keepdims=True)
            m = jnp.where(jnp.isfinite(m), m, 0.0)
            e = jnp.where(valid, jnp.exp2(s - m), 0.0)
            p = e / jnp.maximum(jnp.sum(e, axis=0, keepdims=True), F32_TINY)
            p_heads.append(_pad_rows(p.astype(BF16), n_cmp_pad))
        o_cmp.append(_dot(vct, jnp.concatenate(p_heads, axis=1)))

        imp = _dot(ovl, p_heads[0])
        for h in range(1, hg):
            imp += _dot(ovl, p_heads[h])
        blk_t = t_row[c] // SLC_BLOCK
        forced = (jj == 0) | (jj == blk_t) | (jj == blk_t - 1)
        imp = jnp.where(forced, SEL_FORCE, jnp.where(jj <= blk_t, imp, -SEL_FORCE))
        rank = jnp.zeros((n_blk, Q_BLOCK), jnp.int32)
        for i in range(n_cand):
            ri = imp[i:i + 1, :]
            before = (ri > imp) | ((ri == imp) & (jj > i))
            rank += before.astype(jnp.int32)
        member.append(_pad_rows(jnp.where(rank < min(SLC_TOP_N, n_slc), 1.0, 0.0).astype(BF16), n_slc))

    key_iota = {}

    def key_local(n):
        if n not in key_iota:
            key_iota[n] = lax.broadcasted_iota(jnp.int32, (n, Q_BLOCK), 0)
        return key_iota[n]

    def tiles(lo, hi):
        return [(p, min(KV_TILE, hi - p)) for p in range(lo, hi, KV_TILE)]

    streams = []
    for c in chains:
        streams.append([("slc", c, ks_ref, vst_ref, lo, n) for lo, n in tiles(0, t0[c] + Q_BLOCK)])
    for c in chains:
        streams.append([("win", c, kw_ref, vwt_ref, lo, n)
                        for lo, n in tiles(max(t0[c] - WINDOW, 0), t0[c] + Q_BLOCK)])
    items = [s[i] for i in range(max(len(s) for s in streams)) for s in streams if i < len(s)]

    def scores(item):
        _, c, k_ref, _, lo, n = item
        return _dot(k_ref[lo:lo + n, :], qt[c])

    def finish(state, item, sc):
        branch, c, _, vt_ref, lo, n = item
        dist = t_row[c] - (lo + key_local(n))
        masks = []
        if branch == "slc":
            masks.append(_dot(exp_ref[lo:lo + n, :], member[c]) > 0.5)
        if lo + n - 1 > t0[c]:
            masks.append(dist >= 0)
        if branch == "win" and t0[c] + Q_BLOCK - 1 - lo >= WINDOW:
            masks.append(dist < WINDOW)
        mask = functools.reduce(jnp.logical_and, masks) if masks else None
        dist_f = dist.astype(F32)
        m_new, l_new, alpha, p_t = [], [], [], []
        for h in range(hg):
            s = _head(sc, h) * c1 - slope2[h] * dist_f
            if mask is not None:
                s = jnp.where(mask, s, NEG_BIG)
            m_n = jnp.max(s, axis=0, keepdims=True)
            if state is not None:
                m_o = _head(state[0], h)
                m_n = jnp.maximum(m_o, m_n)
                a = jnp.exp2(m_o - m_n)
                alpha.append(a)
            p = jnp.exp2(s - m_n)
            l_n = jnp.sum(p, axis=0, keepdims=True)
            if state is not None:
                l_n = a * _head(state[1], h) + l_n
            m_new.append(m_n)
            l_new.append(l_n)
            p_t.append(p.astype(BF16))
        acc = _dot(vt_ref[:, lo:lo + n], jnp.concatenate(p_t, axis=1))
        if state is not None:
            acc = jnp.concatenate(alpha, axis=1) * state[2] + acc
        return jnp.concatenate(m_new, axis=1), jnp.concatenate(l_new, axis=1), acc

    states = {}
    sc_next = scores(items[0])
    for i, item in enumerate(items):
        sc = sc_next
        if i + 1 < len(items):
            sc_next = scores(items[i + 1])
        states[item[:2]] = finish(states.get(item[:2]), item, sc)

    for c in chains:
        gt = gt_ref[0, :, c * Q_BLOCK:(c + 1) * Q_BLOCK]

        def gate(br):
            return jnp.concatenate([gt[br * hg + h:br * hg + h + 1, :] for h in range(hg)], axis=1)

        (_, l_s, acc_s), (_, l_w, acc_w) = states[("slc", c)], states[("win", c)]
        o = (gate(0) * o_cmp[c] + (gate(1) / jnp.maximum(l_s, F32_TINY)) * acc_s
             + (gate(2) / jnp.maximum(l_w, F32_TINY)) * acc_w)
        o_ref[c * Q_BLOCK:(c + 1) * Q_BLOCK, :] = jnp.concatenate(
            [_head(o, h).T for h in range(hg)], axis=1).astype(o_ref.dtype)


def _nsa(qt, kv_cmp, k_slc, vt_slc, k_win, vt_win, gates_t, slopes2, ovl, expand, batch, seq):
    g = NSA_KV_GROUPS
    gw = HEADS_PER_GROUP * NSA_HEAD_DIM
    nq = seq // Q_STEP
    n_cmp_pad = kv_cmp.shape[3]
    out = jnp.zeros((batch * seq, NSA_WIDTH), BF16)
    for step in range(nq):
        per_seq = 1
        while seq // (2 * per_seq) >= (step + 1) * Q_STEP:
            per_seq *= 2
        n_keys = seq // per_seq
        k_spec = pl.BlockSpec((n_keys, NSA_HEAD_DIM), lambda b, gg, per_seq=per_seq: (b * per_seq, gg))
        vt_spec = pl.BlockSpec((NSA_HEAD_DIM, n_keys), lambda b, gg, per_seq=per_seq: (gg, b * per_seq))
        row_blk = lambda b, gg, step=step: (b * nq + step, gg)
        out = pl.pallas_call(
            functools.partial(_nsa_kernel, step=step),
            grid=(batch, g),
            in_specs=[pl.BlockSpec(memory_space=pltpu.SMEM),
                      pl.BlockSpec((gw, Q_STEP), lambda b, gg, step=step: (gg, b * nq + step)),
                      pl.BlockSpec((1, 1, 1, n_cmp_pad, NSA_HEAD_DIM), lambda b, gg: (0, b, gg, 0, 0)),
                      pl.BlockSpec((1, 1, 1, NSA_HEAD_DIM, n_cmp_pad), lambda b, gg: (1, b, gg, 0, 0)),
                      k_spec, vt_spec, k_spec, vt_spec,
                      pl.BlockSpec((1, GATE_ROWS, Q_STEP), lambda b, gg, step=step: (gg, 0, b * nq + step)),
                      pl.BlockSpec(ovl.shape, lambda b, gg: (0, 0)),
                      pl.BlockSpec(expand.shape, lambda b, gg: (0, 0)),
                      pl.BlockSpec(memory_space=pl.ANY)],
            out_specs=pl.BlockSpec((Q_STEP, gw), row_blk),
            out_shape=jax.ShapeDtypeStruct((batch * seq, NSA_WIDTH), BF16),
            input_output_aliases={11: 0},
            compiler_params=_params("parallel", "parallel"),
            name=f"nsa{step}",
        )(slopes2, qt, kv_cmp, kv_cmp, k_slc, vt_slc, k_win, vt_win, gates_t, ovl, expand, out)
    return out


def _ret_kernel(cd_ref, q_ref, k_ref, v_ref, g_ref, gain_ref, decay_ref, zeta_ref, xi_ref, o_ref, state_ref):
    @pl.when(pl.program_id(1) == 0)
    def _():
        state_ref[...] = jnp.zeros_like(state_ref)

    dk, dv = RET_KEY_DIM, RET_VAL_DIM
    for h in range(RET_HEADS):
        q = q_ref[:, h * dk:(h + 1) * dk]
        k = k_ref[:, h * dk:(h + 1) * dk]
        v = v_ref[:, h * dv:(h + 1) * dv]
        sc = _dot_nt(q.astype(BF16), k.astype(BF16)) * decay_ref[h]
        inner = _dot(sc.astype(BF16), v)
        st = state_ref[h]
        cross = _dot((q * xi_ref[h]).astype(BF16), st.astype(BF16))
        kz_t = (k * zeta_ref[h]).T.astype(BF16)
        state_ref[h] = st * cd_ref[h] + _dot(kz_t, v)
        y = inner + cross
        yc = y - jnp.mean(y, axis=-1, keepdims=True)
        yn = yc * lax.rsqrt(jnp.mean(yc * yc, axis=-1, keepdims=True) + EPS) * gain_ref[:, h * dv:(h + 1) * dv]
        o_ref[:, h * dv:(h + 1) * dv] = (_silu(g_ref[:, h * dv:(h + 1) * dv]) * yn).astype(o_ref.dtype)


def _retention(q, k, v, g, gain, consts, batch, seq):
    chunk_decay, decay, zeta, xi = consts
    c = RET_CHUNK
    nc = seq // c
    row = lambda b, ci: (b * nc + ci, 0)
    whole3 = lambda b, ci: (0, 0, 0)
    return pl.pallas_call(
        _ret_kernel,
        grid=(batch, nc),
        in_specs=[pl.BlockSpec(memory_space=pltpu.SMEM),
                  pl.BlockSpec((c, RET_QK_WIDTH), row),
                  pl.BlockSpec((c, RET_QK_WIDTH), row),
                  pl.BlockSpec((c, RET_V_WIDTH), row),
                  pl.BlockSpec((c, RET_V_WIDTH), row),
                  pl.BlockSpec((1, RET_V_WIDTH), lambda b, ci: (0, 0)),
                  pl.BlockSpec((RET_HEADS, c, c), whole3),
                  pl.BlockSpec((RET_HEADS, c, 1), whole3),
                  pl.BlockSpec((RET_HEADS, c, 1), whole3)],
        out_specs=pl.BlockSpec((c, RET_V_WIDTH), row),
        out_shape=jax.ShapeDtypeStruct((batch * seq, RET_V_WIDTH), BF16),
        scratch_shapes=[pltpu.VMEM((RET_HEADS, RET_KEY_DIM, RET_VAL_DIM), F32)],
        compiler_params=_params("parallel", "arbitrary"),
        name="retention",
    )(chunk_decay, q, k, v, g, gain, decay, zeta, xi)


def _retention_consts():
    c = RET_CHUNK
    log_gamma = jnp.log1p(-jnp.exp2(-5.0 - jnp.arange(RET_HEADS, dtype=F32)))
    n = jnp.arange(c, dtype=F32)
    diff = n[:, None] - n[None, :]
    decay = jnp.where(diff >= 0, jnp.exp(log_gamma[:, None, None] * jnp.maximum(diff, 0.0)), 0.0)
    zeta = jnp.exp(log_gamma[:, None] * (c - 1 - n)[None, :])[:, :, None]
    xi = jnp.exp(log_gamma[:, None] * (n + 1.0)[None, :])[:, :, None]
    chunk_decay = jnp.exp(log_gamma * c)
    return chunk_decay, decay, zeta, xi


def _merge_kernel(x_ref, gt_ref, on_ref, or_ref, ga_ref, gb_ref, wpn_ref, wpr_ref, wo_ref, o_ref, acc_ref):
    j = pl.program_id(1)

    @pl.when(j == 0)
    def _():
        acc_ref[...] = jnp.zeros_like(acc_ref)

    a = _dot(on_ref[...], wpn_ref[...])
    b = _dot(or_ref[...], wpr_ref[...])
    merged = jax.nn.sigmoid(ga_ref[...]) * a + jax.nn.sigmoid(gb_ref[...]) * b
    acc_ref[...] += _dot(merged.astype(BF16), wo_ref[...])

    @pl.when(j == pl.num_programs(1) - 1)
    def _():
        o_ref[...] = x_ref[...] + gt_ref[0] * acc_ref[...]


def _merge(x2, gate, o_nsa, o_ret, ga, gb, wpn, wpr, wo, seq):
    t, d = x2.shape
    tm, tn = 512, 512
    per_b = seq // tm
    return pl.pallas_call(
        _merge_kernel,
        grid=(t // tm, d // tn),
        in_specs=[pl.BlockSpec((tm, d), lambda i, j: (i, 0)),
                  pl.BlockSpec((1, 1, d), lambda i, j: (i // per_b, 0, 0)),
                  pl.BlockSpec((tm, o_nsa.shape[1]), lambda i, j: (i, 0)),
                  pl.BlockSpec((tm, o_ret.shape[1]), lambda i, j: (i, 0)),
                  pl.BlockSpec((tm, tn), lambda i, j: (i, j)),
                  pl.BlockSpec((tm, tn), lambda i, j: (i, j)),
                  pl.BlockSpec((wpn.shape[0], tn), lambda i, j: (0, j)),
                  pl.BlockSpec((wpr.shape[0], tn), lambda i, j: (0, j)),
                  pl.BlockSpec((tn, d), lambda i, j: (j, 0))],
        out_specs=pl.BlockSpec((tm, d), lambda i, j: (i, 0)),
        out_shape=jax.ShapeDtypeStruct((t, d), F32),
        scratch_shapes=[pltpu.VMEM((tm, d), F32)],
        compiler_params=_params("parallel", "arbitrary"),
        name="merge",
    )(x2, gate, o_nsa, o_ret, ga, gb, wpn, wpr, wo)


def _nsa_tables(seq):
    n_cmp = (seq - CMP_BLOCK) // CMP_STRIDE + 1
    n_cmp_pad = seq // CMP_STRIDE
    n_slc = seq // SLC_BLOCK
    cs = (np.arange(n_cmp) * CMP_STRIDE)[:, None]
    js = (np.arange(n_slc) * SLC_BLOCK)[None, :]
    overlap = np.clip(np.minimum(cs + CMP_BLOCK, js + SLC_BLOCK) - np.maximum(cs, js), 0, None) / CMP_BLOCK
    ovl = np.zeros((n_slc, n_cmp_pad), np.float32)
    ovl[:, :n_cmp] = overlap.T
    expand = (np.arange(seq)[:, None] // SLC_BLOCK == np.arange(n_slc)[None, :]).astype(np.float32)
    slopes2 = jnp.exp2(-8.0 * jnp.arange(1, NSA_HEADS + 1, dtype=F32) / NSA_HEADS) * LOG2E
    return jnp.asarray(ovl, BF16), jnp.asarray(expand, BF16), slopes2


def kernel(x, c, w_ada, b_ada, g_norm, w_ffn_gate, w_ffn_up, w_ffn_down, w_in, g_qk, cmp_pos, cmp_w1, cmp_b1,
           cmp_w2, ret_gn_gain, w_proj_nsa, w_proj_ret, w_out):
    batch, seq, d = x.shape
    depth = w_ada.shape[0]
    t = batch * seq
    x2 = x.reshape(t, d)
    c_pad = jnp.pad(c, ((0, 8 - batch), (0, 0)))
    ovl, expand, slopes2 = _nsa_tables(seq)
    ret_consts = _retention_consts()
    hg = HEADS_PER_GROUP

    for l in range(depth):
        ada = _ada(c_pad, w_ada[l], b_ada[l][None, :])[:batch].reshape(batch, N_ADA, 1, d)
        sh1, sc1, gt1, sh2, sc2, gt2, sh3, sc3, gt3 = [ada[:, i] for i in range(N_ADA)]

        x2 = _ffn(x2, sh1, sc1, gt1, g_norm[l, 0][None, :], w_ffn_gate[l], w_ffn_up[l], w_ffn_down[l], 0, seq)

        u = _mod(x2, sh2, sc2, g_norm[l, 1][None, :], seq)
        w = w_in[l]
        off = [0]

        def take(width):
            w_part = w[:, off[0]:off[0] + width]
            off[0] += width
            return w_part.astype(BF16)

        gq = g_qk[l]
        qt_nsa = _proj_t(u, take(NSA_WIDTH), BF16, gq[0], name="proj_q")
        cmp_raw = _proj(u, take(2 * KV_WIDTH), F32, name="proj_cmp_raw")
        k_slc = _proj(u, take(KV_WIDTH), BF16, "headnorm", gq[2][None, :], name="proj_k_slc")
        vt_slc = _proj_t(u, take(KV_WIDTH), BF16, name="proj_v_slc")
        k_win = _proj(u, take(KV_WIDTH), BF16, "headnorm", gq[3][None, :], name="proj_k_win")
        vt_win = _proj_t(u, take(KV_WIDTH), BF16, name="proj_v_win")
        w_gl = take(3 * NSA_HEADS).reshape(d, 3, NSA_KV_GROUPS, hg).transpose(0, 2, 1, 3)
        w_gl = jnp.pad(w_gl.reshape(d, NSA_KV_GROUPS, 3 * hg),
                       ((0, 0), (0, 0), (0, LANES - 3 * hg))).reshape(d, NSA_KV_GROUPS * LANES)
        gates = _proj(u, w_gl, F32, "sigmoid", name="proj_gates")
        gates_t = gates.reshape(t, NSA_KV_GROUPS, LANES)[:, :, :GATE_ROWS].transpose(1, 2, 0)
        q_r = _proj(u, take(RET_QK_WIDTH), F32, name="proj_q_ret")
        k_r = _proj(u, take(RET_QK_WIDTH), F32, "keyscale", name="proj_k_ret")
        v_r = _proj(u, take(RET_V_WIDTH), BF16, name="proj_v_ret")
        g_r = _proj(u, take(RET_V_WIDTH), F32, name="proj_g_ret")
        ga = _proj(u, take(d), F32, name="proj_ga")
        gb = _proj(u, take(d), F32, name="proj_gb")

        kv_cmp = _compress(cmp_raw.reshape(batch, seq, 2 * KV_WIDTH), cmp_pos[l], cmp_w1[l].astype(BF16),
                           cmp_b1[l][:, None, :], cmp_w2[l].astype(BF16), gq[1][None, :])
        o_nsa = _nsa(qt_nsa, kv_cmp, k_slc, vt_slc, k_win, vt_win, gates_t, slopes2, ovl, expand, batch, seq)
        o_ret = _retention(q_r, k_r, v_r, g_r, ret_gn_gain[l].reshape(1, RET_V_WIDTH), ret_consts, batch, seq)

        x2 = _merge(x2, gt2, o_nsa, o_ret, ga, gb, w_proj_nsa[l].astype(BF16), w_proj_ret[l].astype(BF16),
                    w_out[l].astype(BF16), seq)

        x2 = _ffn(x2, sh3, sc3, gt3, g_norm[l, 2][None, :], w_ffn_gate[l], w_ffn_up[l], w_ffn_down[l], 1, seq)

    return x2.reshape(batch, seq, d)
```

```python
import functools
import math

import numpy as np
import jax
import jax.numpy as jnp
from jax import lax
from jax.experimental import pallas as pl
from jax.experimental.pallas import tpu as pltpu

F32 = jnp.float32
BF16 = jnp.bfloat16

D_MODEL = 2048
NSA_HEADS = 16
NSA_KV_GROUPS = 4
HEADS_PER_GROUP = NSA_HEADS // NSA_KV_GROUPS
NSA_HEAD_DIM = 128
CMP_BLOCK = 32
CMP_STRIDE = 16
SLC_BLOCK = 64
SLC_TOP_N = 16
WINDOW = 512
RET_HEADS = 8
RET_KEY_DIM = 128
RET_VAL_DIM = 256
RET_CHUNK = 128
D_FF = 5632
N_ADA = 9
EPS = 1e-6
SEL_FORCE = 1e4

NSA_WIDTH = NSA_HEADS * NSA_HEAD_DIM
KV_WIDTH = NSA_KV_GROUPS * NSA_HEAD_DIM
RET_QK_WIDTH = RET_HEADS * RET_KEY_DIM
RET_V_WIDTH = RET_HEADS * RET_VAL_DIM

LANES = 128
VMEM_LIMIT_BYTES = 56 * 1024 * 1024
NEG_BIG = -1e30
F32_TINY = float(np.finfo(np.float32).tiny)
LOG2E = math.log2(math.e)

Q_BLOCK = 128
Q_CHAINS = 4
Q_STEP = Q_CHAINS * Q_BLOCK
KV_TILE = 1024
GATE_ROWS = 16


def _params(*sem):
    return pltpu.CompilerParams(dimension_semantics=sem, vmem_limit_bytes=VMEM_LIMIT_BYTES)


def _silu(x):
    return x * jax.nn.sigmoid(x)


def _rms(x, g):
    return x * lax.rsqrt(jnp.mean(x * x, axis=-1, keepdims=True) + EPS) * g


def _modulate(x, g, shift, scale):
    return _rms(x, g) * (1.0 + scale) + shift


def _dot(a, b):
    return jnp.dot(a, b, preferred_element_type=F32)


def _dot_nt(a, b):
    return lax.dot_general(a, b, (((1,), (1,)), ((), ())), preferred_element_type=F32)


def _ada_kernel(c_ref, w_ref, b_ref, o_ref):
    cond = _silu(c_ref[...]).astype(BF16)
    o_ref[...] = _dot(cond, w_ref[...].astype(BF16)) + b_ref[...]


def _ada(c_pad, w, b):
    rows, d = c_pad.shape
    n = w.shape[1]
    tn = 1024
    return pl.pallas_call(
        _ada_kernel,
        grid=(n // tn,),
        in_specs=[pl.BlockSpec((rows, d), lambda j: (0, 0)),
                  pl.BlockSpec((d, tn), lambda j: (0, j)),
                  pl.BlockSpec((1, tn), lambda j: (0, j))],
        out_specs=pl.BlockSpec((rows, tn), lambda j: (0, j)),
        out_shape=jax.ShapeDtypeStruct((rows, n), F32),
        compiler_params=_params("parallel"),
        name="ada",
    )(c_pad, w, b)


def _ffn_kernel(x_ref, sh_ref, sc_ref, gt_ref, g_ref, wg_ref, wu_ref, wd_ref, o_ref, h_ref):
    j = pl.program_id(1)

    @pl.when(j == 0)
    def _():
        h_ref[...] = _modulate(x_ref[...], g_ref[...], sh_ref[0], sc_ref[0]).astype(BF16)
        o_ref[...] = jnp.zeros_like(o_ref)

    h = h_ref[...]
    a = _dot(h, wg_ref[...].astype(BF16))
    b = _dot(h, wu_ref[...].astype(BF16))
    act = (_silu(a) * b).astype(BF16)
    o_ref[...] += _dot(act, wd_ref[...].astype(BF16))

    @pl.when(j == pl.num_programs(1) - 1)
    def _():
        o_ref[...] = x_ref[...] + (0.5 * gt_ref[0]) * o_ref[...]


def _ffn(x2, shift, scale, gate, g, wg, wu, wd, which, seq):
    t, d = x2.shape
    ff = wg.shape[2]
    tm, tf = 1024, 256
    per_b = seq // tm
    mod_spec = pl.BlockSpec((1, 1, d), lambda i, j: (i // per_b, 0, 0))
    return pl.pallas_call(
        _ffn_kernel,
        grid=(t // tm, ff // tf),
        in_specs=[pl.BlockSpec((tm, d), lambda i, j: (i, 0), pipeline_mode=pl.Buffered(1)),
                  mod_spec, mod_spec, mod_spec,
                  pl.BlockSpec((1, d), lambda i, j: (0, 0)),
                  pl.BlockSpec((None, d, tf), lambda i, j: (which, 0, j)),
                  pl.BlockSpec((None, d, tf), lambda i, j: (which, 0, j)),
                  pl.BlockSpec((None, tf, d), lambda i, j: (which, j, 0))],
        out_specs=pl.BlockSpec((tm, d), lambda i, j: (i, 0)),
        out_shape=jax.ShapeDtypeStruct((t, d), F32),
        scratch_shapes=[pltpu.VMEM((tm, d), BF16)],
        compiler_params=_params("parallel", "arbitrary"),
        name="ffn",
    )(x2, shift, scale, gate, g, wg, wu, wd)


def _mod_kernel(x_ref, sh_ref, sc_ref, g_ref, o_ref):
    o_ref[...] = _modulate(x_ref[...], g_ref[...], sh_ref[0], sc_ref[0]).astype(BF16)


def _mod(x2, shift, scale, g, seq):
    t, d = x2.shape
    tm = 512
    per_b = seq // tm
    mod_spec = pl.BlockSpec((1, 1, d), lambda i: (i // per_b, 0, 0))
    return pl.pallas_call(
        _mod_kernel,
        grid=(t // tm,),
        in_specs=[pl.BlockSpec((tm, d), lambda i: (i, 0)), mod_spec, mod_spec,
                  pl.BlockSpec((1, d), lambda i: (0, 0))],
        out_specs=pl.BlockSpec((tm, d), lambda i: (i, 0)),
        out_shape=jax.ShapeDtypeStruct((t, d), BF16),
        compiler_params=_params("parallel"),
        name="modulate",
    )(x2, shift, scale, g)


def _proj_kernel(u_ref, w_ref, gain_ref, o_ref, *, epilogue):
    acc = _dot(u_ref[...], w_ref[...])
    if epilogue == "headnorm":
        g = gain_ref[...]
        parts = [_rms(acc[:, k:k + NSA_HEAD_DIM], g) for k in range(0, acc.shape[1], NSA_HEAD_DIM)]
        acc = jnp.concatenate(parts, axis=1)
    elif epilogue == "keyscale":
        acc = acc * (RET_KEY_DIM ** -0.5)
    elif epilogue == "sigmoid":
        acc = jax.nn.sigmoid(acc)
    o_ref[...] = acc.astype(o_ref.dtype)


def _proj(u, w, out_dtype, epilogue="plain", gain=None, name="proj"):
    t, d = u.shape
    n = w.shape[1]
    tm = 1024
    tn = 1024 if n % 1024 == 0 else 512
    if gain is None:
        gain = jnp.ones((1, NSA_HEAD_DIM), F32)
    return pl.pallas_call(
        functools.partial(_proj_kernel, epilogue=epilogue),
        grid=(t // tm, n // tn),
        in_specs=[pl.BlockSpec((tm, d), lambda i, j: (i, 0)),
                  pl.BlockSpec((d, tn), lambda i, j: (0, j)),
                  pl.BlockSpec((1, NSA_HEAD_DIM), lambda i, j: (0, 0))],
        out_specs=pl.BlockSpec((tm, tn), lambda i, j: (i, j)),
        out_shape=jax.ShapeDtypeStruct((t, n), out_dtype),
        compiler_params=_params("parallel", "parallel"),
        name=name,
    )(u, w, gain)


def _proj_t_kernel(u_ref, w_ref, gain_ref, o_ref, *, headnorm):
    acc = lax.dot_general(w_ref[...], u_ref[...], (((0,), (1,)), ((), ())), preferred_element_type=F32)
    if headnorm:
        g = gain_ref[...]
        parts = []
        for k in range(0, acc.shape[0], NSA_HEAD_DIM):
            xh = acc[k:k + NSA_HEAD_DIM]
            parts.append(xh * lax.rsqrt(jnp.mean(xh * xh, axis=0, keepdims=True) + EPS) * g)
        acc = jnp.concatenate(parts, axis=0)
    o_ref[...] = acc.astype(o_ref.dtype)


def _proj_t(u, w, out_dtype, gain=None, name="proj_t"):
    t, d = u.shape
    n = w.shape[1]
    tm = 1024
    tn = 1024 if n % 1024 == 0 else 512
    headnorm = gain is not None
    if gain is None:
        gain = jnp.ones((NSA_HEAD_DIM,), F32)
    return pl.pallas_call(
        functools.partial(_proj_t_kernel, headnorm=headnorm),
        grid=(t // tm, n // tn),
        in_specs=[pl.BlockSpec((tm, d), lambda i, j: (i, 0)),
                  pl.BlockSpec((d, tn), lambda i, j: (0, j)),
                  pl.BlockSpec((NSA_HEAD_DIM, 1), lambda i, j: (0, 0))],
        out_specs=pl.BlockSpec((tn, tm), lambda i, j: (j, i)),
        out_shape=jax.ShapeDtypeStruct((n, t), out_dtype),
        compiler_params=_params("parallel", "parallel"),
        name=name,
    )(u, w, gain[:, None])


def _cmp_kernel(z_ref, pos_ref, w1_ref, b1_ref, w2_ref, gk_ref, o_ref):
    half = CMP_BLOCK // 2
    n_rows = z_ref.shape[1] // CMP_STRIDE
    y_lo = jnp.zeros((n_rows, NSA_HEAD_DIM), F32)
    y_hi = jnp.zeros((n_rows, NSA_HEAD_DIM), F32)
    for l in range(half):
        zl = z_ref[0, pl.ds(l, n_rows, stride=CMP_STRIDE), :]
        lo = (zl + pos_ref[0, l:l + 1, :]).astype(BF16)
        hi = (zl + pos_ref[0, half + l:half + l + 1, :]).astype(BF16)
        y_lo += _dot(lo, w1_ref[0, l * NSA_HEAD_DIM:(l + 1) * NSA_HEAD_DIM, :])
        y_hi += _dot(hi, w1_ref[0, (half + l) * NSA_HEAD_DIM:(half + l + 1) * NSA_HEAD_DIM, :])
    y = y_lo + pltpu.roll(y_hi, n_rows - 1, 0)
    hdn = jax.nn.gelu(y + b1_ref[0]).astype(BF16)
    out = _dot(hdn, w2_ref[0])
    normed = _rms(out, gk_ref[...])
    is_key = pl.program_id(0) == 0
    o_ref[0, 0, 0] = jnp.where(is_key, normed, out.T).astype(o_ref.dtype)


def _compress(raw, pos, w1, b1, w2, gk):
    b, s, _ = raw.shape
    g = NSA_KV_GROUPS
    n_rows = s // CMP_STRIDE
    assert n_rows == NSA_HEAD_DIM, "key / transposed-value tiles share one square output block"
    return pl.pallas_call(
        _cmp_kernel,
        grid=(2, b, g),
        in_specs=[pl.BlockSpec((1, s, NSA_HEAD_DIM), lambda i, bb, gg: (bb, 0, i * g + gg)),
                  pl.BlockSpec((1, CMP_BLOCK, NSA_HEAD_DIM), lambda i, bb, gg: (i, 0, 0)),
                  pl.BlockSpec((1, CMP_BLOCK * NSA_HEAD_DIM, NSA_HEAD_DIM), lambda i, bb, gg: (i, 0, 0)),
                  pl.BlockSpec((1, 1, NSA_HEAD_DIM), lambda i, bb, gg: (i, 0, 0)),
                  pl.BlockSpec((1, NSA_HEAD_DIM, NSA_HEAD_DIM), lambda i, bb, gg: (i, 0, 0)),
                  pl.BlockSpec((1, NSA_HEAD_DIM), lambda i, bb, gg: (0, 0))],
        out_specs=pl.BlockSpec((1, 1, 1, n_rows, NSA_HEAD_DIM), lambda i, bb, gg: (i, bb, gg, 0, 0)),
        out_shape=jax.ShapeDtypeStruct((2, b, g, n_rows, NSA_HEAD_DIM), BF16),
        compiler_params=_params("parallel", "parallel", "parallel"),
        name="compress",
    )(raw, pos, w1, b1, w2, gk)


def _head(x, h):
    return x[:, h * Q_BLOCK:(h + 1) * Q_BLOCK]


def _round_up(x, m):
    return (x + m - 1) // m * m


def _pad_rows(x, n):
    if x.shape[0] == n:
        return x
    return jnp.concatenate([x, jnp.zeros((n - x.shape[0], x.shape[1]), x.dtype)], axis=0)


def _nsa_kernel(slope_ref, qt_ref, kc_ref, vct_ref, ks_ref, vst_ref, kw_ref, vwt_ref, gt_ref, ovl_ref, exp_ref,
                prev_ref, o_ref, *, step):
    del prev_ref
    hg, dh = HEADS_PER_GROUP, NSA_HEAD_DIM
    chains = range(Q_CHAINS)
    grp = pl.program_id(1)
    t0 = [step * Q_STEP + c * Q_BLOCK for c in chains]
    t_end = (step + 1) * Q_STEP
    c1 = (dh ** -0.5) * LOG2E
    slope2 = [slope_ref[grp * hg + h] for h in range(hg)]

    qt = [jnp.concatenate([qt_ref[h * dh:(h + 1) * dh, c * Q_BLOCK:(c + 1) * Q_BLOCK] for h in range(hg)], axis=1)
          for c in chains]
    t_row = [t0[c] + lax.broadcasted_iota(jnp.int32, (1, Q_BLOCK), 1) for c in chains]

    n_cmp_pad = kc_ref.shape[3]
    n_slc = ovl_ref.shape[0]
    n_cmp = min(n_cmp_pad, _round_up((t_end - CMP_BLOCK) // CMP_STRIDE + 1, 16))
    n_cand = min(n_slc, t_end // SLC_BLOCK)
    n_blk = min(n_slc, _round_up(n_cand, 8))
    cstart = lax.broadcasted_iota(jnp.int32, (n_cmp, Q_BLOCK), 0) * CMP_STRIDE
    centre = cstart.astype(F32) + (CMP_BLOCK - 1) / 2
    jj = lax.broadcasted_iota(jnp.int32, (n_blk, Q_BLOCK), 0)
    kc, vct, ovl = kc_ref[0, 0, 0, 0:n_cmp, :], vct_ref[0, 0, 0], ovl_ref[0:n_blk, :]
    o_cmp, member = [], []
    for c in chains:
        centre_dist = t_row[c].astype(F32) - centre
        valid = (cstart + (CMP_BLOCK - 1)) <= t_row[c]
        sc = _dot(kc, qt[c])
        p_heads = []
        for h in range(hg):
            s = _head(sc, h) * c1 - slope2[h] * centre_dist
            s = jnp.where(valid, s, -jnp.inf)
            m = jnp.max(s, axis=0, keepdims=True)
            m = jnp.where(jnp.isfinite(m), m, 0.0)
            e = jnp.where(valid, jnp.exp2(s - m), 0.0)
            p = e / jnp.maximum(jnp.sum(e, axis=0, keepdims=True), F32_TINY)
            p_heads.append(_pad_rows(p.astype(BF16), n_cmp_pad))
        o_cmp.append(_dot(vct, jnp.concatenate(p_heads, axis=1)))

        imp = _dot(ovl, p_heads[0])
        for h in range(1, hg):
            imp += _dot(ovl, p_heads[h])
        blk_t = t_row[c] // SLC_BLOCK
        forced = (jj == 0) | (jj == blk_t) | (jj == blk_t - 1)
        imp = jnp.where(forced, SEL_FORCE, jnp.where(jj <= blk_t, imp, -SEL_FORCE))
        rank = jnp.zeros((n_blk, Q_BLOCK), jnp.int32)
        for i in range(n_cand):
            ri = imp[i:i + 1, :]
            before = (ri > imp) | ((ri == imp) & (jj > i))
            rank += before.astype(jnp.int32)
        member.append(_pad_rows(jnp.where(rank < min(SLC_TOP_N, n_slc), 1.0, 0.0).astype(BF16), n_slc))

    key_iota = {}

    def key_local(n):
        if n not in key_iota:
            key_iota[n] = lax.broadcasted_iota(jnp.int32, (n, Q_BLOCK), 0)
        return key_iota[n]

    def tiles(lo, hi):
        return [(p, min(KV_TILE, hi - p)) for p in range(lo, hi, KV_TILE)]

    streams = []
    for c in chains:
        streams.append([("slc", c, ks_ref, vst_ref, lo, n) for lo, n in tiles(0, t0[c] + Q_BLOCK)])
    for c in chains:
        streams.append([("win", c, kw_ref, vwt_ref, lo, n)
                        for lo, n in tiles(max(t0[c] - WINDOW, 0), t0[c] + Q_BLOCK)])
    items = [s[i] for i in range(max(len(s) for s in streams)) for s in streams if i < len(s)]

    def scores(item):
        _, c, k_ref, _, lo, n = item
        return _dot(k_ref[lo:lo + n, :], qt[c])

    def finish(state, item, sc):
        branch, c, _, vt_ref, lo, n = item
        dist = t_row[c] - (lo + key_local(n))
        masks = []
        if branch == "slc":
            masks.append(_dot(exp_ref[lo:lo + n, :], member[c]) > 0.5)
        if lo + n - 1 > t0[c]:
            masks.append(dist >= 0)
        if branch == "win" and t0[c] + Q_BLOCK - 1 - lo >= WINDOW:
            masks.append(dist < WINDOW)
        mask = functools.reduce(jnp.logical_and, masks) if masks else None
        dist_f = dist.astype(F32)
        m_new, l_new, alpha, p_t = [], [], [], []
        for h in range(hg):
            s = _head(sc, h) * c1 - slope2[h] * dist_f
            if mask is not None:
                s = jnp.where(mask, s, NEG_BIG)
            m_n = jnp.max(s, axis=0, keepdims=True)
            if state is not None:
                m_o = _head(state[0], h)
                m_n = jnp.maximum(m_o, m_n)
                a = jnp.exp2(m_o - m_n)
                alpha.append(a)
            p = jnp.exp2(s - m_n)
            l_n = jnp.sum(p, axis=0, keepdims=True)
            if state is not None:
                l_n = a * _head(state[1], h) + l_n
            m_new.append(m_n)
            l_new.append(l_n)
            p_t.append(p.astype(BF16))
        acc = _dot(vt_ref[:, lo:lo + n], jnp.concatenate(p_t, axis=1))
        if state is not None:
            acc = jnp.concatenate(alpha, axis=1) * state[2] + acc
        return jnp.concatenate(m_new, axis=1), jnp.concatenate(l_new, axis=1), acc

    states = {}
    sc_next = scores(items[0])
    for i, item in enumerate(items):
        sc = sc_next
        if i + 1 < len(items):
            sc_next = scores(items[i + 1])
        states[item[:2]] = finish(states.get(item[:2]), item, sc)

    for c in chains:
        gt = gt_ref[0, :, c * Q_BLOCK:(c + 1) * Q_BLOCK]

        def gate(br):
            return jnp.concatenate([gt[br * hg + h:br * hg + h + 1, :] for h in range(hg)], axis=1)

        (_, l_s, acc_s), (_, l_w, acc_w) = states[("slc", c)], states[("win", c)]
        o = (gate(0) * o_cmp[c] + (gate(1) / jnp.maximum(l_s, F32_TINY)) * acc_s
             + (gate(2) / jnp.maximum(l_w, F32_TINY)) * acc_w)
        o_ref[c * Q_BLOCK:(c + 1) * Q_BLOCK, :] = jnp.concatenate(
            [_head(o, h).T for h in range(hg)], axis=1).astype(o_ref.dtype)


def _nsa(qt, kv_cmp, k_slc, vt_slc, k_win, vt_win, gates_t, slopes2, ovl, expand, batch, seq):
    g = NSA_KV_GROUPS
    gw = HEADS_PER_GROUP * NSA_HEAD_DIM
    nq = seq // Q_STEP
    n_cmp_pad = kv_cmp.shape[3]
    out = jnp.zeros((batch * seq, NSA_WIDTH), BF16)
    for step in range(nq):
        per_seq = 1
        while seq // (2 * per_seq) >= (step + 1) * Q_STEP:
            per_seq *= 2
        n_keys = seq // per_seq
        k_spec = pl.BlockSpec((n_keys, NSA_HEAD_DIM), lambda b, gg, per_seq=per_seq: (b * per_seq, gg))
        vt_spec = pl.BlockSpec((NSA_HEAD_DIM, n_keys), lambda b, gg, per_seq=per_seq: (gg, b * per_seq))
        row_blk = lambda b, gg, step=step: (b * nq + step, gg)
        out = pl.pallas_call(
            functools.partial(_nsa_kernel, step=step),
            grid=(batch, g),
            in_specs=[pl.BlockSpec(memory_space=pltpu.SMEM),
                      pl.BlockSpec((gw, Q_STEP), lambda b, gg, step=step: (gg, b * nq + step)),
                      pl.BlockSpec((1, 1, 1, n_cmp_pad, NSA_HEAD_DIM), lambda b, gg: (0, b, gg, 0, 0)),
                      pl.BlockSpec((1, 1, 1, NSA_HEAD_DIM, n_cmp_pad), lambda b, gg: (1, b, gg, 0, 0)),
                      k_spec, vt_spec, k_spec, vt_spec,
                      pl.BlockSpec((1, GATE_ROWS, Q_STEP), lambda b, gg, step=step: (gg, 0, b * nq + step)),
                      pl.BlockSpec(ovl.shape, lambda b, gg: (0, 0)),
                      pl.BlockSpec(expand.shape, lambda b, gg: (0, 0)),
                      pl.BlockSpec(memory_space=pl.ANY)],
            out_specs=pl.BlockSpec((Q_STEP, gw), row_blk),
            out_shape=jax.ShapeDtypeStruct((batch * seq, NSA_WIDTH), BF16),
            input_output_aliases={11: 0},
            compiler_params=_params("parallel", "parallel"),
            name=f"nsa{step}",
        )(slopes2, qt, kv_cmp, kv_cmp, k_slc, vt_slc, k_win, vt_win, gates_t, ovl, expand, out)
    return out


def _ret_kernel(cd_ref, q_ref, k_ref, v_ref, g_ref, gain_ref, decay_ref, zeta_ref, xi_ref, o_ref, state_ref):
    @pl.when(pl.program_id(1) == 0)
    def _():
        state_ref[...] = jnp.zeros_like(state_ref)

    dk, dv = RET_KEY_DIM, RET_VAL_DIM
    for h in range(RET_HEADS):
        q = q_ref[:, h * dk:(h + 1) * dk]
        k = k_ref[:, h * dk:(h + 1) * dk]
        v = v_ref[:, h * dv:(h + 1) * dv]
        sc = _dot_nt(q.astype(BF16), k.astype(BF16)) * decay_ref[h]
        inner = _dot(sc.astype(BF16), v)
        st = state_ref[h]
        cross = _dot((q * xi_ref[h]).astype(BF16), st.astype(BF16))
        kz_t = (k * zeta_ref[h]).T.astype(BF16)
        state_ref[h] = st * cd_ref[h] + _dot(kz_t, v)
        y = inner + cross
        yc = y - jnp.mean(y, axis=-1, keepdims=True)
        yn = yc * lax.rsqrt(jnp.mean(yc * yc, axis=-1, keepdims=True) + EPS) * gain_ref[:, h * dv:(h + 1) * dv]
        o_ref[:, h * dv:(h + 1) * dv] = (_silu(g_ref[:, h * dv:(h + 1) * dv]) * yn).astype(o_ref.dtype)


def _retention(q, k, v, g, gain, consts, batch, seq):
    chunk_decay, decay, zeta, xi = consts
    c = RET_CHUNK
    nc = seq // c
    row = lambda b, ci: (b * nc + ci, 0)
    whole3 = lambda b, ci: (0, 0, 0)
    return pl.pallas_call(
        _ret_kernel,
        grid=(batch, nc),
        in_specs=[pl.BlockSpec(memory_space=pltpu.SMEM),
                  pl.BlockSpec((c, RET_QK_WIDTH), row),
                  pl.BlockSpec((c, RET_QK_WIDTH), row),
                  pl.BlockSpec((c, RET_V_WIDTH), row),
                  pl.BlockSpec((c, RET_V_WIDTH), row),
                  pl.BlockSpec((1, RET_V_WIDTH), lambda b, ci: (0, 0)),
                  pl.BlockSpec((RET_HEADS, c, c), whole3),
                  pl.BlockSpec((RET_HEADS, c, 1), whole3),
                  pl.BlockSpec((RET_HEADS, c, 1), whole3)],
        out_specs=pl.BlockSpec((c, RET_V_WIDTH), row),
        out_shape=jax.ShapeDtypeStruct((batch * seq, RET_V_WIDTH), BF16),
        scratch_shapes=[pltpu.VMEM((RET_HEADS, RET_KEY_DIM, RET_VAL_DIM), F32)],
        compiler_params=_params("parallel", "arbitrary"),
        name="retention",
    )(chunk_decay, q, k, v, g, gain, decay, zeta, xi)


def _retention_consts():
    c = RET_CHUNK
    log_gamma = jnp.log1p(-jnp.exp2(-5.0 - jnp.arange(RET_HEADS, dtype=F32)))
    n = jnp.arange(c, dtype=F32)
    diff = n[:, None] - n[None, :]
    decay = jnp.where(diff >= 0, jnp.exp(log_gamma[:, None, None] * jnp.maximum(diff, 0.0)), 0.0)
    zeta = jnp.exp(log_gamma[:, None] * (c - 1 - n)[None, :])[:, :, None]
    xi = jnp.exp(log_gamma[:, None] * (n + 1.0)[None, :])[:, :, None]
    chunk_decay = jnp.exp(log_gamma * c)
    return chunk_decay, decay, zeta, xi


def _merge_kernel(x_ref, gt_ref, on_ref, or_ref, ga_ref, gb_ref, wpn_ref, wpr_ref, wo_ref, o_ref, acc_ref):
    j = pl.program_id(1)

    @pl.when(j == 0)
    def _():
        acc_ref[...] = jnp.zeros_like(acc_ref)

    a = _dot(on_ref[...], wpn_ref[...])
    b = _dot(or_ref[...], wpr_ref[...])
    merged = jax.nn.sigmoid(ga_ref[...]) * a + jax.nn.sigmoid(gb_ref[...]) * b
    acc_ref[...] += _dot(merged.astype(BF16), wo_ref[...])

    @pl.when(j == pl.num_programs(1) - 1)
    def _():
        o_ref[...] = x_ref[...] + gt_ref[0] * acc_ref[...]


def _merge(x2, gate, o_nsa, o_ret, ga, gb, wpn, wpr, wo, seq):
    t, d = x2.shape
    tm, tn = 512, 512
    per_b = seq // tm
    return pl.pallas_call(
        _merge_kernel,
        grid=(t // tm, d // tn),
        in_specs=[pl.BlockSpec((tm, d), lambda i, j: (i, 0)),
                  pl.BlockSpec((1, 1, d), lambda i, j: (i // per_b, 0, 0)),
                  pl.BlockSpec((tm, o_nsa.shape[1]), lambda i, j: (i, 0)),
                  pl.BlockSpec((tm, o_ret.shape[1]), lambda i, j: (i, 0)),
                  pl.BlockSpec((tm, tn), lambda i, j: (i, j)),
                  pl.BlockSpec((tm, tn), lambda i, j: (i, j)),
                  pl.BlockSpec((wpn.shape[0], tn), lambda i, j: (0, j)),
                  pl.BlockSpec((wpr.shape[0], tn), lambda i, j: (0, j)),
                  pl.BlockSpec((tn, d), lambda i, j: (j, 0))],
        out_specs=pl.BlockSpec((tm, d), lambda i, j: (i, 0)),
        out_shape=jax.ShapeDtypeStruct((t, d), F32),
        scratch_shapes=[pltpu.VMEM((tm, d), F32)],
        compiler_params=_params("parallel", "arbitrary"),
        name="merge",
    )(x2, gate, o_nsa, o_ret, ga, gb, wpn, wpr, wo)


def _nsa_tables(seq):
    n_cmp = (seq - CMP_BLOCK) // CMP_STRIDE + 1
    n_cmp_pad = seq // CMP_STRIDE
    n_slc = seq // SLC_BLOCK
    cs = (np.arange(n_cmp) * CMP_STRIDE)[:, None]
    js = (np.arange(n_slc) * SLC_BLOCK)[None, :]
    overlap = np.clip(np.minimum(cs + CMP_BLOCK, js + SLC_BLOCK) - np.maximum(cs, js), 0, None) / CMP_BLOCK
    ovl = np.zeros((n_slc, n_cmp_pad), np.float32)
    ovl[:, :n_cmp] = overlap.T
    expand = (np.arange(seq)[:, None] // SLC_BLOCK == np.arange(n_slc)[None, :]).astype(np.float32)
    slopes2 = jnp.exp2(-8.0 * jnp.arange(1, NSA_HEADS + 1, dtype=F32) / NSA_HEADS) * LOG2E
    return jnp.asarray(ovl, BF16), jnp.asarray(expand, BF16), slopes2


def kernel(x, c, w_ada, b_ada, g_norm, w_ffn_gate, w_ffn_up, w_ffn_down, w_in, g_qk, cmp_pos, cmp_w1, cmp_b1,
           cmp_w2, ret_gn_gain, w_proj_nsa, w_proj_ret, w_out):
    batch, seq, d = x.shape
    depth = w_ada.shape[0]
    t = batch * seq
    x2 = x.reshape(t, d)
    c_pad = jnp.pad(c, ((0, 8 - batch), (0, 0)))
    ovl, expand, slopes2 = _nsa_tables(seq)
    ret_consts = _retention_consts()
    hg = HEADS_PER_GROUP

    for l in range(depth):
        ada = _ada(c_pad, w_ada[l], b_ada[l][None, :])[:batch].reshape(batch, N_ADA, 1, d)
        sh1, sc1, gt1, sh2, sc2, gt2, sh3, sc3, gt3 = [ada[:, i] for i in range(N_ADA)]

        x2 = _ffn(x2, sh1, sc1, gt1, g_norm[l, 0][None, :], w_ffn_gate[l], w_ffn_up[l], w_ffn_down[l], 0, seq)

        u = _mod(x2, sh2, sc2, g_norm[l, 1][None, :], seq)
        w = w_in[l]
        off = [0]

        def take(width):
            w_part = w[:, off[0]:off[0] + width]
            off[0] += width
            return w_part.astype(BF16)

        gq = g_qk[l]
        qt_nsa = _proj_t(u, take(NSA_WIDTH), BF16, gq[0], name="proj_q")
        cmp_raw = _proj(u, take(2 * KV_WIDTH), F32, name="proj_cmp_raw")
        k_slc = _proj(u, take(KV_WIDTH), BF16, "headnorm", gq[2][None, :], name="proj_k_slc")
        vt_slc = _proj_t(u, take(KV_WIDTH), BF16, name="proj_v_slc")
        k_win = _proj(u, take(KV_WIDTH), BF16, "headnorm", gq[3][None, :], name="proj_k_win")
        vt_win = _proj_t(u, take(KV_WIDTH), BF16, name="proj_v_win")
        w_gl = take(3 * NSA_HEADS).reshape(d, 3, NSA_KV_GROUPS, hg).transpose(0, 2, 1, 3)
        w_gl = jnp.pad(w_gl.reshape(d, NSA_KV_GROUPS, 3 * hg),
                       ((0, 0), (0, 0), (0, LANES - 3 * hg))).reshape(d, NSA_KV_GROUPS * LANES)
        gates = _proj(u, w_gl, F32, "sigmoid", name="proj_gates")
        gates_t = gates.reshape(t, NSA_KV_GROUPS, LANES)[:, :, :GATE_ROWS].transpose(1, 2, 0)
        q_r = _proj(u, take(RET_QK_WIDTH), F32, name="proj_q_ret")
        k_r = _proj(u, take(RET_QK_WIDTH), F32, "keyscale", name="proj_k_ret")
        v_r = _proj(u, take(RET_V_WIDTH), BF16, name="proj_v_ret")
        g_r = _proj(u, take(RET_V_WIDTH), F32, name="proj_g_ret")
        ga = _proj(u, take(d), F32, name="proj_ga")
        gb = _proj(u, take(d), F32, name="proj_gb")

        kv_cmp = _compress(cmp_raw.reshape(batch, seq, 2 * KV_WIDTH), cmp_pos[l], cmp_w1[l].astype(BF16),
                           cmp_b1[l][:, None, :], cmp_w2[l].astype(BF16), gq[1][None, :])
        o_nsa = _nsa(qt_nsa, kv_cmp, k_slc, vt_slc, k_win, vt_win, gates_t, slopes2, ovl, expand, batch, seq)
        o_ret = _retention(q_r, k_r, v_r, g_r, ret_gn_gain[l].reshape(1, RET_V_WIDTH), ret_consts, batch, seq)

        x2 = _merge(x2, gt2, o_nsa, o_ret, ga, gb, w_proj_nsa[l].astype(BF16), w_proj_ret[l].astype(BF16),
                    w_out[l].astype(BF16), seq)

        x2 = _ffn(x2, sh3, sc3, gt3, g_norm[l, 2][None, :], w_ffn_gate[l], w_ffn_up[l], w_ffn_down[l], 1, seq)

    return x2.reshape(batch, seq, d)
```

```python
import functools
import math

import numpy as np
import jax
import jax.numpy as jnp
from jax import lax
from jax.experimental import pallas as pl
from jax.experimental.pallas import tpu as pltpu

F32 = jnp.float32
BF16 = jnp.bfloat16

D_MODEL = 2048
NSA_HEADS = 16
NSA_KV_GROUPS = 4
HEADS_PER_GROUP = NSA_HEADS // NSA_KV_GROUPS
NSA_HEAD_DIM = 128
CMP_BLOCK = 32
CMP_STRIDE = 16
SLC_BLOCK = 64
SLC_TOP_N = 16
WINDOW = 512
RET_HEADS = 8
RET_KEY_DIM = 128
RET_VAL_DIM = 256
RET_CHUNK = 128
D_FF = 5632
N_ADA = 9
EPS = 1e-6
SEL_FORCE = 1e4

NSA_WIDTH = NSA_HEADS * NSA_HEAD_DIM
KV_WIDTH = NSA_KV_GROUPS * NSA_HEAD_DIM
RET_QK_WIDTH = RET_HEADS * RET_KEY_DIM
RET_V_WIDTH = RET_HEADS * RET_VAL_DIM

LANES = 128
VMEM_LIMIT_BYTES = 56 * 1024 * 1024
NEG_BIG = -1e30
F32_TINY = float(np.finfo(np.float32).tiny)
LOG2E = math.log2(math.e)

Q_BLOCK = 128
Q_CHAINS = 4
Q_STEP = Q_CHAINS * Q_BLOCK
KV_TILE = 1024
GATE_ROWS = 16


def _params(*sem):
    return pltpu.CompilerParams(dimension_semantics=sem, vmem_limit_bytes=VMEM_LIMIT_BYTES)


def _silu(x):
    return x * jax.nn.sigmoid(x)


def _rms(x, g):
    return x * lax.rsqrt(jnp.mean(x * x, axis=-1, keepdims=True) + EPS) * g


def _modulate(x, g, shift, scale):
    return _rms(x, g) * (1.0 + scale) + shift


def _dot(a, b):
    return jnp.dot(a, b, preferred_element_type=F32)


def _dot_nt(a, b):
    return lax.dot_general(a, b, (((1,), (1,)), ((), ())), preferred_element_type=F32)


def _ada_kernel(c_ref, w_ref, b_ref, o_ref):
    cond = _silu(c_ref[...]).astype(BF16)
    o_ref[...] = _dot(cond, w_ref[...].astype(BF16)) + b_ref[...]


def _ada(c_pad, w, b):
    rows, d = c_pad.shape
    n = w.shape[1]
    tn = 1024
    return pl.pallas_call(
        _ada_kernel,
        grid=(n // tn,),
        in_specs=[pl.BlockSpec((rows, d), lambda j: (0, 0)),
                  pl.BlockSpec((d, tn), lambda j: (0, j)),
                  pl.BlockSpec((1, tn), lambda j: (0, j))],
        out_specs=pl.BlockSpec((rows, tn), lambda j: (0, j)),
        out_shape=jax.ShapeDtypeStruct((rows, n), F32),
        compiler_params=_params("parallel"),
        name="ada",
    )(c_pad, w, b)


def _ffn_kernel(x_ref, sh_ref, sc_ref, gt_ref, g_ref, wg_ref, wu_ref, wd_ref, o_ref, h_ref):
    j = pl.program_id(1)

    @pl.when(j == 0)
    def _():
        h_ref[...] = _modulate(x_ref[...], g_ref[...], sh_ref[0], sc_ref[0]).astype(BF16)
        o_ref[...] = jnp.zeros_like(o_ref)

    h = h_ref[...]
    a = _dot(h, wg_ref[...].astype(BF16))
    b = _dot(h, wu_ref[...].astype(BF16))
    act = (_silu(a) * b).astype(BF16)
    o_ref[...] += _dot(act, wd_ref[...].astype(BF16))

    @pl.when(j == pl.num_programs(1) - 1)
    def _():
        o_ref[...] = x_ref[...] + (0.5 * gt_ref[0]) * o_ref[...]


def _ffn(x2, shift, scale, gate, g, wg, wu, wd, which, seq):
    t, d = x2.shape
    ff = wg.shape[2]
    tm, tf = 1024, 256
    per_b = seq // tm
    mod_spec = pl.BlockSpec((1, 1, d), lambda i, j: (i // per_b, 0, 0))
    return pl.pallas_call(
        _ffn_kernel,
        grid=(t // tm, ff // tf),
        in_specs=[pl.BlockSpec((tm, d), lambda i, j: (i, 0), pipeline_mode=pl.Buffered(1)),
                  mod_spec, mod_spec, mod_spec,
                  pl.BlockSpec((1, d), lambda i, j: (0, 0)),
                  pl.BlockSpec((None, d, tf), lambda i, j: (which, 0, j)),
                  pl.BlockSpec((None, d, tf), lambda i, j: (which, 0, j)),
                  pl.BlockSpec((None, tf, d), lambda i, j: (which, j, 0))],
        out_specs=pl.BlockSpec((tm, d), lambda i, j: (i, 0)),
        out_shape=jax.ShapeDtypeStruct((t, d), F32),
        scratch_shapes=[pltpu.VMEM((tm, d), BF16)],
        compiler_params=_params("parallel", "arbitrary"),
        name="ffn",
    )(x2, shift, scale, gate, g, wg, wu, wd)


def _mod_kernel(x_ref, sh_ref, sc_ref, g_ref, o_ref):
    o_ref[...] = _modulate(x_ref[...], g_ref[...], sh_ref[0], sc_ref[0]).astype(BF16)


def _mod(x2, shift, scale, g, seq):
    t, d = x2.shape
    tm = 512
    per_b = seq // tm
    mod_spec = pl.BlockSpec((1, 1, d), lambda i: (i // per_b, 0, 0))
    return pl.pallas_call(
        _mod_kernel,
        grid=(t // tm,),
        in_specs=[pl.BlockSpec((tm, d), lambda i: (i, 0)), mod_spec, mod_spec,
                  pl.BlockSpec((1, d), lambda i: (0, 0))],
        out_specs=pl.BlockSpec((tm, d), lambda i: (i, 0)),
        out_shape=jax.ShapeDtypeStruct((t, d), BF16),
        compiler_params=_params("parallel"),
        name="modulate",
    )(x2, shift, scale, g)


def _proj_kernel(u_ref, w_ref, gain_ref, o_ref, *, epilogue):
    acc = _dot(u_ref[...], w_ref[...])
    if epilogue == "headnorm":
        g = gain_ref[...]
        parts = [_rms(acc[:, k:k + NSA_HEAD_DIM], g) for k in range(0, acc.shape[1], NSA_HEAD_DIM)]
        acc = jnp.concatenate(parts, axis=1)
    elif epilogue == "keyscale":
        acc = acc * (RET_KEY_DIM ** -0.5)
    elif epilogue == "sigmoid":
        acc = jax.nn.sigmoid(acc)
    o_ref[...] = acc.astype(o_ref.dtype)


def _proj(u, w, out_dtype, epilogue="plain", gain=None, name="proj"):
    t, d = u.shape
    n = w.shape[1]
    tm = 2048
    tn = 1024 if n % 1024 == 0 else 512
    if gain is None:
        gain = jnp.ones((1, NSA_HEAD_DIM), F32)
    return pl.pallas_call(
        functools.partial(_proj_kernel, epilogue=epilogue),
        grid=(t // tm, n // tn),
        in_specs=[pl.BlockSpec((tm, d), lambda i, j: (i, 0)),
                  pl.BlockSpec((d, tn), lambda i, j: (0, j)),
                  pl.BlockSpec((1, NSA_HEAD_DIM), lambda i, j: (0, 0))],
        out_specs=pl.BlockSpec((tm, tn), lambda i, j: (i, j)),
        out_shape=jax.ShapeDtypeStruct((t, n), out_dtype),
        compiler_params=_params("parallel", "parallel"),
        name=name,
    )(u, w, gain)


def _proj_t_kernel(u_ref, w_ref, gain_ref, o_ref, *, headnorm):
    acc = lax.dot_general(w_ref[...], u_ref[...], (((0,), (1,)), ((), ())), preferred_element_type=F32)
    if headnorm:
        g = gain_ref[...]
        parts = []
        for k in range(0, acc.shape[0], NSA_HEAD_DIM):
            xh = acc[k:k + NSA_HEAD_DIM]
            parts.append(xh * lax.rsqrt(jnp.mean(xh * xh, axis=0, keepdims=True) + EPS) * g)
        acc = jnp.concatenate(parts, axis=0)
    o_ref[...] = acc.astype(o_ref.dtype)


def _proj_t(u, w, out_dtype, gain=None, name="proj_t"):
    t, d = u.shape
    n = w.shape[1]
    tm = 2048
    tn = 1024 if n % 1024 == 0 else 512
    headnorm = gain is not None
    if gain is None:
        gain = jnp.ones((NSA_HEAD_DIM,), F32)
    return pl.pallas_call(
        functools.partial(_proj_t_kernel, headnorm=headnorm),
        grid=(t // tm, n // tn),
        in_specs=[pl.BlockSpec((tm, d), lambda i, j: (i, 0)),
                  pl.BlockSpec((d, tn), lambda i, j: (0, j)),
                  pl.BlockSpec((NSA_HEAD_DIM, 1), lambda i, j: (0, 0))],
        out_specs=pl.BlockSpec((tn, tm), lambda i, j: (j, i)),
        out_shape=jax.ShapeDtypeStruct((n, t), out_dtype),
        compiler_params=_params("parallel", "parallel"),
        name=name,
    )(u, w, gain[:, None])


def _cmp_kernel(z_ref, pos_ref, w1_ref, b1_ref, w2_ref, gk_ref, o_ref):
    half = CMP_BLOCK // 2
    n_rows = z_ref.shape[1] // CMP_STRIDE
    y_lo = jnp.zeros((n_rows, NSA_HEAD_DIM), F32)
    y_hi = jnp.zeros((n_rows, NSA_HEAD_DIM), F32)
    for l in range(half):
        zl = z_ref[0, pl.ds(l, n_rows, stride=CMP_STRIDE), :]
        lo = (zl + pos_ref[0, l:l + 1, :]).astype(BF16)
        hi = (zl + pos_ref[0, half + l:half + l + 1, :]).astype(BF16)
        y_lo += _dot(lo, w1_ref[0, l * NSA_HEAD_DIM:(l + 1) * NSA_HEAD_DIM, :])
        y_hi += _dot(hi, w1_ref[0, (half + l) * NSA_HEAD_DIM:(half + l + 1) * NSA_HEAD_DIM, :])
    y = y_lo + pltpu.roll(y_hi, n_rows - 1, 0)
    hdn = jax.nn.gelu(y + b1_ref[0]).astype(BF16)
    out = _dot(hdn, w2_ref[0])
    normed = _rms(out, gk_ref[...])
    is_key = pl.program_id(0) == 0
    o_ref[0, 0, 0] = jnp.where(is_key, normed, out.T).astype(o_ref.dtype)


def _compress(raw, pos, w1, b1, w2, gk):
    b, s, _ = raw.shape
    g = NSA_KV_GROUPS
    n_rows = s // CMP_STRIDE
    assert n_rows == NSA_HEAD_DIM, "key / transposed-value tiles share one square output block"
    return pl.pallas_call(
        _cmp_kernel,
        grid=(2, b, g),
        in_specs=[pl.BlockSpec((1, s, NSA_HEAD_DIM), lambda i, bb, gg: (bb, 0, i * g + gg)),
                  pl.BlockSpec((1, CMP_BLOCK, NSA_HEAD_DIM), lambda i, bb, gg: (i, 0, 0)),
                  pl.BlockSpec((1, CMP_BLOCK * NSA_HEAD_DIM, NSA_HEAD_DIM), lambda i, bb, gg: (i, 0, 0)),
                  pl.BlockSpec((1, 1, NSA_HEAD_DIM), lambda i, bb, gg: (i, 0, 0)),
                  pl.BlockSpec((1, NSA_HEAD_DIM, NSA_HEAD_DIM), lambda i, bb, gg: (i, 0, 0)),
                  pl.BlockSpec((1, NSA_HEAD_DIM), lambda i, bb, gg: (0, 0))],
        out_specs=pl.BlockSpec((1, 1, 1, n_rows, NSA_HEAD_DIM), lambda i, bb, gg: (i, bb, gg, 0, 0)),
        out_shape=jax.ShapeDtypeStruct((2, b, g, n_rows, NSA_HEAD_DIM), BF16),
        compiler_params=_params("parallel", "parallel", "parallel"),
        name="compress",
    )(raw, pos, w1, b1, w2, gk)


def _head(x, h):
    return x[:, h * Q_BLOCK:(h + 1) * Q_BLOCK]


def _round_up(x, m):
    return (x + m - 1) // m * m


def _pad_rows(x, n):
    if x.shape[0] == n:
        return x
    return jnp.concatenate([x, jnp.zeros((n - x.shape[0], x.shape[1]), x.dtype)], axis=0)


def _nsa_kernel(slope_ref, qt_ref, kc_ref, vct_ref, ks_ref, vst_ref, kw_ref, vwt_ref, gt_ref, ovl_ref, exp_ref,
                prev_ref, o_ref, *, step):
    del prev_ref
    hg, dh = HEADS_PER_GROUP, NSA_HEAD_DIM
    chains = range(Q_CHAINS)
    grp = pl.program_id(1)
    t0 = [step * Q_STEP + c * Q_BLOCK for c in chains]
    t_end = (step + 1) * Q_STEP
    c1 = (dh ** -0.5) * LOG2E
    slope2 = [slope_ref[grp * hg + h] for h in range(hg)]

    qt = [jnp.concatenate([qt_ref[h * dh:(h + 1) * dh, c * Q_BLOCK:(c + 1) * Q_BLOCK] for h in range(hg)], axis=1)
          for c in chains]
    t_row = [t0[c] + lax.broadcasted_iota(jnp.int32, (1, Q_BLOCK), 1) for c in chains]

    n_cmp_pad = kc_ref.shape[3]
    n_slc = ovl_ref.shape[0]
    n_cmp = min(n_cmp_pad, _round_up((t_end - CMP_BLOCK) // CMP_STRIDE + 1, 16))
    n_cand = min(n_slc, t_end // SLC_BLOCK)
    n_blk = min(n_slc, _round_up(n_cand, 8))
    cstart = lax.broadcasted_iota(jnp.int32, (n_cmp, Q_BLOCK), 0) * CMP_STRIDE
    centre = cstart.astype(F32) + (CMP_BLOCK - 1) / 2
    jj = lax.broadcasted_iota(jnp.int32, (n_blk, Q_BLOCK), 0)
    kc, vct, ovl = kc_ref[0, 0, 0, 0:n_cmp, :], vct_ref[0, 0, 0], ovl_ref[0:n_blk, :]
    o_cmp, member = [], []
    for c in chains:
        centre_dist = t_row[c].astype(F32) - centre
        valid = (cstart + (CMP_BLOCK - 1)) <= t_row[c]
        sc = _dot(kc, qt[c])
        p_heads = []
        for h in range(hg):
            s = _head(sc, h) * c1 - slope2[h] * centre_dist
            s = jnp.where(valid, s, -jnp.inf)
            m = jnp.max(s, axis=0, keepdims=True)
            m = jnp.where(jnp.isfinite(m), m, 0.0)
            e = jnp.where(valid, jnp.exp2(s - m), 0.0)
            p = e / jnp.maximum(jnp.sum(e, axis=0, keepdims=True), F32_TINY)
            p_heads.append(_pad_rows(p.astype(BF16), n_cmp_pad))
        o_cmp.append(_dot(vct, jnp.concatenate(p_heads, axis=1)))

        imp = _dot(ovl, p_heads[0])
        for h in range(1, hg):
            imp += _dot(ovl, p_heads[h])
        blk_t = t_row[c] // SLC_BLOCK
        forced = (jj == 0) | (jj == blk_t) | (jj == blk_t - 1)
        imp = jnp.where(forced, SEL_FORCE, jnp.where(jj <= blk_t, imp, -SEL_FORCE))
        rank = jnp.zeros((n_blk, Q_BLOCK), jnp.int32)
        for i in range(n_cand):
            ri = imp[i:i + 1, :]
            before = (ri > imp) | ((ri == imp) & (jj > i))
            rank += before.astype(jnp.int32)
        member.append(_pad_rows(jnp.where(rank < min(SLC_TOP_N, n_slc), 1.0, 0.0).astype(BF16), n_slc))

    key_iota = {}

    def key_local(n):
        if n not in key_iota:
            key_iota[n] = lax.broadcasted_iota(jnp.int32, (n, Q_BLOCK), 0)
        return key_iota[n]

    def tiles(lo, hi):
        return [(p, min(KV_TILE, hi - p)) for p in range(lo, hi, KV_TILE)]

    streams = []
    for c in chains:
        streams.append([("slc", c, ks_ref, vst_ref, lo, n) for lo, n in tiles(0, t0[c] + Q_BLOCK)])
    for c in chains:
        streams.append([("win", c, kw_ref, vwt_ref, lo, n)
                        for lo, n in tiles(max(t0[c] - WINDOW, 0), t0[c] + Q_BLOCK)])
    items = [s[i] for i in range(max(len(s) for s in streams)) for s in streams if i < len(s)]

    def scores(item):
        _, c, k_ref, _, lo, n = item
        return _dot(k_ref[lo:lo + n, :], qt[c])

    def finish(state, item, sc):
        branch, c, _, vt_ref, lo, n = item
        dist = t_row[c] - (lo + key_local(n))
        masks = []
        if branch == "slc":
            masks.append(_dot(exp_ref[lo:lo + n, :], member[c]) > 0.5)
        if lo + n - 1 > t0[c]:
            masks.append(dist >= 0)
        if branch == "win" and t0[c] + Q_BLOCK - 1 - lo >= WINDOW:
            masks.append(dist < WINDOW)
        mask = functools.reduce(jnp.logical_and, masks) if masks else None
        dist_f = dist.astype(F32)
        m_new, l_new, alpha, p_t = [], [], [], []
        for h in range(hg):
            s = _head(sc, h) * c1 - slope2[h] * dist_f
            if mask is not None:
                s = jnp.where(mask, s, NEG_BIG)
            m_n = jnp.max(s, axis=0, keepdims=True)
            if state is not None:
                m_o = _head(state[0], h)
                m_n = jnp.maximum(m_o, m_n)
                a = jnp.exp2(m_o - m_n)
                alpha.append(a)
            p = jnp.exp2(s - m_n)
            l_n = jnp.sum(p, axis=0, keepdims=True)
            if state is not None:
                l_n = a * _head(state[1], h) + l_n
            m_new.append(m_n)
            l_new.append(l_n)
            p_t.append(p.astype(BF16))
        acc = _dot(vt_ref[:, lo:lo + n], jnp.concatenate(p_t, axis=1))
        if state is not None:
            acc = jnp.concatenate(alpha, axis=1) * state[2] + acc
        return jnp.concatenate(m_new, axis=1), jnp.concatenate(l_new, axis=1), acc

    states = {}
    sc_next = scores(items[0])
    for i, item in enumerate(items):
        sc = sc_next
        if i + 1 < len(items):
            sc_next = scores(items[i + 1])
        states[item[:2]] = finish(states.get(item[:2]), item, sc)

    for c in chains:
        gt = gt_ref[0, :, c * Q_BLOCK:(c + 1) * Q_BLOCK]

        def gate(br):
            return jnp.concatenate([gt[br * hg + h:br * hg + h + 1, :] for h in range(hg)], axis=1)

        (_, l_s, acc_s), (_, l_w, acc_w) = states[("slc", c)], states[("win", c)]
        o = (gate(0) * o_cmp[c] + (gate(1) / jnp.maximum(l_s, F32_TINY)) * acc_s
             + (gate(2) / jnp.maximum(l_w, F32_TINY)) * acc_w)
        o_ref[c * Q_BLOCK:(c + 1) * Q_BLOCK, :] = jnp.concatenate(
            [_head(o, h).T for h in range(hg)], axis=1).astype(o_ref.dtype)


def _nsa(qt, kv_cmp, k_slc, vt_slc, k_win, vt_win, gates_t, slopes2, ovl, expand, batch, seq):
    g = NSA_KV_GROUPS
    gw = HEADS_PER_GROUP * NSA_HEAD_DIM
    nq = seq // Q_STEP
    n_cmp_pad = kv_cmp.shape[3]
    out = jnp.zeros((batch * seq, NSA_WIDTH), BF16)
    for step in range(nq):
        per_seq = 1
        while seq // (2 * per_seq) >= (step + 1) * Q_STEP:
            per_seq *= 2
        n_keys = seq // per_seq
        k_spec = pl.BlockSpec((n_keys, NSA_HEAD_DIM), lambda b, gg, per_seq=per_seq: (b * per_seq, gg))
        vt_spec = pl.BlockSpec((NSA_HEAD_DIM, n_keys), lambda b, gg, per_seq=per_seq: (gg, b * per_seq))
        row_blk = lambda b, gg, step=step: (b * nq + step, gg)
        out = pl.pallas_call(
            functools.partial(_nsa_kernel, step=step),
            grid=(batch, g),
            in_specs=[pl.BlockSpec(memory_space=pltpu.SMEM),
                      pl.BlockSpec((gw, Q_STEP), lambda b, gg, step=step: (gg, b * nq + step)),
                      pl.BlockSpec((1, 1, 1, n_cmp_pad, NSA_HEAD_DIM), lambda b, gg: (0, b, gg, 0, 0)),
                      pl.BlockSpec((1, 1, 1, NSA_HEAD_DIM, n_cmp_pad), lambda b, gg: (1, b, gg, 0, 0)),
                      k_spec, vt_spec, k_spec, vt_spec,
                      pl.BlockSpec((1, GATE_ROWS, Q_STEP), lambda b, gg, step=step: (gg, 0, b * nq + step)),
                      pl.BlockSpec(ovl.shape, lambda b, gg: (0, 0)),
                      pl.BlockSpec(expand.shape, lambda b, gg: (0, 0)),
                      pl.BlockSpec(memory_space=pl.ANY)],
            out_specs=pl.BlockSpec((Q_STEP, gw), row_blk),
            out_shape=jax.ShapeDtypeStruct((batch * seq, NSA_WIDTH), BF16),
            input_output_aliases={11: 0},
            compiler_params=_params("parallel", "parallel"),
            name=f"nsa{step}",
        )(slopes2, qt, kv_cmp, kv_cmp, k_slc, vt_slc, k_win, vt_win, gates_t, ovl, expand, out)
    return out


def _ret_kernel(cd_ref, q_ref, k_ref, v_ref, g_ref, gain_ref, decay_ref, zeta_ref, xi_ref, o_ref, state_ref):
    @pl.when(pl.program_id(1) == 0)
    def _():
        state_ref[...] = jnp.zeros_like(state_ref)

    dk, dv = RET_KEY_DIM, RET_VAL_DIM
    for h in range(RET_HEADS):
        q = q_ref[:, h * dk:(h + 1) * dk]
        k = k_ref[:, h * dk:(h + 1) * dk]
        v = v_ref[:, h * dv:(h + 1) * dv]
        sc = _dot_nt(q.astype(BF16), k.astype(BF16)) * decay_ref[h]
        inner = _dot(sc.astype(BF16), v)
        st = state_ref[h]
        cross = _dot((q * xi_ref[h]).astype(BF16), st.astype(BF16))
        kz_t = (k * zeta_ref[h]).T.astype(BF16)
        state_ref[h] = st * cd_ref[h] + _dot(kz_t, v)
        y = inner + cross
        yc = y - jnp.mean(y, axis=-1, keepdims=True)
        yn = yc * lax.rsqrt(jnp.mean(yc * yc, axis=-1, keepdims=True) + EPS) * gain_ref[:, h * dv:(h + 1) * dv]
        o_ref[:, h * dv:(h + 1) * dv] = (_silu(g_ref[:, h * dv:(h + 1) * dv]) * yn).astype(o_ref.dtype)


def _retention(q, k, v, g, gain, consts, batch, seq):
    chunk_decay, decay, zeta, xi = consts
    c = RET_CHUNK
    nc = seq // c
    row = lambda b, ci: (b * nc + ci, 0)
    whole3 = lambda b, ci: (0, 0, 0)
    return pl.pallas_call(
        _ret_kernel,
        grid=(batch, nc),
        in_specs=[pl.BlockSpec(memory_space=pltpu.SMEM),
                  pl.BlockSpec((c, RET_QK_WIDTH), row),
                  pl.BlockSpec((c, RET_QK_WIDTH), row),
                  pl.BlockSpec((c, RET_V_WIDTH), row),
                  pl.BlockSpec((c, RET_V_WIDTH), row),
                  pl.BlockSpec((1, RET_V_WIDTH), lambda b, ci: (0, 0)),
                  pl.BlockSpec((RET_HEADS, c, c), whole3),
                  pl.BlockSpec((RET_HEADS, c, 1), whole3),
                  pl.BlockSpec((RET_HEADS, c, 1), whole3)],
        out_specs=pl.BlockSpec((c, RET_V_WIDTH), row),
        out_shape=jax.ShapeDtypeStruct((batch * seq, RET_V_WIDTH), BF16),
        scratch_shapes=[pltpu.VMEM((RET_HEADS, RET_KEY_DIM, RET_VAL_DIM), F32)],
        compiler_params=_params("parallel", "arbitrary"),
        name="retention",
    )(chunk_decay, q, k, v, g, gain, decay, zeta, xi)


def _retention_consts():
    c = RET_CHUNK
    log_gamma = jnp.log1p(-jnp.exp2(-5.0 - jnp.arange(RET_HEADS, dtype=F32)))
    n = jnp.arange(c, dtype=F32)
    diff = n[:, None] - n[None, :]
    decay = jnp.where(diff >= 0, jnp.exp(log_gamma[:, None, None] * jnp.maximum(diff, 0.0)), 0.0)
    zeta = jnp.exp(log_gamma[:, None] * (c - 1 - n)[None, :])[:, :, None]
    xi = jnp.exp(log_gamma[:, None] * (n + 1.0)[None, :])[:, :, None]
    chunk_decay = jnp.exp(log_gamma * c)
    return chunk_decay, decay, zeta, xi


def _merge_kernel(x_ref, gt_ref, on_ref, or_ref, ga_ref, gb_ref, wpn_ref, wpr_ref, wo_ref, o_ref, acc_ref):
    j = pl.program_id(1)

    @pl.when(j == 0)
    def _():
        acc_ref[...] = jnp.zeros_like(acc_ref)

    a = _dot(on_ref[...], wpn_ref[...])
    b = _dot(or_ref[...], wpr_ref[...])
    merged = jax.nn.sigmoid(ga_ref[...]) * a + jax.nn.sigmoid(gb_ref[...]) * b
    acc_ref[...] += _dot(merged.astype(BF16), wo_ref[...])

    @pl.when(j == pl.num_programs(1) - 1)
    def _():
        o_ref[...] = x_ref[...] + gt_ref[0] * acc_ref[...]


def _merge(x2, gate, o_nsa, o_ret, ga, gb, wpn, wpr, wo, seq):
    t, d = x2.shape
    tm, tn = 512, 512
    per_b = seq // tm
    return pl.pallas_call(
        _merge_kernel,
        grid=(t // tm, d // tn),
        in_specs=[pl.BlockSpec((tm, d), lambda i, j: (i, 0)),
                  pl.BlockSpec((1, 1, d), lambda i, j: (i // per_b, 0, 0)),
                  pl.BlockSpec((tm, o_nsa.shape[1]), lambda i, j: (i, 0)),
                  pl.BlockSpec((tm, o_ret.shape[1]), lambda i, j: (i, 0)),
                  pl.BlockSpec((tm, tn), lambda i, j: (i, j)),
                  pl.BlockSpec((tm, tn), lambda i, j: (i, j)),
                  pl.BlockSpec((wpn.shape[0], tn), lambda i, j: (0, j)),
                  pl.BlockSpec((wpr.shape[0], tn), lambda i, j: (0, j)),
                  pl.BlockSpec((tn, d), lambda i, j: (j, 0))],
        out_specs=pl.BlockSpec((tm, d), lambda i, j: (i, 0)),
        out_shape=jax.ShapeDtypeStruct((t, d), F32),
        scratch_shapes=[pltpu.VMEM((tm, d), F32)],
        compiler_params=_params("parallel", "arbitrary"),
        name="merge",
    )(x2, gate, o_nsa, o_ret, ga, gb, wpn, wpr, wo)


def _nsa_tables(seq):
    n_cmp = (seq - CMP_BLOCK) // CMP_STRIDE + 1
    n_cmp_pad = seq // CMP_STRIDE
    n_slc = seq // SLC_BLOCK
    cs = (np.arange(n_cmp) * CMP_STRIDE)[:, None]
    js = (np.arange(n_slc) * SLC_BLOCK)[None, :]
    overlap = np.clip(np.minimum(cs + CMP_BLOCK, js + SLC_BLOCK) - np.maximum(cs, js), 0, None) / CMP_BLOCK
    ovl = np.zeros((n_slc, n_cmp_pad), np.float32)
    ovl[:, :n_cmp] = overlap.T
    expand = (np.arange(seq)[:, None] // SLC_BLOCK == np.arange(n_slc)[None, :]).astype(np.float32)
    slopes2 = jnp.exp2(-8.0 * jnp.arange(1, NSA_HEADS + 1, dtype=F32) / NSA_HEADS) * LOG2E
    return jnp.asarray(ovl, BF16), jnp.asarray(expand, BF16), slopes2


def kernel(x, c, w_ada, b_ada, g_norm, w_ffn_gate, w_ffn_up, w_ffn_down, w_in, g_qk, cmp_pos, cmp_w1, cmp_b1,
           cmp_w2, ret_gn_gain, w_proj_nsa, w_proj_ret, w_out):
    batch, seq, d = x.shape
    depth = w_ada.shape[0]
    t = batch * seq
    x2 = x.reshape(t, d)
    c_pad = jnp.pad(c, ((0, 8 - batch), (0, 0)))
    ovl, expand, slopes2 = _nsa_tables(seq)
    ret_consts = _retention_consts()
    hg = HEADS_PER_GROUP

    for l in range(depth):
        ada = _ada(c_pad, w_ada[l], b_ada[l][None, :])[:batch].reshape(batch, N_ADA, 1, d)
        sh1, sc1, gt1, sh2, sc2, gt2, sh3, sc3, gt3 = [ada[:, i] for i in range(N_ADA)]

        x2 = _ffn(x2, sh1, sc1, gt1, g_norm[l, 0][None, :], w_ffn_gate[l], w_ffn_up[l], w_ffn_down[l], 0, seq)

        u = _mod(x2, sh2, sc2, g_norm[l, 1][None, :], seq)
        w = w_in[l]
        off = [0]

        def take(width):
            w_part = w[:, off[0]:off[0] + width]
            off[0] += width
            return w_part.astype(BF16)

        gq = g_qk[l]
        qt_nsa = _proj_t(u, take(NSA_WIDTH), BF16, gq[0], name="proj_q")
        cmp_raw = _proj(u, take(2 * KV_WIDTH), F32, name="proj_cmp_raw")
        k_slc = _proj(u, take(KV_WIDTH), BF16, "headnorm", gq[2][None, :], name="proj_k_slc")
        vt_slc = _proj_t(u, take(KV_WIDTH), BF16, name="proj_v_slc")
        k_win = _proj(u, take(KV_WIDTH), BF16, "headnorm", gq[3][None, :], name="proj_k_win")
        vt_win = _proj_t(u, take(KV_WIDTH), BF16, name="proj_v_win")
        w_gl = take(3 * NSA_HEADS).reshape(d, 3, NSA_KV_GROUPS, hg).transpose(0, 2, 1, 3)
        w_gl = jnp.pad(w_gl.reshape(d, NSA_KV_GROUPS, 3 * hg),
                       ((0, 0), (0, 0), (0, LANES - 3 * hg))).reshape(d, NSA_KV_GROUPS * LANES)
        gates = _proj(u, w_gl, F32, "sigmoid", name="proj_gates")
        gates_t = gates.reshape(t, NSA_KV_GROUPS, LANES)[:, :, :GATE_ROWS].transpose(1, 2, 0)
        q_r = _proj(u, take(RET_QK_WIDTH), F32, name="proj_q_ret")
        k_r = _proj(u, take(RET_QK_WIDTH), F32, "keyscale", name="proj_k_ret")
        v_r = _proj(u, take(RET_V_WIDTH), BF16, name="proj_v_ret")
        g_r = _proj(u, take(RET_V_WIDTH), F32, name="proj_g_ret")
        ga = _proj(u, take(d), F32, name="proj_ga")
        gb = _proj(u, take(d), F32, name="proj_gb")

        kv_cmp = _compress(cmp_raw.reshape(batch, seq, 2 * KV_WIDTH), cmp_pos[l], cmp_w1[l].astype(BF16),
                           cmp_b1[l][:, None, :], cmp_w2[l].astype(BF16), gq[1][None, :])
        o_nsa = _nsa(qt_nsa, kv_cmp, k_slc, vt_slc, k_win, vt_win, gates_t, slopes2, ovl, expand, batch, seq)
        o_ret = _retention(q_r, k_r, v_r, g_r, ret_gn_gain[l].reshape(1, RET_V_WIDTH), ret_consts, batch, seq)

        x2 = _merge(x2, gt2, o_nsa, o_ret, ga, gb, w_proj_nsa[l].astype(BF16), w_proj_ret[l].astype(BF16),
                    w_out[l].astype(BF16), seq)

        x2 = _ffn(x2, sh3, sc3, gt3, g_norm[l, 2][None, :], w_ffn_gate[l], w_ffn_up[l], w_ffn_down[l], 1, seq)

    return x2.reshape(batch, seq, d)
```

```python
import functools
import math

import numpy as np
import jax
import jax.numpy as jnp
from jax import lax
from jax.experimental import pallas as pl
from jax.experimental.pallas import tpu as pltpu

F32 = jnp.float32
BF16 = jnp.bfloat16

D_MODEL = 2048
NSA_HEADS = 16
NSA_KV_GROUPS = 4
HEADS_PER_GROUP = NSA_HEADS // NSA_KV_GROUPS
NSA_HEAD_DIM = 128
CMP_BLOCK = 32
CMP_STRIDE = 16
SLC_BLOCK = 64
SLC_TOP_N = 16
WINDOW = 512
RET_HEADS = 8
RET_KEY_DIM = 128
RET_VAL_DIM = 256
RET_CHUNK = 128
D_FF = 5632
N_ADA = 9
EPS = 1e-6
SEL_FORCE = 1e4

NSA_WIDTH = NSA_HEADS * NSA_HEAD_DIM
KV_WIDTH = NSA_KV_GROUPS * NSA_HEAD_DIM
RET_QK_WIDTH = RET_HEADS * RET_KEY_DIM
RET_V_WIDTH = RET_HEADS * RET_VAL_DIM

LANES = 128
VMEM_LIMIT_BYTES = 56 * 1024 * 1024
NEG_BIG = -1e30
F32_TINY = float(np.finfo(np.float32).tiny)
LOG2E = math.log2(math.e)

Q_BLOCK = 128
Q_CHAINS = 4
Q_STEP = Q_CHAINS * Q_BLOCK
KV_TILE = 1024
GATE_ROWS = 16


def _params(*sem):
    return pltpu.CompilerParams(dimension_semantics=sem, vmem_limit_bytes=VMEM_LIMIT_BYTES)


def _silu(x):
    return x * jax.nn.sigmoid(x)


def _rms(x, g):
    return x * lax.rsqrt(jnp.mean(x * x, axis=-1, keepdims=True) + EPS) * g


def _modulate(x, g, shift, scale):
    return _rms(x, g) * (1.0 + scale) + shift


def _dot(a, b):
    return jnp.dot(a, b, preferred_element_type=F32)


def _dot_nt(a, b):
    return lax.dot_general(a, b, (((1,), (1,)), ((), ())), preferred_element_type=F32)


def _ada_kernel(c_ref, w_ref, b_ref, o_ref):
    cond = _silu(c_ref[...]).astype(BF16)
    o_ref[...] = _dot(cond, w_ref[...].astype(BF16)) + b_ref[...]


def _ada(c_pad, w, b):
    rows, d = c_pad.shape
    n = w.shape[1]
    tn = 1024
    return pl.pallas_call(
        _ada_kernel,
        grid=(n // tn,),
        in_specs=[pl.BlockSpec((rows, d), lambda j: (0, 0)),
                  pl.BlockSpec((d, tn), lambda j: (0, j)),
                  pl.BlockSpec((1, tn), lambda j: (0, j))],
        out_specs=pl.BlockSpec((rows, tn), lambda j: (0, j)),
        out_shape=jax.ShapeDtypeStruct((rows, n), F32),
        compiler_params=_params("parallel"),
        name="ada",
    )(c_pad, w, b)


def _ffn_kernel(x_ref, sh_ref, sc_ref, gt_ref, g_ref, wg_ref, wu_ref, wd_ref, o_ref, h_ref):
    j = pl.program_id(1)

    @pl.when(j == 0)
    def _():
        h_ref[...] = _modulate(x_ref[...], g_ref[...], sh_ref[0], sc_ref[0]).astype(BF16)
        o_ref[...] = jnp.zeros_like(o_ref)

    h = h_ref[...]
    a = _dot(h, wg_ref[...].astype(BF16))
    b = _dot(h, wu_ref[...].astype(BF16))
    act = (_silu(a) * b).astype(BF16)
    o_ref[...] += _dot(act, wd_ref[...].astype(BF16))

    @pl.when(j == pl.num_programs(1) - 1)
    def _():
        o_ref[...] = x_ref[...] + (0.5 * gt_ref[0]) * o_ref[...]


def _ffn(x2, shift, scale, gate, g, wg, wu, wd, which, seq):
    t, d = x2.shape
    ff = wg.shape[2]
    tm, tf = 1024, 256
    per_b = seq // tm
    mod_spec = pl.BlockSpec((1, 1, d), lambda i, j: (i // per_b, 0, 0))
    return pl.pallas_call(
        _ffn_kernel,
        grid=(t // tm, ff // tf),
        in_specs=[pl.BlockSpec((tm, d), lambda i, j: (i, 0), pipeline_mode=pl.Buffered(1)),
                  mod_spec, mod_spec, mod_spec,
                  pl.BlockSpec((1, d), lambda i, j: (0, 0)),
                  pl.BlockSpec((None, d, tf), lambda i, j: (which, 0, j)),
                  pl.BlockSpec((None, d, tf), lambda i, j: (which, 0, j)),
                  pl.BlockSpec((None, tf, d), lambda i, j: (which, j, 0))],
        out_specs=pl.BlockSpec((tm, d), lambda i, j: (i, 0)),
        out_shape=jax.ShapeDtypeStruct((t, d), F32),
        scratch_shapes=[pltpu.VMEM((tm, d), BF16)],
        compiler_params=_params("parallel", "arbitrary"),
        name="ffn",
    )(x2, shift, scale, gate, g, wg, wu, wd)


def _mod_kernel(x_ref, sh_ref, sc_ref, g_ref, o_ref):
    o_ref[...] = _modulate(x_ref[...], g_ref[...], sh_ref[0], sc_ref[0]).astype(BF16)


def _mod(x2, shift, scale, g, seq):
    t, d = x2.shape
    tm = 512
    per_b = seq // tm
    mod_spec = pl.BlockSpec((1, 1, d), lambda i: (i // per_b, 0, 0))
    return pl.pallas_call(
        _mod_kernel,
        grid=(t // tm,),
        in_specs=[pl.BlockSpec((tm, d), lambda i: (i, 0)), mod_spec, mod_spec,
                  pl.BlockSpec((1, d), lambda i: (0, 0))],
        out_specs=pl.BlockSpec((tm, d), lambda i: (i, 0)),
        out_shape=jax.ShapeDtypeStruct((t, d), BF16),
        compiler_params=_params("parallel"),
        name="modulate",
    )(x2, shift, scale, g)


def _proj_kernel(u_ref, w_ref, gain_ref, o_ref, *, epilogue):
    acc = _dot(u_ref[...], w_ref[...])
    if epilogue == "headnorm":
        g = gain_ref[...]
        parts = [_rms(acc[:, k:k + NSA_HEAD_DIM], g) for k in range(0, acc.shape[1], NSA_HEAD_DIM)]
        acc = jnp.concatenate(parts, axis=1)
    elif epilogue == "keyscale":
        acc = acc * (RET_KEY_DIM ** -0.5)
    elif epilogue == "sigmoid":
        acc = jax.nn.sigmoid(acc)
    o_ref[...] = acc.astype(o_ref.dtype)


def _proj(u, w, out_dtype, epilogue="plain", gain=None, name="proj"):
    t, d = u.shape
    n = w.shape[1]
    tm = 1024
    tn = 1024 if n % 1024 == 0 else 512
    if gain is None:
        gain = jnp.ones((1, NSA_HEAD_DIM), F32)
    return pl.pallas_call(
        functools.partial(_proj_kernel, epilogue=epilogue),
        grid=(t // tm, n // tn),
        in_specs=[pl.BlockSpec((tm, d), lambda i, j: (i, 0)),
                  pl.BlockSpec((d, tn), lambda i, j: (0, j)),
                  pl.BlockSpec((1, NSA_HEAD_DIM), lambda i, j: (0, 0))],
        out_specs=pl.BlockSpec((tm, tn), lambda i, j: (i, j)),
        out_shape=jax.ShapeDtypeStruct((t, n), out_dtype),
        compiler_params=_params("parallel", "parallel"),
        name=name,
    )(u, w, gain)


def _proj_t_kernel(u_ref, w_ref, gain_ref, o_ref, *, headnorm):
    acc = lax.dot_general(w_ref[...], u_ref[...], (((0,), (1,)), ((), ())), preferred_element_type=F32)
    if headnorm:
        g = gain_ref[...]
        parts = []
        for k in range(0, acc.shape[0], NSA_HEAD_DIM):
            xh = acc[k:k + NSA_HEAD_DIM]
            parts.append(xh * lax.rsqrt(jnp.mean(xh * xh, axis=0, keepdims=True) + EPS) * g)
        acc = jnp.concatenate(parts, axis=0)
    o_ref[...] = acc.astype(o_ref.dtype)


def _proj_t(u, w, out_dtype, gain=None, name="proj_t"):
    t, d = u.shape
    n = w.shape[1]
    tm = 1024
    tn = 1024 if n % 1024 == 0 else 512
    headnorm = gain is not None
    if gain is None:
        gain = jnp.ones((NSA_HEAD_DIM,), F32)
    return pl.pallas_call(
        functools.partial(_proj_t_kernel, headnorm=headnorm),
        grid=(t // tm, n // tn),
        in_specs=[pl.BlockSpec((tm, d), lambda i, j: (i, 0)),
                  pl.BlockSpec((d, tn), lambda i, j: (0, j)),
                  pl.BlockSpec((NSA_HEAD_DIM, 1), lambda i, j: (0, 0))],
        out_specs=pl.BlockSpec((tn, tm), lambda i, j: (j, i)),
        out_shape=jax.ShapeDtypeStruct((n, t), out_dtype),
        compiler_params=_params("parallel", "parallel"),
        name=name,
    )(u, w, gain[:, None])


def _cmp_kernel(z_ref, pos_ref, w1_ref, b1_ref, w2_ref, gk_ref, o_ref):
    half = CMP_BLOCK // 2
    n_rows = z_ref.shape[1] // CMP_STRIDE
    y_lo = jnp.zeros((n_rows, NSA_HEAD_DIM), F32)
    y_hi = jnp.zeros((n_rows, NSA_HEAD_DIM), F32)
    for l in range(half):
        zl = z_ref[0, pl.ds(l, n_rows, stride=CMP_STRIDE), :]
        lo = (zl + pos_ref[0, l:l + 1, :]).astype(BF16)
        hi = (zl + pos_ref[0, half + l:half + l + 1, :]).astype(BF16)
        y_lo += _dot(lo, w1_ref[0, l * NSA_HEAD_DIM:(l + 1) * NSA_HEAD_DIM, :])
        y_hi += _dot(hi, w1_ref[0, (half + l) * NSA_HEAD_DIM:(half + l + 1) * NSA_HEAD_DIM, :])
    y = y_lo + pltpu.roll(y_hi, n_rows - 1, 0)
    hdn = jax.nn.gelu(y + b1_ref[0]).astype(BF16)
    out = _dot(hdn, w2_ref[0])
    normed = _rms(out, gk_ref[...])
    is_key = pl.program_id(0) == 0
    o_ref[0, 0, 0] = jnp.where(is_key, normed, out.T).astype(o_ref.dtype)


def _compress(raw, pos, w1, b1, w2, gk):
    b, s, _ = raw.shape
    g = NSA_KV_GROUPS
    n_rows = s // CMP_STRIDE
    assert n_rows == NSA_HEAD_DIM, "key / transposed-value tiles share one square output block"
    return pl.pallas_call(
        _cmp_kernel,
        grid=(2, b, g),
        in_specs=[pl.BlockSpec((1, s, NSA_HEAD_DIM), lambda i, bb, gg: (bb, 0, i * g + gg)),
                  pl.BlockSpec((1, CMP_BLOCK, NSA_HEAD_DIM), lambda i, bb, gg: (i, 0, 0)),
                  pl.BlockSpec((1, CMP_BLOCK * NSA_HEAD_DIM, NSA_HEAD_DIM), lambda i, bb, gg: (i, 0, 0)),
                  pl.BlockSpec((1, 1, NSA_HEAD_DIM), lambda i, bb, gg: (i, 0, 0)),
                  pl.BlockSpec((1, NSA_HEAD_DIM, NSA_HEAD_DIM), lambda i, bb, gg: (i, 0, 0)),
                  pl.BlockSpec((1, NSA_HEAD_DIM), lambda i, bb, gg: (0, 0))],
        out_specs=pl.BlockSpec((1, 1, 1, n_rows, NSA_HEAD_DIM), lambda i, bb, gg: (i, bb, gg, 0, 0)),
        out_shape=jax.ShapeDtypeStruct((2, b, g, n_rows, NSA_HEAD_DIM), BF16),
        compiler_params=_params("parallel", "parallel", "parallel"),
        name="compress",
    )(raw, pos, w1, b1, w2, gk)


def _head(x, h):
    return x[:, h * Q_BLOCK:(h + 1) * Q_BLOCK]


def _round_up(x, m):
    return (x + m - 1) // m * m


def _pad_rows(x, n):
    if x.shape[0] == n:
        return x
    return jnp.concatenate([x, jnp.zeros((n - x.shape[0], x.shape[1]), x.dtype)], axis=0)


def _nsa_kernel(slope_ref, qt_ref, kc_ref, vct_ref, ks_ref, vst_ref, kw_ref, vwt_ref, gt_ref, ovl_ref, exp_ref,
                prev_ref, o_ref, *, step):
    del prev_ref
    hg, dh = HEADS_PER_GROUP, NSA_HEAD_DIM
    chains = range(Q_CHAINS)
    grp = pl.program_id(1)
    t0 = [step * Q_STEP + c * Q_BLOCK for c in chains]
    t_end = (step + 1) * Q_STEP
    c1 = (dh ** -0.5) * LOG2E
    slope2 = [slope_ref[grp * hg + h] for h in range(hg)]

    qt = [jnp.concatenate([qt_ref[h * dh:(h + 1) * dh, c * Q_BLOCK:(c + 1) * Q_BLOCK] for h in range(hg)], axis=1)
          for c in chains]
    t_row = [t0[c] + lax.broadcasted_iota(jnp.int32, (1, Q_BLOCK), 1) for c in chains]

    n_cmp_pad = kc_ref.shape[3]
    n_slc = ovl_ref.shape[0]
    n_cmp = min(n_cmp_pad, _round_up((t_end - CMP_BLOCK) // CMP_STRIDE + 1, 16))
    n_cand = min(n_slc, t_end // SLC_BLOCK)
    n_blk = min(n_slc, _round_up(n_cand, 8))
    cstart = lax.broadcasted_iota(jnp.int32, (n_cmp, Q_BLOCK), 0) * CMP_STRIDE
    centre = cstart.astype(F32) + (CMP_BLOCK - 1) / 2
    jj = lax.broadcasted_iota(jnp.int32, (n_blk, Q_BLOCK), 0)
    kc, vct, ovl = kc_ref[0, 0, 0, 0:n_cmp, :], vct_ref[0, 0, 0], ovl_ref[0:n_blk, :]
    o_cmp, member = [], []
    for c in chains:
        centre_dist = t_row[c].astype(F32) - centre
        valid = (cstart + (CMP_BLOCK - 1)) <= t_row[c]
        sc = _dot(kc, qt[c])
        p_heads = []
        for h in range(hg):
            s = _head(sc, h) * c1 - slope2[h] * centre_dist
            s = jnp.where(valid, s, -jnp.inf)
            m = jnp.max(s, axis=0, keepdims=True)
            m = jnp.where(jnp.isfinite(m), m, 0.0)
            e = jnp.where(valid, jnp.exp2(s - m), 0.0)
            p = e / jnp.maximum(jnp.sum(e, axis=0, keepdims=True), F32_TINY)
            p_heads.append(_pad_rows(p.astype(BF16), n_cmp_pad))
        o_cmp.append(_dot(vct, jnp.concatenate(p_heads, axis=1)))

        imp = _dot(ovl, p_heads[0])
        for h in range(1, hg):
            imp += _dot(ovl, p_heads[h])
        blk_t = t_row[c] // SLC_BLOCK
        forced = (jj == 0) | (jj == blk_t) | (jj == blk_t - 1)
        imp = jnp.where(forced, SEL_FORCE, jnp.where(jj <= blk_t, imp, -SEL_FORCE))
        rank = jnp.zeros((n_blk, Q_BLOCK), jnp.int32)
        for i in range(n_cand):
            ri = imp[i:i + 1, :]
            before = (ri > imp) | ((ri == imp) & (jj > i))
            rank += before.astype(jnp.int32)
        member.append(_pad_rows(jnp.where(rank < min(SLC_TOP_N, n_slc), 1.0, 0.0).astype(BF16), n_slc))

    key_iota = {}

    def key_local(n):
        if n not in key_iota:
            key_iota[n] = lax.broadcasted_iota(jnp.int32, (n, Q_BLOCK), 0)
        return key_iota[n]

    def tiles(lo, hi):
        return [(p, min(KV_TILE, hi - p)) for p in range(lo, hi, KV_TILE)]

    streams = []
    for c in chains:
        streams.append([("slc", c, ks_ref, vst_ref, lo, n) for lo, n in tiles(0, t0[c] + Q_BLOCK)])
    for c in chains:
        streams.append([("win", c, kw_ref, vwt_ref, lo, n)
                        for lo, n in tiles(max(t0[c] - WINDOW, 0), t0[c] + Q_BLOCK)])
    items = [s[i] for i in range(max(len(s) for s in streams)) for s in streams if i < len(s)]

    def scores(item):
        _, c, k_ref, _, lo, n = item
        return _dot(k_ref[lo:lo + n, :], qt[c])

    def finish(state, item, sc):
        branch, c, _, vt_ref, lo, n = item
        dist = t_row[c] - (lo + key_local(n))
        masks = []
        if branch == "slc":
            masks.append(_dot(exp_ref[lo:lo + n, :], member[c]) > 0.5)
        if lo + n - 1 > t0[c]:
            masks.append(dist >= 0)
        if branch == "win" and t0[c] + Q_BLOCK - 1 - lo >= WINDOW:
            masks.append(dist < WINDOW)
        mask = functools.reduce(jnp.logical_and, masks) if masks else None
        dist_f = dist.astype(F32)
        m_new, l_new, alpha, p_t = [], [], [], []
        for h in range(hg):
            s = _head(sc, h) * c1 - slope2[h] * dist_f
            if mask is not None:
                s = jnp.where(mask, s, NEG_BIG)
            m_n = jnp.max(s, axis=0, keepdims=True)
            if state is not None:
                m_o = _head(state[0], h)
                m_n = jnp.maximum(m_o, m_n)
                a = jnp.exp2(m_o - m_n)
                alpha.append(a)
            p = jnp.exp2(s - m_n)
            l_n = jnp.sum(p, axis=0, keepdims=True)
            if state is not None:
                l_n = a * _head(state[1], h) + l_n
            m_new.append(m_n)
            l_new.append(l_n)
            p_t.append(p.astype(BF16))
        acc = _dot(vt_ref[:, lo:lo + n], jnp.concatenate(p_t, axis=1))
        if state is not None:
            acc = jnp.concatenate(alpha, axis=1) * state[2] + acc
        return jnp.concatenate(m_new, axis=1), jnp.concatenate(l_new, axis=1), acc

    states = {}
    sc_next = scores(items[0])
    for i, item in enumerate(items):
        sc = sc_next
        if i + 1 < len(items):
            sc_next = scores(items[i + 1])
        states[item[:2]] = finish(states.get(item[:2]), item, sc)

    for c in chains:
        gt = gt_ref[0, :, c * Q_BLOCK:(c + 1) * Q_BLOCK]

        def gate(br):
            return jnp.concatenate([gt[br * hg + h:br * hg + h + 1, :] for h in range(hg)], axis=1)

        (_, l_s, acc_s), (_, l_w, acc_w) = states[("slc", c)], states[("win", c)]
        o = (gate(0) * o_cmp[c] + (gate(1) / jnp.maximum(l_s, F32_TINY)) * acc_s
             + (gate(2) / jnp.maximum(l_w, F32_TINY)) * acc_w)
        o_ref[c * Q_BLOCK:(c + 1) * Q_BLOCK, :] = jnp.concatenate(
            [_head(o, h).T for h in range(hg)], axis=1).astype(o_ref.dtype)


def _nsa(qt, kv_cmp, k_slc, vt_slc, k_win, vt_win, gates_t, slopes2, ovl, expand, batch, seq):
    g = NSA_KV_GROUPS
    gw = HEADS_PER_GROUP * NSA_HEAD_DIM
    nq = seq // Q_STEP
    n_cmp_pad = kv_cmp.shape[3]
    out = jnp.zeros((batch * seq, NSA_WIDTH), BF16)
    for step in range(nq):
        per_seq = 1
        while seq // (2 * per_seq) >= (step + 1) * Q_STEP:
            per_seq *= 2
        n_keys = seq // per_seq
        k_spec = pl.BlockSpec((n_keys, NSA_HEAD_DIM), lambda b, gg, per_seq=per_seq: (b * per_seq, gg))
        vt_spec = pl.BlockSpec((NSA_HEAD_DIM, n_keys), lambda b, gg, per_seq=per_seq: (gg, b * per_seq))
        row_blk = lambda b, gg, step=step: (b * nq + step, gg)
        out = pl.pallas_call(
            functools.partial(_nsa_kernel, step=step),
            grid=(batch, g),
            in_specs=[pl.BlockSpec(memory_space=pltpu.SMEM),
                      pl.BlockSpec((gw, Q_STEP), lambda b, gg, step=step: (gg, b * nq + step)),
                      pl.BlockSpec((1, 1, 1, n_cmp_pad, NSA_HEAD_DIM), lambda b, gg: (0, b, gg, 0, 0)),
                      pl.BlockSpec((1, 1, 1, NSA_HEAD_DIM, n_cmp_pad), lambda b, gg: (1, b, gg, 0, 0)),
                      k_spec, vt_spec, k_spec, vt_spec,
                      pl.BlockSpec((1, GATE_ROWS, Q_STEP), lambda b, gg, step=step: (gg, 0, b * nq + step)),
                      pl.BlockSpec(ovl.shape, lambda b, gg: (0, 0)),
                      pl.BlockSpec(expand.shape, lambda b, gg: (0, 0)),
                      pl.BlockSpec(memory_space=pl.ANY)],
            out_specs=pl.BlockSpec((Q_STEP, gw), row_blk),
            out_shape=jax.ShapeDtypeStruct((batch * seq, NSA_WIDTH), BF16),
            input_output_aliases={11: 0},
            compiler_params=_params("parallel", "parallel"),
            name=f"nsa{step}",
        )(slopes2, qt, kv_cmp, kv_cmp, k_slc, vt_slc, k_win, vt_win, gates_t, ovl, expand, out)
    return out


def _ret_kernel(cd_ref, q_ref, k_ref, v_ref, g_ref, gain_ref, decay_ref, zeta_ref, xi_ref, o_ref, state_ref):
    @pl.when(pl.program_id(1) == 0)
    def _():
        state_ref[...] = jnp.zeros_like(state_ref)

    dk, dv = RET_KEY_DIM, RET_VAL_DIM
    for h in range(RET_HEADS):
        q = q_ref[:, h * dk:(h + 1) * dk]
        k = k_ref[:, h * dk:(h + 1) * dk]
        v = v_ref[:, h * dv:(h + 1) * dv]
        sc = _dot_nt(q.astype(BF16), k.astype(BF16)) * decay_ref[h]
        inner = _dot(sc.astype(BF16), v)
        st = state_ref[h]
        cross = _dot((q * xi_ref[h]).astype(BF16), st.astype(BF16))
        kz_t = (k * zeta_ref[h]).T.astype(BF16)
        state_ref[h] = st * cd_ref[h] + _dot(kz_t, v)
        y = inner + cross
        yc = y - jnp.mean(y, axis=-1, keepdims=True)
        yn = yc * lax.rsqrt(jnp.mean(yc * yc, axis=-1, keepdims=True) + EPS) * gain_ref[:, h * dv:(h + 1) * dv]
        o_ref[:, h * dv:(h + 1) * dv] = (_silu(g_ref[:, h * dv:(h + 1) * dv]) * yn).astype(o_ref.dtype)


def _retention(q, k, v, g, gain, consts, batch, seq):
    chunk_decay, decay, zeta, xi = consts
    c = RET_CHUNK
    nc = seq // c
    row = lambda b, ci: (b * nc + ci, 0)
    whole3 = lambda b, ci: (0, 0, 0)
    return pl.pallas_call(
        _ret_kernel,
        grid=(batch, nc),
        in_specs=[pl.BlockSpec(memory_space=pltpu.SMEM),
                  pl.BlockSpec((c, RET_QK_WIDTH), row),
                  pl.BlockSpec((c, RET_QK_WIDTH), row),
                  pl.BlockSpec((c, RET_V_WIDTH), row),
                  pl.BlockSpec((c, RET_V_WIDTH), row),
                  pl.BlockSpec((1, RET_V_WIDTH), lambda b, ci: (0, 0)),
                  pl.BlockSpec((RET_HEADS, c, c), whole3),
                  pl.BlockSpec((RET_HEADS, c, 1), whole3),
                  pl.BlockSpec((RET_HEADS, c, 1), whole3)],
        out_specs=pl.BlockSpec((c, RET_V_WIDTH), row),
        out_shape=jax.ShapeDtypeStruct((batch * seq, RET_V_WIDTH), BF16),
        scratch_shapes=[pltpu.VMEM((RET_HEADS, RET_KEY_DIM, RET_VAL_DIM), F32)],
        compiler_params=_params("parallel", "arbitrary"),
        name="retention",
    )(chunk_decay, q, k, v, g, gain, decay, zeta, xi)


def _retention_consts():
    c = RET_CHUNK
    log_gamma = jnp.log1p(-jnp.exp2(-5.0 - jnp.arange(RET_HEADS, dtype=F32)))
    n = jnp.arange(c, dtype=F32)
    diff = n[:, None] - n[None, :]
    decay = jnp.where(diff >= 0, jnp.exp(log_gamma[:, None, None] * jnp.maximum(diff, 0.0)), 0.0)
    zeta = jnp.exp(log_gamma[:, None] * (c - 1 - n)[None, :])[:, :, None]
    xi = jnp.exp(log_gamma[:, None] * (n + 1.0)[None, :])[:, :, None]
    chunk_decay = jnp.exp(log_gamma * c)
    return chunk_decay, decay, zeta, xi


def _merge_kernel(x_ref, gt_ref, on_ref, or_ref, u_ref, wga_ref, wgb_ref, wpn_ref, wpr_ref, wo_ref, o_ref):
    j = pl.program_id(1)

    @pl.when(j == 0)
    def _():
        o_ref[...] = jnp.zeros_like(o_ref)

    u = u_ref[...]
    a = _dot(on_ref[...], wpn_ref[...])
    b = _dot(or_ref[...], wpr_ref[...])
    merged = jax.nn.sigmoid(_dot(u, wga_ref[...])) * a + jax.nn.sigmoid(_dot(u, wgb_ref[...])) * b
    o_ref[...] += _dot(merged.astype(BF16), wo_ref[...])

    @pl.when(j == pl.num_programs(1) - 1)
    def _():
        o_ref[...] = x_ref[...] + gt_ref[0] * o_ref[...]


def _merge(x2, gate, o_nsa, o_ret, u, wga, wgb, wpn, wpr, wo, seq):
    t, d = x2.shape
    tm, tn = 512, 512
    per_b = seq // tm
    row = lambda i, j: (i, 0)
    col = lambda i, j: (0, j)
    return pl.pallas_call(
        _merge_kernel,
        grid=(t // tm, d // tn),
        in_specs=[pl.BlockSpec((tm, d), row),
                  pl.BlockSpec((1, 1, d), lambda i, j: (i // per_b, 0, 0)),
                  pl.BlockSpec((tm, o_nsa.shape[1]), row),
                  pl.BlockSpec((tm, o_ret.shape[1]), row),
                  pl.BlockSpec((tm, d), row),
                  pl.BlockSpec((d, tn), col),
                  pl.BlockSpec((d, tn), col),
                  pl.BlockSpec((wpn.shape[0], tn), col),
                  pl.BlockSpec((wpr.shape[0], tn), col),
                  pl.BlockSpec((tn, d), lambda i, j: (j, 0))],
        out_specs=pl.BlockSpec((tm, d), row),
        out_shape=jax.ShapeDtypeStruct((t, d), F32),
        compiler_params=_params("parallel", "arbitrary"),
        name="merge",
    )(x2, gate, o_nsa, o_ret, u, wga, wgb, wpn, wpr, wo)


def _nsa_tables(seq):
    n_cmp = (seq - CMP_BLOCK) // CMP_STRIDE + 1
    n_cmp_pad = seq // CMP_STRIDE
    n_slc = seq // SLC_BLOCK
    cs = (np.arange(n_cmp) * CMP_STRIDE)[:, None]
    js = (np.arange(n_slc) * SLC_BLOCK)[None, :]
    overlap = np.clip(np.minimum(cs + CMP_BLOCK, js + SLC_BLOCK) - np.maximum(cs, js), 0, None) / CMP_BLOCK
    ovl = np.zeros((n_slc, n_cmp_pad), np.float32)
    ovl[:, :n_cmp] = overlap.T
    expand = (np.arange(seq)[:, None] // SLC_BLOCK == np.arange(n_slc)[None, :]).astype(np.float32)
    slopes2 = jnp.exp2(-8.0 * jnp.arange(1, NSA_HEADS + 1, dtype=F32) / NSA_HEADS) * LOG2E
    return jnp.asarray(ovl, BF16), jnp.asarray(expand, BF16), slopes2


def kernel(x, c, w_ada, b_ada, g_norm, w_ffn_gate, w_ffn_up, w_ffn_down, w_in, g_qk, cmp_pos, cmp_w1, cmp_b1,
           cmp_w2, ret_gn_gain, w_proj_nsa, w_proj_ret, w_out):
    batch, seq, d = x.shape
    depth = w_ada.shape[0]
    t = batch * seq
    x2 = x.reshape(t, d)
    c_pad = jnp.pad(c, ((0, 8 - batch), (0, 0)))
    ovl, expand, slopes2 = _nsa_tables(seq)
    ret_consts = _retention_consts()
    hg = HEADS_PER_GROUP

    for l in range(depth):
        ada = _ada(c_pad, w_ada[l], b_ada[l][None, :])[:batch].reshape(batch, N_ADA, 1, d)
        sh1, sc1, gt1, sh2, sc2, gt2, sh3, sc3, gt3 = [ada[:, i] for i in range(N_ADA)]

        x2 = _ffn(x2, sh1, sc1, gt1, g_norm[l, 0][None, :], w_ffn_gate[l], w_ffn_up[l], w_ffn_down[l], 0, seq)

        u = _mod(x2, sh2, sc2, g_norm[l, 1][None, :], seq)
        w = w_in[l]
        off = [0]

        def take(width):
            w_part = w[:, off[0]:off[0] + width]
            off[0] += width
            return w_part.astype(BF16)

        gq = g_qk[l]
        qt_nsa = _proj_t(u, take(NSA_WIDTH), BF16, gq[0], name="proj_q")
        cmp_raw = _proj(u, take(2 * KV_WIDTH), F32, name="proj_cmp_raw")
        k_slc = _proj(u, take(KV_WIDTH), BF16, "headnorm", gq[2][None, :], name="proj_k_slc")
        vt_slc = _proj_t(u, take(KV_WIDTH), BF16, name="proj_v_slc")
        k_win = _proj(u, take(KV_WIDTH), BF16, "headnorm", gq[3][None, :], name="proj_k_win")
        vt_win = _proj_t(u, take(KV_WIDTH), BF16, name="proj_v_win")
        w_gl = take(3 * NSA_HEADS).reshape(d, 3, NSA_KV_GROUPS, hg).transpose(0, 2, 1, 3)
        w_gl = jnp.pad(w_gl.reshape(d, NSA_KV_GROUPS, 3 * hg),
                       ((0, 0), (0, 0), (0, LANES - 3 * hg))).reshape(d, NSA_KV_GROUPS * LANES)
        gates = _proj(u, w_gl, F32, "sigmoid", name="proj_gates")
        gates_t = gates.reshape(t, NSA_KV_GROUPS, LANES)[:, :, :GATE_ROWS].transpose(1, 2, 0)
        q_r = _proj(u, take(RET_QK_WIDTH), F32, name="proj_q_ret")
        k_r = _proj(u, take(RET_QK_WIDTH), F32, "keyscale", name="proj_k_ret")
        v_r = _proj(u, take(RET_V_WIDTH), BF16, name="proj_v_ret")
        g_r = _proj(u, take(RET_V_WIDTH), F32, name="proj_g_ret")
        w_ga, w_gb = take(d), take(d)

        kv_cmp = _compress(cmp_raw.reshape(batch, seq, 2 * KV_WIDTH), cmp_pos[l], cmp_w1[l].astype(BF16),
                           cmp_b1[l][:, None, :], cmp_w2[l].astype(BF16), gq[1][None, :])
        o_nsa = _nsa(qt_nsa, kv_cmp, k_slc, vt_slc, k_win, vt_win, gates_t, slopes2, ovl, expand, batch, seq)
        o_ret = _retention(q_r, k_r, v_r, g_r, ret_gn_gain[l].reshape(1, RET_V_WIDTH), ret_consts, batch, seq)

        x2 = _merge(x2, gt2, o_nsa, o_ret, u, w_ga, w_gb, w_proj_nsa[l].astype(BF16), w_proj_ret[l].astype(BF16),
                    w_out[l].astype(BF16), seq)

        x2 = _ffn(x2, sh3, sc3, gt3, g_norm[l, 2][None, :], w_ffn_gate[l], w_ffn_up[l], w_ffn_down[l], 1, seq)

    return x2.reshape(batch, seq, d)
```
